```python
import math
import jax, jax.numpy as jnp
from jax import lax
import numpy as np

D_MODEL = 1024
BATCH = 2
SEQ = 8192
DEPTH = 1

CHUNK = 64
N_META = 16
QBLOCK = 128
ROPE_THETA = 10000.0
EPS = 1e-6

DA_DIM = 64
DA_VDIM = 2 * DA_DIM
DA_WIDTH = D_MODEL // 2
DA_HEADS = DA_WIDTH // DA_VDIM
GDN_DK = 64
GDN_DV = 64
GDN_WIDTH = D_MODEL - DA_WIDTH
GDN_HEADS = GDN_WIDTH // GDN_DV
CONV_K = 4
MIX_WIDTH = DA_WIDTH + GDN_WIDTH

COL_QA = 2 * DA_HEADS * DA_DIM
COL_KA = 2 * DA_HEADS * DA_DIM
COL_VA = DA_WIDTH
COL_QKVB = 2 * GDN_HEADS * GDN_DK + GDN_WIDTH
COL_ZB = GDN_WIDTH
COL_AB = GDN_HEADS
COL_BB = GDN_HEADS
PROJ = COL_QA + COL_KA + COL_VA + COL_QKVB + COL_ZB + COL_AB + COL_BB
SPLITS = list(np.cumsum([COL_QA, COL_KA, COL_VA, COL_QKVB, COL_ZB, COL_AB])[:].tolist())

N_GROUPS = 4
EXPERTS_PER_GROUP = 8
N_EXPERTS = N_GROUPS * EXPERTS_PER_GROUP
TOP_K = 2
D_EXPERT = D_MODEL // 4

kernel_name = "hymba_diffattn_gdn_hmoe_block"


def rmsnorm(x, gain):
    xf = x.astype(jnp.float32)
    y = xf * lax.rsqrt(jnp.mean(xf * xf, axis=-1, keepdims=True) + EPS)
    return (y * gain.astype(jnp.float32)).astype(x.dtype)


def l2norm(x):
    xf = x.astype(jnp.float32)
    return xf * lax.rsqrt(jnp.sum(xf * xf, axis=-1, keepdims=True) + EPS)


def chunk_ids(n):
    p = jnp.arange(n, dtype=jnp.int32)
    return jnp.where(p < N_META, 0, 1 + (p - N_META) // CHUNK)


def rope(x, cos, sin):
    x1, x2 = jnp.split(x.astype(jnp.float32), 2, axis=-1)
    c, s = cos[None, :, None, :], sin[None, :, None, :]
    return jnp.concatenate([x1 * c - x2 * s, x2 * c + x1 * s], axis=-1).astype(x.dtype)


def masked_softmax(s, visible):
    return jax.nn.softmax(jnp.where(visible, s, -jnp.inf), axis=-1)


def diff_attention(q, k, v, lam):
    B, L = q.shape[0], q.shape[1]
    n_blocks = -(-L // QBLOCK)
    Lp = n_blocks * QBLOCK

    def pad(t):
        return jnp.pad(t, ((0, 0), (0, Lp - L)) + ((0, 0),) * (t.ndim - 2))

    q, k, v = pad(q), pad(k), pad(v)
    scale = DA_DIM ** -0.5
    q1 = (q[:, :, :, 0] * scale).transpose(0, 2, 1, 3)
    q2 = (q[:, :, :, 1] * scale).transpose(0, 2, 1, 3)
    k1 = k[:, :, :, 0].transpose(0, 2, 1, 3)
    k2 = k[:, :, :, 1].transpose(0, 2, 1, 3)
    vh = v.transpose(0, 2, 1, 3)
    cid = chunk_ids(Lp)

    def block(i):
        start = i * QBLOCK
        q1b = lax.dynamic_slice_in_dim(q1, start, QBLOCK, axis=2)
        q2b = lax.dynamic_slice_in_dim(q2, start, QBLOCK, axis=2)
        qc = lax.dynamic_slice_in_dim(cid, start, QBLOCK)
        visible = cid[None, :] <= qc[:, None]
        p1 = masked_softmax(jnp.einsum('bhqd,bhkd->bhqk', q1b, k1).astype(jnp.float32), visible)
        p2 = masked_softmax(jnp.einsum('bhqd,bhkd->bhqk', q2b, k2).astype(jnp.float32), visible)
        p = p1 - lam * p2
        return jnp.einsum('bhqk,bhkd->bhqd', p.astype(vh.dtype), vh)

    out = lax.map(block, jnp.arange(n_blocks))
    return out.transpose(1, 0, 3, 2, 4).reshape(B, Lp, DA_HEADS, DA_VDIM)[:, :L]


def causal_conv(x, w):
    C = x.shape[-1]
    return lax.conv_general_dilated(
        x, w[:, None, :].astype(x.dtype), window_strides=(1,), padding=[(CONV_K - 1, 0)],
        dimension_numbers=('NWC', 'WIO', 'NWC'), feature_group_count=C)


def gated_delta_rule(q, k, v, g, beta):
    B, L, H, DK = q.shape
    DV = v.shape[-1]
    n = -(-L // CHUNK)
    Lp = n * CHUNK
    f32 = jnp.float32

    def chunks(t):
        t = jnp.pad(t.astype(f32), ((0, 0), (0, Lp - L)) + ((0, 0),) * (t.ndim - 2))
        return jnp.moveaxis(t.reshape((B, n, CHUNK) + t.shape[2:]), 1, 0)

    q = chunks(q).transpose(0, 1, 3, 2, 4)
    k = chunks(k).transpose(0, 1, 3, 2, 4)
    v = chunks(v).transpose(0, 1, 3, 2, 4)
    g = chunks(g).transpose(0, 1, 3, 2)
    beta = chunks(beta).transpose(0, 1, 3, 2)

    causal = jnp.tril(jnp.ones((CHUNK, CHUNK), dtype=bool))
    strict = jnp.tril(jnp.ones((CHUNK, CHUNK), dtype=bool), -1)
    gc = jnp.cumsum(g, axis=-1)
    diff = gc[..., :, None] - gc[..., None, :]
    decay = jnp.where(causal, jnp.exp(jnp.where(causal, diff, 0.0)), 0.0)
    k_beta = k * beta[..., None]
    a_kk = jnp.where(strict, jnp.einsum('nbhcd,nbhsd->nbhcs', k_beta, k) * decay, 0.0)
    m = a_kk + jnp.eye(CHUNK, dtype=f32)
    rhs = jnp.concatenate([v * beta[..., None], k_beta * jnp.exp(gc)[..., None]], axis=-1)
    sol = lax.linalg.triangular_solve(m, rhs, left_side=True, lower=True, unit_diagonal=True)
    u, w = sol[..., :DV], sol[..., DV:]
    a_qk = jnp.where(causal, jnp.einsum('nbhcd,nbhsd->nbhcs', q, k) * decay, 0.0)
    q_g = q * jnp.exp(gc)[..., None]
    g_last = gc[..., -1]
    k_g = k * jnp.exp(g_last[..., None] - gc)[..., None]

    def step(S, xs):
        u_c, w_c, q_c, aqk_c, k_c, gl = xs
        v_new = u_c - jnp.einsum('bhck,bhkv->bhcv', w_c, S)
        o = jnp.einsum('bhck,bhkv->bhcv', q_c, S) + jnp.einsum('bhcs,bhsv->bhcv', aqk_c, v_new)
        S = S * jnp.exp(gl)[..., None, None] + jnp.einsum('bhck,bhcv->bhkv', k_c, v_new)
        return S, o

    S0 = jnp.zeros((B, H, DK, DV), f32)
    _, o = lax.scan(step, S0, (u, w, q_g, a_qk, k_g, g_last))
    return o.transpose(1, 0, 3, 2, 4).reshape(B, Lp, H, DV)[:, :L]


def hier_moe(u, w_group, b_group, w_router, b_router, w_gate, w_up, w_down):
    B, L, D = u.shape
    t = u.reshape(B * L, D)
    glog = (t @ w_group).astype(jnp.float32) + b_group.astype(jnp.float32)
    gprob = jax.nn.softmax(glog, axis=-1)
    gsel = jnp.argmax(glog, axis=-1)
    psel = jnp.take_along_axis(gprob, gsel[:, None], axis=-1)
    elog = ((t @ w_router).astype(jnp.float32) + b_router.astype(jnp.float32)).reshape(-1, N_GROUPS, EXPERTS_PER_GROUP)
    elog_g = jnp.take_along_axis(elog, gsel[:, None, None], axis=1)[:, 0]
    topv, topi = lax.top_k(jax.nn.softmax(elog_g, axis=-1), TOP_K)
    wts = topv / jnp.sum(topv, axis=-1, keepdims=True) * psel
    eidx = gsel[:, None] * EXPERTS_PER_GROUP + topi
    combine = jnp.einsum('tk,tke->te', wts, jax.nn.one_hot(eidx, N_EXPERTS, dtype=jnp.float32)).astype(t.dtype)
    y = jnp.zeros_like(t)
    for gi in range(N_GROUPS):
        sl = slice(gi * EXPERTS_PER_GROUP, (gi + 1) * EXPERTS_PER_GROUP)
        hg = jnp.einsum('td,edf->tef', t, w_gate[sl])
        hu = jnp.einsum('td,edf->tef', t, w_up[sl])
        act = jax.nn.silu(hg) * hu * combine[:, sl, None]
        y = y + jnp.einsum('tef,efd->td', act, w_down[sl])
    return y.reshape(B, L, D)


def setup_inputs(seed: int = 0) -> dict:
    key = jax.random.key(seed)
    ks = jax.random.split(key, 24)
    f32 = jnp.float32
    nrm = lambda k, shape, s: jax.random.normal(k, shape, f32) * s
    dt = jnp.exp(jax.random.uniform(ks[10], (DEPTH, GDN_HEADS), f32) * (math.log(0.1) - math.log(0.001)) + math.log(0.001))
    return {
        "x": nrm(ks[0], (BATCH, SEQ, D_MODEL), 1.0),
        "meta": nrm(ks[1], (N_META, D_MODEL), 1.0),
        "norm_mix": 1.0 + nrm(ks[2], (DEPTH, D_MODEL), 0.02),
        "w_in": nrm(ks[3], (DEPTH, D_MODEL, PROJ), D_MODEL ** -0.5),
        "lambda_q1": nrm(ks[4], (DEPTH, DA_DIM), 0.1),
        "lambda_k1": nrm(ks[5], (DEPTH, DA_DIM), 0.1),
        "lambda_q2": nrm(ks[6], (DEPTH, DA_DIM), 0.1),
        "lambda_k2": nrm(ks[7], (DEPTH, DA_DIM), 0.1),
        "diff_norm": 1.0 + nrm(ks[8], (DEPTH, DA_VDIM), 0.02),
        "conv_w": nrm(ks[9], (DEPTH, CONV_K, COL_QKVB), CONV_K ** -0.5),
        "a_log": jnp.log(jax.random.uniform(ks[11], (DEPTH, GDN_HEADS), f32, 1.0, 16.0)),
        "dt_bias": dt + jnp.log(-jnp.expm1(-dt)),
        "gdn_norm": 1.0 + nrm(ks[12], (DEPTH, GDN_DV), 0.02),
        "w_out": nrm(ks[13], (DEPTH, MIX_WIDTH, D_MODEL), MIX_WIDTH ** -0.5),
        "norm_ffn": 1.0 + nrm(ks[14], (DEPTH, D_MODEL), 0.02),
        "w_group": nrm(ks[15], (DEPTH, D_MODEL, N_GROUPS), D_MODEL ** -0.5),
        "b_group": nrm(ks[16], (DEPTH, N_GROUPS), 0.01),
        "w_router": nrm(ks[17], (DEPTH, D_MODEL, N_EXPERTS), D_MODEL ** -0.5),
        "b_router": nrm(ks[18], (DEPTH, N_EXPERTS), 0.01),
        "w_gate": nrm(ks[19], (DEPTH, N_EXPERTS, D_MODEL, D_EXPERT), D_MODEL ** -0.5),
        "w_up": nrm(ks[20], (DEPTH, N_EXPERTS, D_MODEL, D_EXPERT), D_MODEL ** -0.5),
        "w_down": nrm(ks[21], (DEPTH, N_EXPERTS, D_EXPERT, D_MODEL), D_EXPERT ** -0.5),
        "norm_final": 1.0 + nrm(ks[22], (D_MODEL,), 0.02),
    }


def reference(x, meta, norm_mix, w_in, lambda_q1, lambda_k1, lambda_q2, lambda_k2, diff_norm, conv_w,
              a_log, dt_bias, gdn_norm, w_out, norm_ffn, w_group, b_group, w_router, b_router,
              w_gate, w_up, w_down, norm_final):
    B, S, D = x.shape
    L = S + N_META
    h = jnp.concatenate([jnp.broadcast_to(meta[None].astype(x.dtype), (B, N_META, D)), x], axis=1)
    pos = jnp.arange(L, dtype=jnp.float32)
    inv_freq = ROPE_THETA ** (-jnp.arange(0, DA_DIM, 2, dtype=jnp.float32) / DA_DIM)
    ang = pos[:, None] * inv_freq[None, :]
    cos, sin = jnp.cos(ang), jnp.sin(ang)

    for l in range(DEPTH):
        lam_init = 0.8 - 0.6 * math.exp(-0.3 * l)
        u = rmsnorm(h, norm_mix[l])
        proj = u @ w_in[l]
        qa, ka, va, qkvb, zb, ab, bb = jnp.split(proj, SPLITS, axis=-1)

        qa = rope(qa.reshape(B, L, 2 * DA_HEADS, DA_DIM), cos, sin).reshape(B, L, DA_HEADS, 2, DA_DIM)
        ka = rope(ka.reshape(B, L, 2 * DA_HEADS, DA_DIM), cos, sin).reshape(B, L, DA_HEADS, 2, DA_DIM)
        lam = (jnp.exp(jnp.sum(lambda_q1[l].astype(jnp.float32) * lambda_k1[l].astype(jnp.float32)))
               - jnp.exp(jnp.sum(lambda_q2[l].astype(jnp.float32) * lambda_k2[l].astype(jnp.float32))) + lam_init)
        oa = diff_attention(qa, ka, va.reshape(B, L, DA_HEADS, DA_VDIM), lam)
        oa = rmsnorm(oa, diff_norm[l]) * (1.0 - lam_init)

        qkvb = jax.nn.silu(causal_conv(qkvb, conv_w[l]))
        qb, kb, vb = jnp.split(qkvb, [GDN_HEADS * GDN_DK, 2 * GDN_HEADS * GDN_DK], axis=-1)
        qb = l2norm(qb.reshape(B, L, GDN_HEADS, GDN_DK)) * (GDN_DK ** -0.5)
        kb = l2norm(kb.reshape(B, L, GDN_HEADS, GDN_DK))
        vb = vb.reshape(B, L, GDN_HEADS, GDN_DV)
        beta = jax.nn.sigmoid(bb.astype(jnp.float32))
        g = -jnp.exp(a_log[l].astype(jnp.float32)) * jax.nn.softplus(ab.astype(jnp.float32) + dt_bias[l].astype(jnp.float32))
        ob = gated_delta_rule(qb, kb, vb, g, beta).astype(h.dtype)
        ob = rmsnorm(ob, gdn_norm[l]) * jax.nn.silu(zb.reshape(B, L, GDN_HEADS, GDN_DV))

        mix = jnp.concatenate([oa.reshape(B, L, DA_WIDTH), ob.reshape(B, L, GDN_WIDTH)], axis=-1)
        h = h + mix @ w_out[l]

        h = h + hier_moe(rmsnorm(h, norm_ffn[l]), w_group[l], b_group[l], w_router[l], b_router[l],
                         w_gate[l], w_up[l], w_down[l])

    h = rmsnorm(h, norm_final)
    return h[:, N_META:]
```

```python
import functools
import math

import jax
import jax.numpy as jnp
from jax import lax
from jax.experimental import pallas as pl
from jax.experimental.pallas import tpu as pltpu

F32 = jnp.float32
BF16 = jnp.bfloat16
HIGHEST = lax.Precision.HIGHEST

D_MODEL = 1024
N_META = 16
CHUNK = 64
EPS = 1e-6
ROPE_THETA = 10000.0
HEAD_DIM = 64
SLAB = 128
DA_HEADS = 4
GDN_HEADS = 8
MIX_HALF = 512
N_GROUPS = 4
EXPERTS_PER_GROUP = 8
N_EXPERTS = 32
D_EXPERT = 256
CONV_K = 4
LAM_INIT = 0.8 - 0.6 * math.exp(-0.3 * 0)
ROUTE_LANES = 128
NEG_BIG = -1e30
VMEM_LIMIT = 56 * 1024 * 1024


def _dot(a, b, precision=None):
    return jnp.dot(a, b, preferred_element_type=F32, precision=precision)


def _dot_nt(a, b):
    return lax.dot_general(a, b, (((1,), (1,)), ((), ())), preferred_element_type=F32)


def _dot_tn(a, b):
    return lax.dot_general(a, b, (((0,), (0,)), ((), ())), preferred_element_type=F32)


def _rms_scale(x):
    return x * lax.rsqrt(jnp.mean(x * x, axis=-1, keepdims=True) + EPS)


def _silu(x):
    return x * jax.nn.sigmoid(x)


def _group_sumsq(x, g_ref):
    sq = x * x
    hi = sq.astype(BF16)
    lo = (sq - hi.astype(F32)).astype(BF16)
    g = g_ref[...]
    return _dot(hi, g) + _dot(lo, g)


def _proj_attn_kernel(x_ref, gain_ref, w_ref, cos_ref, sina_ref, sinb_ref, qa_ref, ka_ref, va_ref):
    u = (_rms_scale(x_ref[0]) * gain_ref[...]).astype(BF16)
    proj = _dot(u, w_ref[...])
    cos, sina, sinb = cos_ref[...], sina_ref[...], sinb_ref[...]
    for s in range(2 * DA_HEADS):
        xs = proj[:, SLAB * s:SLAB * (s + 1)]
        r = xs * cos + pltpu.roll(xs, SLAB - 32, 1) * sina + pltpu.roll(xs, 32, 1) * sinb
        if s < DA_HEADS:
            qa_ref[0, :, SLAB * s:SLAB * (s + 1)] = (r * (HEAD_DIM ** -0.5)).astype(BF16)
        else:
            t = s - DA_HEADS
            ka_ref[0, :, SLAB * t:SLAB * (t + 1)] = r.astype(BF16)
    va_ref[0] = proj[:, 2 * MIX_HALF:].astype(BF16)


def _proj_attn(x, gain, w_a, cos, sina, sinb, tb):
    b, s, d = x.shape
    nb = s // tb
    slab_out = jax.ShapeDtypeStruct((b, s, MIX_HALF), BF16)
    full = lambda shape: pl.BlockSpec(shape, lambda i, j: (0,) * len(shape))
    tok = pl.BlockSpec((1, tb, MIX_HALF), lambda i, j: (i, j, 0))
    tab = pl.BlockSpec((tb, SLAB), lambda i, j: (j, 0))
    return pl.pallas_call(
        _proj_attn_kernel,
        grid=(b, nb),
        in_specs=[pl.BlockSpec((1, tb, d), lambda i, j: (i, j, 0)), full((1, d)), full(w_a.shape), tab, tab, tab],
        out_specs=[tok, tok, tok],
        out_shape=[slab_out, slab_out, slab_out],
        compiler_params=pltpu.CompilerParams(dimension_semantics=("parallel", "parallel"),
                                             vmem_limit_bytes=VMEM_LIMIT),
        name="proj_attn",
    )(x, gain, w_a, cos, sina, sinb)


def _proj_gdn_kernel(x_ref, halo_ref, gain_ref, wb_ref, wz_ref, wg_ref, convw_ref, alog_ref, dtb_ref, g_ref,
                     q_ref, k_ref, v_ref, z_ref, gb_ref, beta_ref):
    halo_rows = halo_ref.shape[2]
    xe = jnp.concatenate([halo_ref[0, 0], x_ref[0]], axis=0)
    ue = (_rms_scale(xe) * gain_ref[...]).astype(BF16)
    u = ue[halo_rows:]
    cw = convw_ref[...]
    outs = (q_ref, k_ref, v_ref)
    for part in range(3):
        cols = slice(MIX_HALF * part, MIX_HALF * (part + 1))
        pb = _dot(ue, wb_ref[:, cols])
        c = cw[:, cols]
        y = pb[halo_rows:] * c[CONV_K - 1:CONV_K]
        for back in range(1, CONV_K):
            tap = CONV_K - 1 - back
            y = y + pltpu.roll(pb, back, 0)[halo_rows:] * c[tap:tap + 1]
        y = _silu(y)
        if part < 2:
            y = y * lax.rsqrt(_group_sumsq(y, g_ref) + EPS)
            if part == 0:
                y = y * (HEAD_DIM ** -0.5)
        outs[part][0] = y.astype(BF16)
    z_ref[0] = _silu(_dot(u, wz_ref[...])).astype(BF16)
    gates = _dot(u, wg_ref[...])
    ab = gates[:, :MIX_HALF] + dtb_ref[...]
    softplus = jnp.maximum(ab, 0.0) + jnp.log1p(jnp.exp(-jnp.abs(ab)))
    gb_ref[0] = -jnp.exp(alog_ref[...]) * softplus
    beta_ref[0] = jax.nn.sigmoid(gates[:, MIX_HALF:]).astype(BF16)


def _proj_gdn(x, halo, gain, w_b, w_z, w_g, conv_w, alog_rep, dtb_rep, g512, tb):
    b, s, d = x.shape
    nb = s // tb
    full = lambda shape: pl.BlockSpec(shape, lambda i, j: (0,) * len(shape))
    tok = pl.BlockSpec((1, tb, MIX_HALF), lambda i, j: (i, j, 0))
    bf = jax.ShapeDtypeStruct((b, s, MIX_HALF), BF16)
    return pl.pallas_call(
        _proj_gdn_kernel,
        grid=(b, nb),
        in_specs=[pl.BlockSpec((1, tb, d), lambda i, j: (i, j, 0)),
                  pl.BlockSpec((1, 1) + halo.shape[2:], lambda i, j: (i, j, 0, 0)),
                  full((1, d)), full(w_b.shape), full(w_z.shape), full(w_g.shape), full(conv_w.shape),
                  full((1, MIX_HALF)), full((1, MIX_HALF)), full(g512.shape)],
        out_specs=[tok] * 6,
        out_shape=[bf, bf, bf, bf, jax.ShapeDtypeStruct((b, s, MIX_HALF), F32), bf],
        compiler_params=pltpu.CompilerParams(dimension_semantics=("parallel", "parallel"),
                                             vmem_limit_bytes=VMEM_LIMIT),
        name="proj_gdn",
    )(x, halo, gain, w_b, w_z, w_g, conv_w, alog_rep, dtb_rep, g512)


def _attn_kernel(q_ref, k_ref, v_ref, km_ref, vm_ref, lq1_ref, lk1_ref, lq2_ref, lk2_ref, dn_ref, o_ref,
                 m_scr, acc_scr, *, qb):
    i = pl.program_id(2)
    rows = 2 * qb
    q = q_ref[0]
    lane = lax.broadcasted_iota(jnp.int32, (qb, SLAB), 1)
    zero = jnp.zeros_like(q)
    qq = jnp.concatenate([jnp.where(lane < HEAD_DIM, q, zero), jnp.where(lane < HEAD_DIM, zero, q)], axis=0)

    def accumulate(s, v, first):
        v1 = jnp.concatenate([v, jnp.ones_like(v)], axis=1)
        m_cur = jnp.max(s, axis=1, keepdims=True)
        if first:
            m_new = m_cur
            p = jnp.exp(s - m_new)
            acc_scr[...] = _dot(p.astype(BF16), v1)
        else:
            m_prev = m_scr[:, :1]
            m_new = jnp.maximum(m_prev, m_cur)
            p = jnp.exp(s - m_new)
            acc_scr[...] = acc_scr[...] * jnp.exp(m_prev - m_new) + _dot(p.astype(BF16), v1)
        m_scr[...] = jnp.broadcast_to(m_new, (rows, SLAB))

    accumulate(_dot_nt(qq, km_ref[0]), vm_ref[0], True)

    def full_block(j, carry):
        start = pl.multiple_of(j * qb, qb)
        accumulate(_dot_nt(qq, k_ref[0, pl.ds(start, qb), :]), v_ref[0, pl.ds(start, qb), :], False)
        return carry

    lax.fori_loop(0, i, full_block, 0)

    start = pl.multiple_of(i * qb, qb)
    s = _dot_nt(qq, k_ref[0, pl.ds(start, qb), :])
    q_chunk = (lax.broadcasted_iota(jnp.int32, (rows, qb), 0) % qb) // CHUNK
    k_chunk = lax.broadcasted_iota(jnp.int32, (rows, qb), 1) // CHUNK
    accumulate(jnp.where(k_chunk <= q_chunk, s, NEG_BIG), v_ref[0, pl.ds(start, qb), :], False)

    acc = acc_scr[...]
    o1 = acc[:qb, :SLAB] / acc[:qb, SLAB:]
    o2 = acc[qb:, :SLAB] / acc[qb:, SLAB:]
    lam = (jnp.exp(jnp.sum(lq1_ref[...] * lk1_ref[...], axis=1, keepdims=True))
           - jnp.exp(jnp.sum(lq2_ref[...] * lk2_ref[...], axis=1, keepdims=True)) + LAM_INIT)
    o = o1 - lam * o2
    o_ref[0] = (_rms_scale(o) * dn_ref[...] * (1.0 - LAM_INIT)).astype(BF16)


def _diff_attention(qa, ka, va, ka_meta, va_meta, lq1, lk1, lq2, lk2, diff_norm, qb):
    b, s, _ = qa.shape
    nq = s // qb
    vec = lambda n: pl.BlockSpec((1, n), lambda bi, h, i: (0, 0))
    return pl.pallas_call(
        functools.partial(_attn_kernel, qb=qb),
        grid=(b, DA_HEADS, nq),
        in_specs=[pl.BlockSpec((1, qb, SLAB), lambda bi, h, i: (bi, i, h)),
                  pl.BlockSpec((1, s, SLAB), lambda bi, h, i: (bi, 0, h)),
                  pl.BlockSpec((1, s, SLAB), lambda bi, h, i: (bi, 0, h)),
                  pl.BlockSpec((1, N_META, SLAB), lambda bi, h, i: (0, 0, h)),
                  pl.BlockSpec((1, N_META, SLAB), lambda bi, h, i: (0, 0, h)),
                  vec(HEAD_DIM), vec(HEAD_DIM), vec(HEAD_DIM), vec(HEAD_DIM), vec(SLAB)],
        out_specs=pl.BlockSpec((1, qb, SLAB), lambda bi, h, i: (bi, i, h)),
        out_shape=jax.ShapeDtypeStruct((b, s, MIX_HALF), BF16),
        scratch_shapes=[pltpu.VMEM((2 * qb, SLAB), F32), pltpu.VMEM((2 * qb, 2 * SLAB), F32)],
        compiler_params=pltpu.CompilerParams(dimension_semantics=("parallel", "parallel", "arbitrary"),
                                             vmem_limit_bytes=VMEM_LIMIT),
        name="diff_attention",
    )(qa, ka, va, ka_meta, va_meta, lq1, lk1, lq2, lk2, diff_norm)


def _gdn_kernel(q_ref, k_ref, v_ref, z_ref, g_ref, beta_ref, gn_ref, g512_ref, s0_ref, o_ref, sfin_ref,
                s_scr, o_scr, *, n_chunks):
    j = pl.program_id(1)
    n_pairs = GDN_HEADS // 2

    @pl.when(j == 0)
    def _():
        s_scr[...] = s0_ref[...]

    row = lax.broadcasted_iota(jnp.int32, (CHUNK, SLAB), 0)
    lane = lax.broadcasted_iota(jnp.int32, (CHUNK, SLAB), 1)
    col = lane % CHUNK
    head_a = lane < HEAD_DIM
    tri_incl = row >= col
    tri_strict = row > col
    eye = (row == col).astype(F32)
    ltri = (lax.broadcasted_iota(jnp.int32, (CHUNK, CHUNK), 0)
            >= lax.broadcasted_iota(jnp.int32, (CHUNK, CHUNK), 1)).astype(F32)
    bd_mask = ((lax.broadcasted_iota(jnp.int32, (SLAB, SLAB), 0) < HEAD_DIM)
               == (lax.broadcasted_iota(jnp.int32, (SLAB, SLAB), 1) < HEAD_DIM))

    def block_diag(x):
        xb = x.astype(BF16)
        zero = jnp.zeros_like(xb)
        return jnp.concatenate([jnp.where(head_a, xb, zero), jnp.where(head_a, zero, xb)], axis=0)

    def pair_mm(x, y):
        return _dot(x.astype(BF16), block_diag(y))

    for c in range(n_chunks):
        rows = slice(CHUNK * c, CHUNK * (c + 1))
        for p in range(n_pairs):
            cols = slice(SLAB * p, SLAB * (p + 1))
            q = q_ref[0, rows, cols].astype(F32)
            k = k_ref[0, rows, cols].astype(F32)
            v = v_ref[0, rows, cols].astype(F32)
            beta = beta_ref[0, rows, cols].astype(F32)
            g = g_ref[0, rows, cols]

            g_col = _dot(ltri, g, precision=HIGHEST)
            g_row = jnp.sum(jnp.where(row <= col, g, 0.0), axis=0, keepdims=True)
            decay = jnp.where(tri_incl, jnp.exp(jnp.where(tri_incl, g_col - g_row, 0.0)), 0.0)
            k_beta = k * beta
            k_bd = block_diag(k)
            a = jnp.where(tri_strict, _dot_nt(k_beta.astype(BF16), k_bd) * decay, 0.0)
            a_qk = jnp.where(tri_incl, _dot_nt(q.astype(BF16), k_bd) * decay, 0.0)

            t = eye - a
            pw = a
            for _ in range(5):
                pw = pair_mm(pw, pw)
                t = t + pair_mm(t, pw)

            e_col = jnp.exp(g_col)
            u = pair_mm(t, v * beta)
            w = pair_mm(t, k_beta * e_col)
            g_last = g_col[CHUNK - 1:CHUNK, :]
            k_g = k * jnp.exp(g_last - g_col)
            q_g = q * e_col

            s_bf = s_scr[p].astype(BF16)
            v_new = u - _dot(w.astype(BF16), s_bf)
            o = _dot(q_g.astype(BF16), s_bf) + pair_mm(a_qk, v_new)
            upd = _dot_tn(k_g.astype(BF16), v_new.astype(BF16))
            s_scr[p] = s_scr[p] * jnp.exp(g_last) + jnp.where(bd_mask, upd, 0.0)
            o_scr[rows, cols] = o

    o = o_scr[...]
    ss = _group_sumsq(o, g512_ref)
    y = o * lax.rsqrt(ss * (1.0 / HEAD_DIM) + EPS) * gn_ref[...]
    o_ref[0] = (y * z_ref[0].astype(F32)).astype(BF16)

    @pl.when(j == pl.num_programs(1) - 1)
    def _():
        sfin_ref[0] = s_scr[...]


def _gated_delta(q, k, v, z, gb, beta, gn_rep, g512, s0, tbg):
    b, s, _ = q.shape
    nb = s // tbg
    n_pairs = GDN_HEADS // 2
    tok = pl.BlockSpec((1, tbg, MIX_HALF), lambda bi, j: (bi, j, 0))
    full = lambda shape: pl.BlockSpec(shape, lambda bi, j: (0,) * len(shape))
    return pl.pallas_call(
        functools.partial(_gdn_kernel, n_chunks=tbg // CHUNK),
        grid=(b, nb),
        in_specs=[tok] * 6 + [full((1, MIX_HALF)), full(g512.shape), full(s0.shape)],
        out_specs=[tok, pl.BlockSpec((1, n_pairs, SLAB, SLAB), lambda bi, j: (bi, 0, 0, 0))],
        out_shape=[jax.ShapeDtypeStruct((b, s, MIX_HALF), BF16),
                   jax.ShapeDtypeStruct((b, n_pairs, SLAB, SLAB), F32)],
        scratch_shapes=[pltpu.VMEM((n_pairs, SLAB, SLAB), F32), pltpu.VMEM((tbg, MIX_HALF), F32)],
        compiler_params=pltpu.CompilerParams(dimension_semantics=("parallel", "arbitrary"),
                                             vmem_limit_bytes=VMEM_LIMIT),
        name="gated_delta",
    )(q, k, v, z, gb, beta, gn_rep, g512, s0)


def _out_router_kernel(x_ref, oa_ref, ob_ref, woa_ref, wob_ref, gain_ref, wr_ref, br_ref, h_ref, t_ref, comb_ref):
    h = x_ref[0] + _dot(oa_ref[0], woa_ref[...]) + _dot(ob_ref[0], wob_ref[...])
    h_ref[0] = h
    t = _rms_scale(h) * gain_ref[...]
    t_ref[0] = t.astype(BF16)
    logits = _dot(t, wr_ref[...], precision=HIGHEST) + br_ref[...]
    lane = lax.broadcasted_iota(jnp.int32, logits.shape, 1)
    no_lane = jnp.int32(ROUTE_LANES)

    def first_argmax(vals, vmax):
        return jnp.min(jnp.where(vals == vmax, lane, no_lane), axis=1, keepdims=True)

    gl = jnp.where(lane < N_GROUPS, logits, NEG_BIG)
    gmax = jnp.max(gl, axis=1, keepdims=True)
    gsel = first_argmax(gl, gmax)
    psel = 1.0 / jnp.sum(jnp.exp(gl - gmax), axis=1, keepdims=True)
    e_lane = lane - N_GROUPS
    in_group = (e_lane >= 0) & (e_lane < N_EXPERTS) & (lax.shift_right_arithmetic(e_lane, 3) == gsel)
    el = jnp.where(in_group, logits, NEG_BIG)
    m1 = jnp.max(el, axis=1, keepdims=True)
    i1 = first_argmax(el, m1)
    el2 = jnp.where(lane == i1, NEG_BIG, el)
    m2 = jnp.max(el2, axis=1, keepdims=True)
    i2 = first_argmax(el2, m2)
    denom = jnp.sum(jnp.exp(el - m1), axis=1, keepdims=True)
    p1 = 1.0 / denom
    p2 = jnp.exp(m2 - m1) / denom
    w1 = p1 / (p1 + p2) * psel
    w2 = p2 / (p1 + p2) * psel
    comb_ref[0] = jnp.where(lane == i1, w1, 0.0) + jnp.where(lane == i2, w2, 0.0)


def _out_router(x, oa, ob, wo_a, wo_b, gain, w_route, b_route, tb):
    b, s, d = x.shape
    nb = s // tb
    full = lambda shape: pl.BlockSpec(shape, lambda i, j: (0,) * len(shape))
    tokd = pl.BlockSpec((1, tb, d), lambda i, j: (i, j, 0))
    tokh = pl.BlockSpec((1, tb, MIX_HALF), lambda i, j: (i, j, 0))
    return pl.pallas_call(
        _out_router_kernel,
        grid=(b, nb),
        in_specs=[tokd, tokh, tokh, full(wo_a.shape), full(wo_b.shape), full((1, d)), full(w_route.shape),
                  full((1, ROUTE_LANES))],
        out_specs=[tokd, tokd, pl.BlockSpec((1, tb, ROUTE_LANES), lambda i, j: (i, j, 0))],
        out_shape=[jax.ShapeDtypeStruct((b, s, d), F32), jax.ShapeDtypeStruct((b, s, d), BF16),
                   jax.ShapeDtypeStruct((b, s, ROUTE_LANES), F32)],
        compiler_params=pltpu.CompilerParams(dimension_semantics=("parallel", "parallel"),
                                             vmem_limit_bytes=VMEM_LIMIT),
        name="out_router",
    )(x, oa, ob, wo_a, wo_b, gain, w_route, b_route)


def _moe_kernel(t_ref, comb_ref, h_ref, wg_ref, wu_ref, wd_ref, gain_ref, o_ref, acc_scr, *, e_step):
    e = pl.program_id(1)

    @pl.when(e == 0)
    def _():
        acc_scr[...] = jnp.zeros_like(acc_scr)

    t = t_ref[...]
    comb = comb_ref[...]
    lane = lax.broadcasted_iota(jnp.int32, comb.shape, 1)
    acc = acc_scr[...]
    for s in range(e_step):
        wt = jnp.sum(jnp.where(lane == N_GROUPS + e * e_step + s, comb, 0.0), axis=1, keepdims=True)
        act = _silu(_dot(t, wg_ref[s])) * _dot(t, wu_ref[s]) * wt
        acc = acc + _dot(act.astype(BF16), wd_ref[s])
    acc_scr[...] = acc

    @pl.when(e == pl.num_programs(1) - 1)
    def _():
        o_ref[...] = _rms_scale(h_ref[...] + acc_scr[...]) * gain_ref[...]


def _moe(t, comb, h, w_gate, w_up, w_down, gain, tm, e_step):
    n, d = t.shape
    ne = N_EXPERTS // e_step
    tokd = pl.BlockSpec((tm, d), lambda i, e: (i, 0))
    return pl.pallas_call(
        functools.partial(_moe_kernel, e_step=e_step),
        grid=(n // tm, ne),
        in_specs=[tokd, pl.BlockSpec((tm, ROUTE_LANES), lambda i, e: (i, 0)), tokd,
                  pl.BlockSpec((e_step, d, D_EXPERT), lambda i, e: (e, 0, 0)),
                  pl.BlockSpec((e_step, d, D_EXPERT), lambda i, e: (e, 0, 0)),
                  pl.BlockSpec((e_step, D_EXPERT, d), lambda i, e: (e, 0, 0)),
                  pl.BlockSpec((1, d), lambda i, e: (0, 0))],
        out_specs=tokd,
        out_shape=jax.ShapeDtypeStruct((n, d), F32),
        scratch_shapes=[pltpu.VMEM((tm, d), F32)],
        compiler_params=pltpu.CompilerParams(dimension_semantics=("parallel", "arbitrary"),
                                             vmem_limit_bytes=VMEM_LIMIT),
        name="moe_experts",
    )(t, comb, h, w_gate, w_up, w_down, gain)


def _rope_tables(length):
    pos = jnp.arange(length, dtype=F32)
    inv_freq = ROPE_THETA ** (-jnp.arange(0, HEAD_DIM, 2, dtype=F32) / HEAD_DIM)
    ang = pos[:, None] * inv_freq[None, :]
    cos, sin = jnp.cos(ang), jnp.sin(ang)
    zero = jnp.zeros_like(sin)
    cos_t = jnp.tile(cos, (1, 4))
    sina_t = jnp.tile(jnp.concatenate([-sin, zero], axis=1), (1, 2))
    sinb_t = jnp.tile(jnp.concatenate([zero, sin], axis=1), (1, 2))
    return cos_t, sina_t, sinb_t


def _block(total, want):
    blk = min(total, want)
    assert total % blk == 0, (total, blk)
    return blk


def kernel(x, meta, norm_mix, w_in, lambda_q1, lambda_k1, lambda_q2, lambda_k2, diff_norm, conv_w, a_log, dt_bias,
           gdn_norm, w_out, norm_ffn, w_group, b_group, w_router, b_router, w_gate, w_up, w_down, norm_final):
    b, s, d = x.shape
    assert d == D_MODEL and meta.shape == (N_META, D_MODEL) and s % CHUNK == 0
    assert norm_mix.shape[0] == 1, "single-layer block"
    l = 0
    rep = lambda a: jnp.repeat(a, HEAD_DIM, axis=-1)

    w = w_in[l]
    w_a = w[:, :3 * MIX_HALF].astype(BF16)
    w_b = w[:, 3 * MIX_HALF:6 * MIX_HALF].astype(BF16)
    w_z = w[:, 6 * MIX_HALF:7 * MIX_HALF].astype(BF16)
    c_ab = 7 * MIX_HALF
    w_g = jnp.concatenate([rep(w[:, c_ab:c_ab + GDN_HEADS]), rep(w[:, c_ab + GDN_HEADS:])], axis=1).astype(BF16)
    alog_rep = rep(a_log[l])[None]
    dtb_rep = rep(dt_bias[l])[None]
    gn_rep = jnp.tile(gdn_norm[l], GDN_HEADS)[None]
    g512 = (jnp.arange(MIX_HALF)[:, None] // HEAD_DIM == jnp.arange(MIX_HALF)[None, :] // HEAD_DIM).astype(BF16)
    gain_mix = norm_mix[l][None]
    wo_a = w_out[l][:MIX_HALF].astype(BF16)
    wo_b = w_out[l][MIX_HALF:].astype(BF16)
    pad_r = ROUTE_LANES - N_GROUPS - N_EXPERTS
    w_route = jnp.pad(jnp.concatenate([w_group[l], w_router[l]], axis=1), ((0, 0), (0, pad_r)))
    b_route = jnp.pad(jnp.concatenate([b_group[l], b_router[l]]), (0, pad_r))[None]
    lam_vecs = [v[l][None] for v in (lambda_q1, lambda_k1, lambda_q2, lambda_k2)]
    cos_t, sina_t, sinb_t = _rope_tables(N_META + s)

    meta3 = meta[None]
    _, ka_m, va_m = _proj_attn(meta3, gain_mix, w_a, cos_t[:N_META], sina_t[:N_META], sinb_t[:N_META], N_META)
    zero_halo = jnp.zeros((1, 1, N_META, d), F32)
    gdn_m = _proj_gdn(meta3, zero_halo, gain_mix, w_b, w_z, w_g, conv_w[l], alog_rep, dtb_rep, g512, N_META)
    pad_m = lambda a: jnp.pad(a, ((0, 0), (0, CHUNK - N_META), (0, 0)))
    s_zero = jnp.zeros((GDN_HEADS // 2, SLAB, SLAB), F32)
    _, s_meta = _gated_delta(*[pad_m(a) for a in gdn_m], gn_rep, g512, s_zero, CHUNK)

    tb_a = _block(s, 1024)
    qa, ka, va = _proj_attn(x, gain_mix, w_a, cos_t[N_META:], sina_t[N_META:], sinb_t[N_META:], tb_a)
    tb_g = _block(s, 512)
    nb_g = s // tb_g
    tails = x.reshape(b, nb_g, tb_g, d)[:, :-1, tb_g - N_META:, :]
    halo = jnp.concatenate([jnp.broadcast_to(meta[None, None], (b, 1, N_META, d)), tails], axis=1)
    gdn_f = _proj_gdn(x, halo, gain_mix, w_b, w_z, w_g, conv_w[l], alog_rep, dtb_rep, g512, tb_g)

    oa = _diff_attention(qa, ka, va, ka_m, va_m, *lam_vecs, diff_norm[l][None], _block(s, 512))
    ob, _ = _gated_delta(*gdn_f, gn_rep, g512, s_meta[0], _block(s, 256))

    h1, t, comb = _out_router(x, oa, ob, wo_a, wo_b, norm_ffn[l][None], w_route, b_route, _block(s, 512))

    n = b * s
    out = _moe(t.reshape(n, d), comb.reshape(n, ROUTE_LANES), h1.reshape(n, d), w_gate[l].astype(BF16),
               w_up[l].astype(BF16), w_down[l].astype(BF16), norm_final[None], _block(n, 1024), 2)
    return out.reshape(b, s, d)
```

```python
import functools
import math

import jax
import jax.numpy as jnp
from jax import lax
from jax.experimental import pallas as pl
from jax.experimental.pallas import tpu as pltpu

F32 = jnp.float32
BF16 = jnp.bfloat16
HIGHEST = lax.Precision.HIGHEST

D_MODEL = 1024
N_META = 16
CHUNK = 64
EPS = 1e-6
ROPE_THETA = 10000.0
HEAD_DIM = 64
SLAB = 128
DA_HEADS = 4
GDN_HEADS = 8
MIX_HALF = 512
N_GROUPS = 4
EXPERTS_PER_GROUP = 8
N_EXPERTS = 32
D_EXPERT = 256
CONV_K = 4
LAM_INIT = 0.8 - 0.6 * math.exp(-0.3 * 0)
ROUTE_LANES = 128
NEG_BIG = -1e30
ONES_ROWS = 16
VMEM_LIMIT = 56 * 1024 * 1024


def _dot(a, b, precision=None):
    return jnp.dot(a, b, preferred_element_type=F32, precision=precision)


def _dot_nt(a, b):
    return lax.dot_general(a, b, (((1,), (1,)), ((), ())), preferred_element_type=F32)


def _dot_tn(a, b):
    return lax.dot_general(a, b, (((0,), (0,)), ((), ())), preferred_element_type=F32)


def _rms_scale(x):
    return x * lax.rsqrt(jnp.mean(x * x, axis=-1, keepdims=True) + EPS)


def _silu(x):
    return x * jax.nn.sigmoid(x)


def _group_sumsq(x, g_ref):
    sq = x * x
    hi = sq.astype(BF16)
    lo = (sq - hi.astype(F32)).astype(BF16)
    g = g_ref[...]
    return _dot(hi, g) + _dot(lo, g)


def _proj_attn_kernel(x_ref, gain_ref, w_ref, wvt_ref, cos_ref, sina_ref, sinb_ref, qa_ref, ka_ref, vt_ref):
    u = (_rms_scale(x_ref[0]) * gain_ref[...]).astype(BF16)
    proj = _dot(u, w_ref[...])
    vt_ref[0] = _dot_nt(wvt_ref[...], u).astype(BF16)
    cos, sina, sinb = cos_ref[...], sina_ref[...], sinb_ref[...]
    for s in range(2 * DA_HEADS):
        xs = proj[:, SLAB * s:SLAB * (s + 1)]
        r = xs * cos + pltpu.roll(xs, SLAB - 32, 1) * sina + pltpu.roll(xs, 32, 1) * sinb
        if s < DA_HEADS:
            qa_ref[0, :, SLAB * s:SLAB * (s + 1)] = (r * (HEAD_DIM ** -0.5)).astype(BF16)
        else:
            t = s - DA_HEADS
            ka_ref[0, :, SLAB * t:SLAB * (t + 1)] = r.astype(BF16)


def _proj_attn(x, gain, w_qk, w_vt, cos, sina, sinb, tb):
    b, s, d = x.shape
    nb = s // tb
    slab_out = jax.ShapeDtypeStruct((b, s, MIX_HALF), BF16)
    full = lambda shape: pl.BlockSpec(shape, lambda i, j: (0,) * len(shape))
    tok = pl.BlockSpec((1, tb, MIX_HALF), lambda i, j: (i, j, 0))
    tab = pl.BlockSpec((tb, SLAB), lambda i, j: (j, 0))
    return pl.pallas_call(
        _proj_attn_kernel,
        grid=(b, nb),
        in_specs=[pl.BlockSpec((1, tb, d), lambda i, j: (i, j, 0)), full((1, d)), full(w_qk.shape),
                  full(w_vt.shape), tab, tab, tab],
        out_specs=[tok, tok, pl.BlockSpec((1, MIX_HALF, tb), lambda i, j: (i, 0, j))],
        out_shape=[slab_out, slab_out, jax.ShapeDtypeStruct((b, MIX_HALF, s), BF16)],
        compiler_params=pltpu.CompilerParams(dimension_semantics=("parallel", "parallel"),
                                             vmem_limit_bytes=VMEM_LIMIT),
        name="proj_attn",
    )(x, gain, w_qk, w_vt, cos, sina, sinb)


def _proj_gdn_kernel(x_ref, halo_ref, gain_ref, wb_ref, wz_ref, wg_ref, convw_ref, alog_ref, dtb_ref, g_ref,
                     q_ref, k_ref, v_ref, z_ref, gb_ref, beta_ref):
    halo_rows = halo_ref.shape[2]
    xe = jnp.concatenate([halo_ref[0, 0], x_ref[0]], axis=0)
    ue = (_rms_scale(xe) * gain_ref[...]).astype(BF16)
    u = ue[halo_rows:]
    cw = convw_ref[...]
    outs = (q_ref, k_ref, v_ref)
    for part in range(3):
        cols = slice(MIX_HALF * part, MIX_HALF * (part + 1))
        pb = _dot(ue, wb_ref[:, cols])
        c = cw[:, cols]
        y = pb[halo_rows:] * c[CONV_K - 1:CONV_K]
        for back in range(1, CONV_K):
            tap = CONV_K - 1 - back
            y = y + pltpu.roll(pb, back, 0)[halo_rows:] * c[tap:tap + 1]
        y = _silu(y)
        if part < 2:
            y = y * lax.rsqrt(_group_sumsq(y, g_ref) + EPS)
            if part == 0:
                y = y * (HEAD_DIM ** -0.5)
        outs[part][0] = y.astype(BF16)
    z_ref[0] = _silu(_dot(u, wz_ref[...])).astype(BF16)
    gates = _dot(u, wg_ref[...])
    ab = gates[:, :MIX_HALF] + dtb_ref[...]
    softplus = jnp.maximum(ab, 0.0) + jnp.log1p(jnp.exp(-jnp.abs(ab)))
    gb_ref[0] = -jnp.exp(alog_ref[...]) * softplus
    beta_ref[0] = jax.nn.sigmoid(gates[:, MIX_HALF:]).astype(BF16)


def _proj_gdn(x, halo, gain, w_b, w_z, w_g, conv_w, alog_rep, dtb_rep, g512, tb):
    b, s, d = x.shape
    nb = s // tb
    full = lambda shape: pl.BlockSpec(shape, lambda i, j: (0,) * len(shape))
    tok = pl.BlockSpec((1, tb, MIX_HALF), lambda i, j: (i, j, 0))
    bf = jax.ShapeDtypeStruct((b, s, MIX_HALF), BF16)
    return pl.pallas_call(
        _proj_gdn_kernel,
        grid=(b, nb),
        in_specs=[pl.BlockSpec((1, tb, d), lambda i, j: (i, j, 0)),
                  pl.BlockSpec((1, 1) + halo.shape[2:], lambda i, j: (i, j, 0, 0)),
                  full((1, d)), full(w_b.shape), full(w_z.shape), full(w_g.shape), full(conv_w.shape),
                  full((1, MIX_HALF)), full((1, MIX_HALF)), full(g512.shape)],
        out_specs=[tok] * 6,
        out_shape=[bf, bf, bf, bf, jax.ShapeDtypeStruct((b, s, MIX_HALF), F32), bf],
        compiler_params=pltpu.CompilerParams(dimension_semantics=("parallel", "parallel"),
                                             vmem_limit_bytes=VMEM_LIMIT),
        name="proj_gdn",
    )(x, halo, gain, w_b, w_z, w_g, conv_w, alog_rep, dtb_rep, g512)


def _attn_kernel(q_ref, k_ref, vt_ref, km_ref, vtm_ref, lq1_ref, lk1_ref, lq2_ref, lk2_ref, dn_ref, o_ref,
                 qqt_scr, m_scr, acc_scr, st_scr, *, qb, cw):
    i = pl.program_id(2)
    n_col = 2 * qb // cw
    qt = q_ref[0].astype(F32).T
    feat = lax.broadcasted_iota(jnp.int32, (SLAB, qb), 0)
    qqt_scr[...] = jnp.concatenate([jnp.where(feat < HEAD_DIM, qt, 0.0), jnp.where(feat < HEAD_DIM, 0.0, qt)],
                                   axis=1).astype(BF16)

    def with_ones(vt_blk):
        return jnp.concatenate([vt_blk, jnp.ones((ONES_ROWS, vt_blk.shape[1]), BF16)], axis=0)

    def softmax_pv(st, v1t, c, first, masked):
        cs = slice(cw * c, cw * (c + 1))
        if masked:
            k_chunk = lax.broadcasted_iota(jnp.int32, st.shape, 0) // CHUNK
            q_chunk = ((lax.broadcasted_iota(jnp.int32, st.shape, 1) + cw * c) % qb) // CHUNK
            st = jnp.where(k_chunk <= q_chunk, st, NEG_BIG)
        m_cur = jnp.max(st, axis=0, keepdims=True)
        if first:
            m_new = m_cur
            acc_scr[:, cs] = _dot(v1t, jnp.exp(st - m_new).astype(BF16))
        else:
            m_prev = m_scr[:, cs]
            m_new = jnp.maximum(m_prev, m_cur)
            pv = _dot(v1t, jnp.exp(st - m_new).astype(BF16))
            acc_scr[:, cs] = acc_scr[:, cs] * jnp.exp(m_prev - m_new) + pv
        m_scr[:, cs] = m_new

    def scores(start, c):
        return _dot(k_ref[0, pl.ds(start, qb), :], qqt_scr[:, cw * c:cw * (c + 1)])

    v1t_meta = with_ones(vtm_ref[0])
    for c in range(n_col):
        softmax_pv(_dot(km_ref[0], qqt_scr[:, cw * c:cw * (c + 1)]), v1t_meta, c, True, False)
    st_scr[...] = scores(0, 0)

    def key_block(j, masked):
        start = pl.multiple_of(j * qb, qb)
        v1t = with_ones(vt_ref[0, :, pl.ds(start, qb)])
        st = st_scr[...]
        for c in range(n_col):
            st_next = None
            if c + 1 < n_col:
                st_next = scores(start, c + 1)
            elif not masked:
                st_scr[...] = scores(start + qb, 0)
            softmax_pv(st, v1t, c, False, masked)
            st = st_next

    def full_block(j, carry):
        key_block(j, False)
        return carry

    lax.fori_loop(0, i, full_block, 0)
    key_block(i, True)

    acc = acc_scr[...]
    o1 = acc[:SLAB, :qb] / acc[SLAB:SLAB + 1, :qb]
    o2 = acc[:SLAB, qb:] / acc[SLAB:SLAB + 1, qb:]
    lam = (jnp.exp(jnp.sum(lq1_ref[...] * lk1_ref[...], axis=1, keepdims=True))
           - jnp.exp(jnp.sum(lq2_ref[...] * lk2_ref[...], axis=1, keepdims=True)) + LAM_INIT)
    o = (o1 - lam * o2).T
    o_ref[0] = (_rms_scale(o) * dn_ref[...] * (1.0 - LAM_INIT)).astype(BF16)


def _diff_attention(qa, ka, vat, ka_meta, vat_meta, lq1, lk1, lq2, lk2, diff_norm, qb, cw):
    b, s, _ = qa.shape
    nq = s // qb
    vec = lambda n: pl.BlockSpec((1, n), lambda bi, h, i: (0, 0))
    return pl.pallas_call(
        functools.partial(_attn_kernel, qb=qb, cw=cw),
        grid=(b, DA_HEADS, nq),
        in_specs=[pl.BlockSpec((1, qb, SLAB), lambda bi, h, i: (bi, i, h)),
                  pl.BlockSpec((1, s, SLAB), lambda bi, h, i: (bi, 0, h)),
                  pl.BlockSpec((1, SLAB, s), lambda bi, h, i: (bi, h, 0)),
                  pl.BlockSpec((1, N_META, SLAB), lambda bi, h, i: (0, 0, h)),
                  pl.BlockSpec((1, SLAB, N_META), lambda bi, h, i: (0, h, 0)),
                  vec(HEAD_DIM), vec(HEAD_DIM), vec(HEAD_DIM), vec(HEAD_DIM), vec(SLAB)],
        out_specs=pl.BlockSpec((1, qb, SLAB), lambda bi, h, i: (bi, i, h)),
        out_shape=jax.ShapeDtypeStruct((b, s, MIX_HALF), BF16),
        scratch_shapes=[pltpu.VMEM((SLAB, 2 * qb), BF16), pltpu.VMEM((1, 2 * qb), F32),
                        pltpu.VMEM((SLAB + ONES_ROWS, 2 * qb), F32), pltpu.VMEM((qb, cw), F32)],
        compiler_params=pltpu.CompilerParams(dimension_semantics=("parallel", "parallel", "arbitrary"),
                                             vmem_limit_bytes=VMEM_LIMIT),
        name="diff_attention",
    )(qa, ka, vat, ka_meta, vat_meta, lq1, lk1, lq2, lk2, diff_norm)


def _gdn_kernel(q_ref, k_ref, v_ref, z_ref, g_ref, beta_ref, gn_ref, g512_ref, s0_ref, o_ref, sfin_ref,
                s_scr, o_scr, *, n_chunks):
    j = pl.program_id(1)
    n_pairs = GDN_HEADS // 2

    @pl.when(j == 0)
    def _():
        s_scr[...] = s0_ref[...]

    row = lax.broadcasted_iota(jnp.int32, (CHUNK, SLAB), 0)
    lane = lax.broadcasted_iota(jnp.int32, (CHUNK, SLAB), 1)
    col = lane % CHUNK
    head_a = lane < HEAD_DIM
    tri_incl = row >= col
    tri_strict = row > col
    eye = (row == col).astype(F32)
    ltri = (lax.broadcasted_iota(jnp.int32, (CHUNK, CHUNK), 0)
            >= lax.broadcasted_iota(jnp.int32, (CHUNK, CHUNK), 1)).astype(F32)
    bd_mask = ((lax.broadcasted_iota(jnp.int32, (SLAB, SLAB), 0) < HEAD_DIM)
               == (lax.broadcasted_iota(jnp.int32, (SLAB, SLAB), 1) < HEAD_DIM))

    def block_diag(x):
        xb = x.astype(BF16)
        zero = jnp.zeros_like(xb)
        return jnp.concatenate([jnp.where(head_a, xb, zero), jnp.where(head_a, zero, xb)], axis=0)

    def pair_mm(x, y):
        return _dot(x.astype(BF16), block_diag(y))

    items = [(c, p) for c in range(n_chunks) for p in range(n_pairs)]
    rows_of = lambda c: slice(CHUNK * c, CHUNK * (c + 1))
    cols_of = lambda p: slice(SLAB * p, SLAB * (p + 1))
    load = lambda ref, c, p: ref[0, rows_of(c), cols_of(p)]

    g_col, decay, k_f, k_beta, k_bd, q_f, vb = {}, {}, {}, {}, {}, {}, {}
    for it in items:
        c, p = it
        g = load(g_ref, c, p)
        g_col[it] = _dot(ltri, g, precision=HIGHEST)
        g_row = jnp.sum(jnp.where(row <= col, g, 0.0), axis=0, keepdims=True)
        decay[it] = jnp.where(tri_incl, jnp.exp(jnp.where(tri_incl, g_col[it] - g_row, 0.0)), 0.0)
        beta = load(beta_ref, c, p).astype(F32)
        k_f[it] = load(k_ref, c, p).astype(F32)
        q_f[it] = load(q_ref, c, p).astype(F32)
        k_beta[it] = k_f[it] * beta
        vb[it] = load(v_ref, c, p).astype(F32) * beta
        k_bd[it] = block_diag(k_f[it])

    a, a_qk = {}, {}
    for it in items:
        lhs = jnp.concatenate([k_beta[it].astype(BF16), q_f[it].astype(BF16)], axis=0)
        kk = _dot_nt(lhs, k_bd[it])
        a[it] = jnp.where(tri_strict, kk[:CHUNK] * decay[it], 0.0)
        a_qk[it] = jnp.where(tri_incl, kk[CHUNK:] * decay[it], 0.0)

    t = {it: eye - a[it] for it in items}
    pw = dict(a)
    for _ in range(5):
        for it in items:
            pw[it] = pair_mm(pw[it], pw[it])
        for it in items:
            t[it] = t[it] + pair_mm(t[it], pw[it])

    u, w, k_g, q_g, e_last = {}, {}, {}, {}, {}
    for it in items:
        e_col = jnp.exp(g_col[it])
        rhs = jnp.concatenate([block_diag(vb[it]), block_diag(k_beta[it] * e_col)], axis=1)
        uw = _dot(t[it].astype(BF16), rhs)
        u[it], w[it] = uw[:, :SLAB], uw[:, SLAB:]
        g_last = g_col[it][CHUNK - 1:CHUNK, :]
        k_g[it] = (k_f[it] * jnp.exp(g_last - g_col[it])).astype(BF16)
        q_g[it] = q_f[it] * e_col
        e_last[it] = jnp.exp(g_last)

    state = [s_scr[p] for p in range(n_pairs)]
    for c in range(n_chunks):
        its = [(c, p) for p in range(n_pairs)]
        ws_qs = [_dot(jnp.concatenate([w[it], q_g[it]], axis=0).astype(BF16), state[it[1]].astype(BF16))
                 for it in its]
        v_new = [u[it] - sq[:CHUNK] for it, sq in zip(its, ws_qs)]
        for it, sq, vn in zip(its, ws_qs, v_new):
            o_scr[rows_of(c), cols_of(it[1])] = sq[CHUNK:] + pair_mm(a_qk[it], vn)
        for it, vn in zip(its, v_new):
            upd = _dot_tn(k_g[it], vn.astype(BF16))
            state[it[1]] = state[it[1]] * e_last[it] + jnp.where(bd_mask, upd, 0.0)
    for p in range(n_pairs):
        s_scr[p] = state[p]

    o = o_scr[...]
    ss = _group_sumsq(o, g512_ref)
    y = o * lax.rsqrt(ss * (1.0 / HEAD_DIM) + EPS) * gn_ref[...]
    o_ref[0] = (y * z_ref[0].astype(F32)).astype(BF16)

    @pl.when(j == pl.num_programs(1) - 1)
    def _():
        sfin_ref[0] = s_scr[...]


def _gated_delta(q, k, v, z, gb, beta, gn_rep, g512, s0, tbg):
    b, s, _ = q.shape
    nb = s // tbg
    n_pairs = GDN_HEADS // 2
    tok = pl.BlockSpec((1, tbg, MIX_HALF), lambda bi, j: (bi, j, 0))
    full = lambda shape: pl.BlockSpec(shape, lambda bi, j: (0,) * len(shape))
    return pl.pallas_call(
        functools.partial(_gdn_kernel, n_chunks=tbg // CHUNK),
        grid=(b, nb),
        in_specs=[tok] * 6 + [full((1, MIX_HALF)), full(g512.shape), full(s0.shape)],
        out_specs=[tok, pl.BlockSpec((1, n_pairs, SLAB, SLAB), lambda bi, j: (bi, 0, 0, 0))],
        out_shape=[jax.ShapeDtypeStruct((b, s, MIX_HALF), BF16),
                   jax.ShapeDtypeStruct((b, n_pairs, SLAB, SLAB), F32)],
        scratch_shapes=[pltpu.VMEM((n_pairs, SLAB, SLAB), F32), pltpu.VMEM((tbg, MIX_HALF), F32)],
        compiler_params=pltpu.CompilerParams(dimension_semantics=("parallel", "arbitrary"),
                                             vmem_limit_bytes=VMEM_LIMIT),
        name="gated_delta",
    )(q, k, v, z, gb, beta, gn_rep, g512, s0)


def _out_router_kernel(x_ref, oa_ref, ob_ref, woa_ref, wob_ref, gain_ref, wr_ref, br_ref, h_ref, t_ref, comb_ref):
    h = x_ref[0] + _dot(oa_ref[0], woa_ref[...]) + _dot(ob_ref[0], wob_ref[...])
    h_ref[0] = h
    t = _rms_scale(h) * gain_ref[...]
    t_ref[0] = t.astype(BF16)
    logits = _dot(t, wr_ref[...], precision=HIGHEST) + br_ref[...]
    lane = lax.broadcasted_iota(jnp.int32, logits.shape, 1)
    no_lane = jnp.int32(ROUTE_LANES)

    def first_argmax(vals, vmax):
        return jnp.min(jnp.where(vals == vmax, lane, no_lane), axis=1, keepdims=True)

    gl = jnp.where(lane < N_GROUPS, logits, NEG_BIG)
    gmax = jnp.max(gl, axis=1, keepdims=True)
    gsel = first_argmax(gl, gmax)
    psel = 1.0 / jnp.sum(jnp.exp(gl - gmax), axis=1, keepdims=True)
    e_lane = lane - N_GROUPS
    in_group = (e_lane >= 0) & (e_lane < N_EXPERTS) & (lax.shift_right_arithmetic(e_lane, 3) == gsel)
    el = jnp.where(in_group, logits, NEG_BIG)
    m1 = jnp.max(el, axis=1, keepdims=True)
    i1 = first_argmax(el, m1)
    el2 = jnp.where(lane == i1, NEG_BIG, el)
    m2 = jnp.max(el2, axis=1, keepdims=True)
    i2 = first_argmax(el2, m2)
    denom = jnp.sum(jnp.exp(el - m1), axis=1, keepdims=True)
    p1 = 1.0 / denom
    p2 = jnp.exp(m2 - m1) / denom
    w1 = p1 / (p1 + p2) * psel
    w2 = p2 / (p1 + p2) * psel
    comb_ref[0] = jnp.where(lane == i1, w1, 0.0) + jnp.where(lane == i2, w2, 0.0)


def _out_router(x, oa, ob, wo_a, wo_b, gain, w_route, b_route, tb):
    b, s, d = x.shape
    nb = s // tb
    full = lambda shape: pl.BlockSpec(shape, lambda i, j: (0,) * len(shape))
    tokd = pl.BlockSpec((1, tb, d), lambda i, j: (i, j, 0))
    tokh = pl.BlockSpec((1, tb, MIX_HALF), lambda i, j: (i, j, 0))
    return pl.pallas_call(
        _out_router_kernel,
        grid=(b, nb),
        in_specs=[tokd, tokh, tokh, full(wo_a.shape), full(wo_b.shape), full((1, d)), full(w_route.shape),
                  full((1, ROUTE_LANES))],
        out_specs=[tokd, tokd, pl.BlockSpec((1, tb, ROUTE_LANES), lambda i, j: (i, j, 0))],
        out_shape=[jax.ShapeDtypeStruct((b, s, d), F32), jax.ShapeDtypeStruct((b, s, d), BF16),
                   jax.ShapeDtypeStruct((b, s, ROUTE_LANES), F32)],
        compiler_params=pltpu.CompilerParams(dimension_semantics=("parallel", "parallel"),
                                             vmem_limit_bytes=VMEM_LIMIT),
        name="out_router",
    )(x, oa, ob, wo_a, wo_b, gain, w_route, b_route)


def _moe_kernel(t_ref, comb_ref, h_ref, wg_ref, wu_ref, wd_ref, gain_ref, o_ref, acc_scr, *, e_step):
    e = pl.program_id(1)

    @pl.when(e == 0)
    def _():
        acc_scr[...] = jnp.zeros_like(acc_scr)

    t = t_ref[...]
    comb = comb_ref[...]
    lane = lax.broadcasted_iota(jnp.int32, comb.shape, 1)
    acc = acc_scr[...]
    for s in range(e_step):
        wt = jnp.sum(jnp.where(lane == N_GROUPS + e * e_step + s, comb, 0.0), axis=1, keepdims=True)
        act = _silu(_dot(t, wg_ref[s])) * _dot(t, wu_ref[s]) * wt
        acc = acc + _dot(act.astype(BF16), wd_ref[s])
    acc_scr[...] = acc

    @pl.when(e == pl.num_programs(1) - 1)
    def _():
        o_ref[...] = _rms_scale(h_ref[...] + acc_scr[...]) * gain_ref[...]


def _moe(t, comb, h, w_gate, w_up, w_down, gain, tm, e_step):
    n, d = t.shape
    ne = N_EXPERTS // e_step
    tokd = pl.BlockSpec((tm, d), lambda i, e: (i, 0))
    return pl.pallas_call(
        functools.partial(_moe_kernel, e_step=e_step),
        grid=(n // tm, ne),
        in_specs=[tokd, pl.BlockSpec((tm, ROUTE_LANES), lambda i, e: (i, 0)), tokd,
                  pl.BlockSpec((e_step, d, D_EXPERT), lambda i, e: (e, 0, 0)),
                  pl.BlockSpec((e_step, d, D_EXPERT), lambda i, e: (e, 0, 0)),
                  pl.BlockSpec((e_step, D_EXPERT, d), lambda i, e: (e, 0, 0)),
                  pl.BlockSpec((1, d), lambda i, e: (0, 0))],
        out_specs=tokd,
        out_shape=jax.ShapeDtypeStruct((n, d), F32),
        scratch_shapes=[pltpu.VMEM((tm, d), F32)],
        compiler_params=pltpu.CompilerParams(dimension_semantics=("parallel", "arbitrary"),
                                             vmem_limit_bytes=VMEM_LIMIT),
        name="moe_experts",
    )(t, comb, h, w_gate, w_up, w_down, gain)


def _rope_tables(length):
    pos = jnp.arange(length, dtype=F32)
    inv_freq = ROPE_THETA ** (-jnp.arange(0, HEAD_DIM, 2, dtype=F32) / HEAD_DIM)
    ang = pos[:, None] * inv_freq[None, :]
    cos, sin = jnp.cos(ang), jnp.sin(ang)
    zero = jnp.zeros_like(sin)
    cos_t = jnp.tile(cos, (1, 4))
    sina_t = jnp.tile(jnp.concatenate([-sin, zero], axis=1), (1, 2))
    sinb_t = jnp.tile(jnp.concatenate([zero, sin], axis=1), (1, 2))
    return cos_t, sina_t, sinb_t


def _block(total, want):
    blk = min(total, want)
    assert total % blk == 0, (total, blk)
    return blk


def kernel(x, meta, norm_mix, w_in, lambda_q1, lambda_k1, lambda_q2, lambda_k2, diff_norm, conv_w, a_log, dt_bias,
           gdn_norm, w_out, norm_ffn, w_group, b_group, w_router, b_router, w_gate, w_up, w_down, norm_final):
    b, s, d = x.shape
    assert d == D_MODEL and meta.shape == (N_META, D_MODEL) and s % CHUNK == 0
    assert norm_mix.shape[0] == 1, "single-layer block"
    l = 0
    rep = lambda a: jnp.repeat(a, HEAD_DIM, axis=-1)

    w = w_in[l]
    w_qk = w[:, :2 * MIX_HALF].astype(BF16)
    w_vt = w[:, 2 * MIX_HALF:3 * MIX_HALF].T.astype(BF16)
    w_b = w[:, 3 * MIX_HALF:6 * MIX_HALF].astype(BF16)
    w_z = w[:, 6 * MIX_HALF:7 * MIX_HALF].astype(BF16)
    c_ab = 7 * MIX_HALF
    w_g = jnp.concatenate([rep(w[:, c_ab:c_ab + GDN_HEADS]), rep(w[:, c_ab + GDN_HEADS:])], axis=1).astype(BF16)
    alog_rep = rep(a_log[l])[None]
    dtb_rep = rep(dt_bias[l])[None]
    gn_rep = jnp.tile(gdn_norm[l], GDN_HEADS)[None]
    g512 = (jnp.arange(MIX_HALF)[:, None] // HEAD_DIM == jnp.arange(MIX_HALF)[None, :] // HEAD_DIM).astype(BF16)
    gain_mix = norm_mix[l][None]
    wo_a = w_out[l][:MIX_HALF].astype(BF16)
    wo_b = w_out[l][MIX_HALF:].astype(BF16)
    pad_r = ROUTE_LANES - N_GROUPS - N_EXPERTS
    w_route = jnp.pad(jnp.concatenate([w_group[l], w_router[l]], axis=1), ((0, 0), (0, pad_r)))
    b_route = jnp.pad(jnp.concatenate([b_group[l], b_router[l]]), (0, pad_r))[None]
    lam_vecs = [v[l][None] for v in (lambda_q1, lambda_k1, lambda_q2, lambda_k2)]
    cos_t, sina_t, sinb_t = _rope_tables(N_META + s)

    meta3 = meta[None]
    _, ka_m, vat_m = _proj_attn(meta3, gain_mix, w_qk, w_vt, cos_t[:N_META], sina_t[:N_META], sinb_t[:N_META],
                                N_META)
    zero_halo = jnp.zeros((1, 1, N_META, d), F32)
    gdn_m = _proj_gdn(meta3, zero_halo, gain_mix, w_b, w_z, w_g, conv_w[l], alog_rep, dtb_rep, g512, N_META)
    pad_m = lambda a: jnp.pad(a, ((0, 0), (0, CHUNK - N_META), (0, 0)))
    s_zero = jnp.zeros((GDN_HEADS // 2, SLAB, SLAB), F32)
    _, s_meta = _gated_delta(*[pad_m(a) for a in gdn_m], gn_rep, g512, s_zero, CHUNK)

    tb_a = _block(s, 1024)
    qa, ka, vat = _proj_attn(x, gain_mix, w_qk, w_vt, cos_t[N_META:], sina_t[N_META:], sinb_t[N_META:], tb_a)
    tb_g = _block(s, 512)
    nb_g = s // tb_g
    tails = x.reshape(b, nb_g, tb_g, d)[:, :-1, tb_g - N_META:, :]
    halo = jnp.concatenate([jnp.broadcast_to(meta[None, None], (b, 1, N_META, d)), tails], axis=1)
    gdn_f = _proj_gdn(x, halo, gain_mix, w_b, w_z, w_g, conv_w[l], alog_rep, dtb_rep, g512, tb_g)

    oa = _diff_attention(qa, ka, vat, ka_m, vat_m, *lam_vecs, diff_norm[l][None], _block(s, 512), 512)
    ob, _ = _gated_delta(*gdn_f, gn_rep, g512, s_meta[0], _block(s, 256))

    h1, t, comb = _out_router(x, oa, ob, wo_a, wo_b, norm_ffn[l][None], w_route, b_route, _block(s, 512))

    n = b * s
    out = _moe(t.reshape(n, d), comb.reshape(n, ROUTE_LANES), h1.reshape(n, d), w_gate[l].astype(BF16),
               w_up[l].astype(BF16), w_down[l].astype(BF16), norm_final[None], _block(n, 1024), 2)
    return out.reshape(b, s, d)
```

```python
import functools
import math

import jax
import jax.numpy as jnp
from jax import lax
from jax.experimental import pallas as pl
from jax.experimental.pallas import tpu as pltpu

F32 = jnp.float32
BF16 = jnp.bfloat16
HIGHEST = lax.Precision.HIGHEST

D_MODEL = 1024
N_META = 16
CHUNK = 64
EPS = 1e-6
ROPE_THETA = 10000.0
HEAD_DIM = 64
SLAB = 128
DA_HEADS = 4
GDN_HEADS = 8
MIX_HALF = 512
N_GROUPS = 4
EXPERTS_PER_GROUP = 8
N_EXPERTS = 32
D_EXPERT = 256
CONV_K = 4
LAM_INIT = 0.8 - 0.6 * math.exp(-0.3 * 0)
ROUTE_LANES = 128
EXPERT_ROW0 = 8
SLOT_GRAN = 16
MOE_BLOCK = 512
MOE_TILE = 512
NEG_BIG = -1e30
ONES_ROWS = 16
VMEM_LIMIT = 56 * 1024 * 1024


def _dot(a, b, precision=None):
    return jnp.dot(a, b, preferred_element_type=F32, precision=precision)


def _dot_nt(a, b):
    return lax.dot_general(a, b, (((1,), (1,)), ((), ())), preferred_element_type=F32)


def _dot_tn(a, b):
    return lax.dot_general(a, b, (((0,), (0,)), ((), ())), preferred_element_type=F32)


def _rms_scale(x):
    return x * lax.rsqrt(jnp.mean(x * x, axis=-1, keepdims=True) + EPS)


def _silu(x):
    return x * jax.nn.sigmoid(x)


def _group_sumsq(x, g_ref):
    sq = x * x
    hi = sq.astype(BF16)
    lo = (sq - hi.astype(F32)).astype(BF16)
    g = g_ref[...]
    return _dot(hi, g) + _dot(lo, g)


def _proj_attn_kernel(x_ref, gain_ref, w_ref, wvt_ref, cos_ref, sina_ref, sinb_ref, qa_ref, ka_ref, vt_ref):
    u = (_rms_scale(x_ref[0]) * gain_ref[...]).astype(BF16)
    proj = _dot(u, w_ref[...])
    vt_ref[0] = _dot_nt(wvt_ref[...], u).astype(BF16)
    cos, sina, sinb = cos_ref[...], sina_ref[...], sinb_ref[...]
    for s in range(2 * DA_HEADS):
        xs = proj[:, SLAB * s:SLAB * (s + 1)]
        r = xs * cos + pltpu.roll(xs, SLAB - 32, 1) * sina + pltpu.roll(xs, 32, 1) * sinb
        if s < DA_HEADS:
            qa_ref[0, :, SLAB * s:SLAB * (s + 1)] = (r * (HEAD_DIM ** -0.5)).astype(BF16)
        else:
            t = s - DA_HEADS
            ka_ref[0, :, SLAB * t:SLAB * (t + 1)] = r.astype(BF16)


def _proj_attn(x, gain, w_qk, w_vt, cos, sina, sinb, tb):
    b, s, d = x.shape
    nb = s // tb
    slab_out = jax.ShapeDtypeStruct((b, s, MIX_HALF), BF16)
    full = lambda shape: pl.BlockSpec(shape, lambda i, j: (0,) * len(shape))
    tok = pl.BlockSpec((1, tb, MIX_HALF), lambda i, j: (i, j, 0))
    tab = pl.BlockSpec((tb, SLAB), lambda i, j: (j, 0))
    return pl.pallas_call(
        _proj_attn_kernel,
        grid=(b, nb),
        in_specs=[pl.BlockSpec((1, tb, d), lambda i, j: (i, j, 0)), full((1, d)), full(w_qk.shape),
                  full(w_vt.shape), tab, tab, tab],
        out_specs=[tok, tok, pl.BlockSpec((1, MIX_HALF, tb), lambda i, j: (i, 0, j))],
        out_shape=[slab_out, slab_out, jax.ShapeDtypeStruct((b, MIX_HALF, s), BF16)],
        compiler_params=pltpu.CompilerParams(dimension_semantics=("parallel", "parallel"),
                                             vmem_limit_bytes=VMEM_LIMIT),
        name="proj_attn",
    )(x, gain, w_qk, w_vt, cos, sina, sinb)


def _proj_gdn_kernel(x_ref, halo_ref, gain_ref, wb_ref, wz_ref, wg_ref, convw_ref, alog_ref, dtb_ref, g_ref,
                     q_ref, k_ref, v_ref, z_ref, gb_ref, beta_ref):
    halo_rows = halo_ref.shape[2]
    xe = jnp.concatenate([halo_ref[0, 0], x_ref[0]], axis=0)
    ue = (_rms_scale(xe) * gain_ref[...]).astype(BF16)
    u = ue[halo_rows:]
    cw = convw_ref[...]
    outs = (q_ref, k_ref, v_ref)
    for part in range(3):
        cols = slice(MIX_HALF * part, MIX_HALF * (part + 1))
        pb = _dot(ue, wb_ref[:, cols])
        c = cw[:, cols]
        y = pb[halo_rows:] * c[CONV_K - 1:CONV_K]
        for back in range(1, CONV_K):
            tap = CONV_K - 1 - back
            y = y + pltpu.roll(pb, back, 0)[halo_rows:] * c[tap:tap + 1]
        y = _silu(y)
        if part < 2:
            y = y * lax.rsqrt(_group_sumsq(y, g_ref) + EPS)
            if part == 0:
                y = y * (HEAD_DIM ** -0.5)
        outs[part][0] = y.astype(BF16)
    z_ref[0] = _silu(_dot(u, wz_ref[...])).astype(BF16)
    gates = _dot(u, wg_ref[...])
    ab = gates[:, :MIX_HALF] + dtb_ref[...]
    softplus = jnp.maximum(ab, 0.0) + jnp.log1p(jnp.exp(-jnp.abs(ab)))
    gb_ref[0] = -jnp.exp(alog_ref[...]) * softplus
    beta_ref[0] = jax.nn.sigmoid(gates[:, MIX_HALF:]).astype(BF16)


def _proj_gdn(x, halo, gain, w_b, w_z, w_g, conv_w, alog_rep, dtb_rep, g512, tb):
    b, s, d = x.shape
    nb = s // tb
    full = lambda shape: pl.BlockSpec(shape, lambda i, j: (0,) * len(shape))
    tok = pl.BlockSpec((1, tb, MIX_HALF), lambda i, j: (i, j, 0))
    bf = jax.ShapeDtypeStruct((b, s, MIX_HALF), BF16)
    return pl.pallas_call(
        _proj_gdn_kernel,
        grid=(b, nb),
        in_specs=[pl.BlockSpec((1, tb, d), lambda i, j: (i, j, 0)),
                  pl.BlockSpec((1, 1) + halo.shape[2:], lambda i, j: (i, j, 0, 0)),
                  full((1, d)), full(w_b.shape), full(w_z.shape), full(w_g.shape), full(conv_w.shape),
                  full((1, MIX_HALF)), full((1, MIX_HALF)), full(g512.shape)],
        out_specs=[tok] * 6,
        out_shape=[bf, bf, bf, bf, jax.ShapeDtypeStruct((b, s, MIX_HALF), F32), bf],
        compiler_params=pltpu.CompilerParams(dimension_semantics=("parallel", "parallel"),
                                             vmem_limit_bytes=VMEM_LIMIT),
        name="proj_gdn",
    )(x, halo, gain, w_b, w_z, w_g, conv_w, alog_rep, dtb_rep, g512)


def _attn_kernel(q_ref, k_ref, vt_ref, km_ref, vtm_ref, lq1_ref, lk1_ref, lq2_ref, lk2_ref, dn_ref, o_ref,
                 qqt_scr, m_scr, acc_scr, st_scr, *, qb, cw):
    i = pl.program_id(2)
    n_col = 2 * qb // cw
    qt = q_ref[0].astype(F32).T
    feat = lax.broadcasted_iota(jnp.int32, (SLAB, qb), 0)
    qqt_scr[...] = jnp.concatenate([jnp.where(feat < HEAD_DIM, qt, 0.0), jnp.where(feat < HEAD_DIM, 0.0, qt)],
                                   axis=1).astype(BF16)

    def with_ones(vt_blk):
        return jnp.concatenate([vt_blk, jnp.ones((ONES_ROWS, vt_blk.shape[1]), BF16)], axis=0)

    def softmax_pv(st, v1t, c, first, masked):
        cs = slice(cw * c, cw * (c + 1))
        if masked:
            k_chunk = lax.broadcasted_iota(jnp.int32, st.shape, 0) // CHUNK
            q_chunk = ((lax.broadcasted_iota(jnp.int32, st.shape, 1) + cw * c) % qb) // CHUNK
            st = jnp.where(k_chunk <= q_chunk, st, NEG_BIG)
        m_cur = jnp.max(st, axis=0, keepdims=True)
        if first:
            m_new = m_cur
            acc_scr[:, cs] = _dot(v1t, jnp.exp(st - m_new).astype(BF16))
        else:
            m_prev = m_scr[:, cs]
            m_new = jnp.maximum(m_prev, m_cur)
            pv = _dot(v1t, jnp.exp(st - m_new).astype(BF16))
            acc_scr[:, cs] = acc_scr[:, cs] * jnp.exp(m_prev - m_new) + pv
        m_scr[:, cs] = m_new

    def scores(start, c):
        return _dot(k_ref[0, pl.ds(start, qb), :], qqt_scr[:, cw * c:cw * (c + 1)])

    v1t_meta = with_ones(vtm_ref[0])
    for c in range(n_col):
        softmax_pv(_dot(km_ref[0], qqt_scr[:, cw * c:cw * (c + 1)]), v1t_meta, c, True, False)
    st_scr[...] = scores(0, 0)

    def key_block(j, masked):
        start = pl.multiple_of(j * qb, qb)
        v1t = with_ones(vt_ref[0, :, pl.ds(start, qb)])
        st = st_scr[...]
        for c in range(n_col):
            st_next = None
            if c + 1 < n_col:
                st_next = scores(start, c + 1)
            elif not masked:
                st_scr[...] = scores(start + qb, 0)
            softmax_pv(st, v1t, c, False, masked)
            st = st_next

    def full_block(j, carry):
        key_block(j, False)
        return carry

    lax.fori_loop(0, i, full_block, 0)
    key_block(i, True)

    acc = acc_scr[...]
    o1 = acc[:SLAB, :qb] / acc[SLAB:SLAB + 1, :qb]
    o2 = acc[:SLAB, qb:] / acc[SLAB:SLAB + 1, qb:]
    lam = (jnp.exp(jnp.sum(lq1_ref[...] * lk1_ref[...], axis=1, keepdims=True))
           - jnp.exp(jnp.sum(lq2_ref[...] * lk2_ref[...], axis=1, keepdims=True)) + LAM_INIT)
    o = (o1 - lam * o2).T
    o_ref[0] = (_rms_scale(o) * dn_ref[...] * (1.0 - LAM_INIT)).astype(BF16)


def _diff_attention(qa, ka, vat, ka_meta, vat_meta, lq1, lk1, lq2, lk2, diff_norm, qb, cw):
    b, s, _ = qa.shape
    nq = s // qb
    vec = lambda n: pl.BlockSpec((1, n), lambda bi, h, i: (0, 0))
    return pl.pallas_call(
        functools.partial(_attn_kernel, qb=qb, cw=cw),
        grid=(b, DA_HEADS, nq),
        in_specs=[pl.BlockSpec((1, qb, SLAB), lambda bi, h, i: (bi, i, h)),
                  pl.BlockSpec((1, s, SLAB), lambda bi, h, i: (bi, 0, h)),
                  pl.BlockSpec((1, SLAB, s), lambda bi, h, i: (bi, h, 0)),
                  pl.BlockSpec((1, N_META, SLAB), lambda bi, h, i: (0, 0, h)),
                  pl.BlockSpec((1, SLAB, N_META), lambda bi, h, i: (0, h, 0)),
                  vec(HEAD_DIM), vec(HEAD_DIM), vec(HEAD_DIM), vec(HEAD_DIM), vec(SLAB)],
        out_specs=pl.BlockSpec((1, qb, SLAB), lambda bi, h, i: (bi, i, h)),
        out_shape=jax.ShapeDtypeStruct((b, s, MIX_HALF), BF16),
        scratch_shapes=[pltpu.VMEM((SLAB, 2 * qb), BF16), pltpu.VMEM((1, 2 * qb), F32),
                        pltpu.VMEM((SLAB + ONES_ROWS, 2 * qb), F32), pltpu.VMEM((qb, cw), F32)],
        compiler_params=pltpu.CompilerParams(dimension_semantics=("parallel", "parallel", "arbitrary"),
                                             vmem_limit_bytes=VMEM_LIMIT),
        name="diff_attention",
    )(qa, ka, vat, ka_meta, vat_meta, lq1, lk1, lq2, lk2, diff_norm)


def _gdn_kernel(q_ref, k_ref, v_ref, z_ref, g_ref, beta_ref, gn_ref, g512_ref, s0_ref, o_ref, sfin_ref,
                s_scr, o_scr, *, n_chunks):
    j = pl.program_id(1)
    n_pairs = GDN_HEADS // 2

    @pl.when(j == 0)
    def _():
        s_scr[...] = s0_ref[...]

    row = lax.broadcasted_iota(jnp.int32, (CHUNK, SLAB), 0)
    lane = lax.broadcasted_iota(jnp.int32, (CHUNK, SLAB), 1)
    col = lane % CHUNK
    head_a = lane < HEAD_DIM
    tri_incl = row >= col
    tri_strict = row > col
    eye = (row == col).astype(F32)
    ltri = (lax.broadcasted_iota(jnp.int32, (CHUNK, CHUNK), 0)
            >= lax.broadcasted_iota(jnp.int32, (CHUNK, CHUNK), 1)).astype(F32)
    bd_mask = ((lax.broadcasted_iota(jnp.int32, (SLAB, SLAB), 0) < HEAD_DIM)
               == (lax.broadcasted_iota(jnp.int32, (SLAB, SLAB), 1) < HEAD_DIM))

    def block_diag(x):
        xb = x.astype(BF16)
        zero = jnp.zeros_like(xb)
        return jnp.concatenate([jnp.where(head_a, xb, zero), jnp.where(head_a, zero, xb)], axis=0)

    def pair_mm(x, y):
        return _dot(x.astype(BF16), block_diag(y))

    items = [(c, p) for c in range(n_chunks) for p in range(n_pairs)]
    rows_of = lambda c: slice(CHUNK * c, CHUNK * (c + 1))
    cols_of = lambda p: slice(SLAB * p, SLAB * (p + 1))
    load = lambda ref, c, p: ref[0, rows_of(c), cols_of(p)]

    g_col, decay, k_f, k_beta, k_bd, q_f, vb = {}, {}, {}, {}, {}, {}, {}
    for it in items:
        c, p = it
        g = load(g_ref, c, p)
        g_col[it] = _dot(ltri, g, precision=HIGHEST)
        g_row = jnp.sum(jnp.where(row <= col, g, 0.0), axis=0, keepdims=True)
        decay[it] = jnp.where(tri_incl, jnp.exp(jnp.where(tri_incl, g_col[it] - g_row, 0.0)), 0.0)
        beta = load(beta_ref, c, p).astype(F32)
        k_f[it] = load(k_ref, c, p).astype(F32)
        q_f[it] = load(q_ref, c, p).astype(F32)
        k_beta[it] = k_f[it] * beta
        vb[it] = load(v_ref, c, p).astype(F32) * beta
        k_bd[it] = block_diag(k_f[it])

    a, a_qk = {}, {}
    for it in items:
        lhs = jnp.concatenate([k_beta[it].astype(BF16), q_f[it].astype(BF16)], axis=0)
        kk = _dot_nt(lhs, k_bd[it])
        a[it] = jnp.where(tri_strict, kk[:CHUNK] * decay[it], 0.0)
        a_qk[it] = jnp.where(tri_incl, kk[CHUNK:] * decay[it], 0.0)

    t = {it: eye - a[it] for it in items}
    pw = dict(a)
    for _ in range(5):
        for it in items:
            pw[it] = pair_mm(pw[it], pw[it])
        for it in items:
            t[it] = t[it] + pair_mm(t[it], pw[it])

    u, w, k_g, q_g, e_last = {}, {}, {}, {}, {}
    for it in items:
        e_col = jnp.exp(g_col[it])
        rhs = jnp.concatenate([block_diag(vb[it]), block_diag(k_beta[it] * e_col)], axis=1)
        uw = _dot(t[it].astype(BF16), rhs)
        u[it], w[it] = uw[:, :SLAB], uw[:, SLAB:]
        g_last = g_col[it][CHUNK - 1:CHUNK, :]
        k_g[it] = (k_f[it] * jnp.exp(g_last - g_col[it])).astype(BF16)
        q_g[it] = q_f[it] * e_col
        e_last[it] = jnp.exp(g_last)

    state = [s_scr[p] for p in range(n_pairs)]
    for c in range(n_chunks):
        its = [(c, p) for p in range(n_pairs)]
        ws_qs = [_dot(jnp.concatenate([w[it], q_g[it]], axis=0).astype(BF16), state[it[1]].astype(BF16))
                 for it in its]
        v_new = [u[it] - sq[:CHUNK] for it, sq in zip(its, ws_qs)]
        for it, sq, vn in zip(its, ws_qs, v_new):
            o_scr[rows_of(c), cols_of(it[1])] = sq[CHUNK:] + pair_mm(a_qk[it], vn)
        for it, vn in zip(its, v_new):
            upd = _dot_tn(k_g[it], vn.astype(BF16))
            state[it[1]] = state[it[1]] * e_last[it] + jnp.where(bd_mask, upd, 0.0)
    for p in range(n_pairs):
        s_scr[p] = state[p]

    o = o_scr[...]
    ss = _group_sumsq(o, g512_ref)
    y = o * lax.rsqrt(ss * (1.0 / HEAD_DIM) + EPS) * gn_ref[...]
    o_ref[0] = (y * z_ref[0].astype(F32)).astype(BF16)

    @pl.when(j == pl.num_programs(1) - 1)
    def _():
        sfin_ref[0] = s_scr[...]


def _gated_delta(q, k, v, z, gb, beta, gn_rep, g512, s0, tbg):
    b, s, _ = q.shape
    nb = s // tbg
    n_pairs = GDN_HEADS // 2
    tok = pl.BlockSpec((1, tbg, MIX_HALF), lambda bi, j: (bi, j, 0))
    full = lambda shape: pl.BlockSpec(shape, lambda bi, j: (0,) * len(shape))
    return pl.pallas_call(
        functools.partial(_gdn_kernel, n_chunks=tbg // CHUNK),
        grid=(b, nb),
        in_specs=[tok] * 6 + [full((1, MIX_HALF)), full(g512.shape), full(s0.shape)],
        out_specs=[tok, pl.BlockSpec((1, n_pairs, SLAB, SLAB), lambda bi, j: (bi, 0, 0, 0))],
        out_shape=[jax.ShapeDtypeStruct((b, s, MIX_HALF), BF16),
                   jax.ShapeDtypeStruct((b, n_pairs, SLAB, SLAB), F32)],
        scratch_shapes=[pltpu.VMEM((n_pairs, SLAB, SLAB), F32), pltpu.VMEM((tbg, MIX_HALF), F32)],
        compiler_params=pltpu.CompilerParams(dimension_semantics=("parallel", "arbitrary"),
                                             vmem_limit_bytes=VMEM_LIMIT),
        name="gated_delta",
    )(q, k, v, z, gb, beta, gn_rep, g512, s0)


def _out_router_kernel(x_ref, oa_ref, ob_ref, wo_ref, gain_ref, wr_ref, br_ref, before_ref,
                       h_ref, t_ref, rows_ref, cols_ref, cnt_ref):
    h = x_ref[0] + _dot(jnp.concatenate([oa_ref[0], ob_ref[0]], axis=1), wo_ref[...])
    h_ref[0] = h
    t = _rms_scale(h) * gain_ref[...]
    t_hi = t.astype(BF16)
    t_ref[0] = t_hi
    t_lo = (t - t_hi.astype(F32)).astype(BF16)
    wr = wr_ref[...]
    hi_part = _dot(t_hi, wr)
    logits = hi_part[:, :ROUTE_LANES] + hi_part[:, ROUTE_LANES:] + _dot(t_lo, wr)[:, :ROUTE_LANES] + br_ref[...]

    lt = logits.T
    tb = lt.shape[1]
    row8 = lax.broadcasted_iota(jnp.int32, (EXPERTS_PER_GROUP, tb), 0)
    col_max = lambda v: jnp.max(v, axis=0, keepdims=True)
    col_sum = lambda v: jnp.sum(v, axis=0, keepdims=True)
    first_argmax = lambda v, vmax: jnp.min(jnp.where(v == vmax, row8, EXPERTS_PER_GROUP), axis=0, keepdims=True)

    gl = jnp.where(row8 < N_GROUPS, lt[:EXPERTS_PER_GROUP], NEG_BIG)
    gmax = col_max(gl)
    gsel = first_argmax(gl, gmax)
    psel = 1.0 / col_sum(jnp.exp(gl - gmax))
    el = jnp.zeros((EXPERTS_PER_GROUP, tb), F32)
    for g in range(N_GROUPS):
        lo = EXPERT_ROW0 + EXPERTS_PER_GROUP * g
        el = jnp.where(gsel == g, lt[lo:lo + EXPERTS_PER_GROUP], el)
    m1 = col_max(el)
    i1 = first_argmax(el, m1)
    el2 = jnp.where(row8 == i1, NEG_BIG, el)
    m2 = col_max(el2)
    i2 = first_argmax(el2, m2)
    denom = col_sum(jnp.exp(el - m1))
    p1 = 1.0 / denom
    p2 = jnp.exp(m2 - m1) / denom
    w1 = p1 / (p1 + p2) * psel
    w2 = p2 / (p1 + p2) * psel
    e1 = gsel * EXPERTS_PER_GROUP + i1
    e2 = gsel * EXPERTS_PER_GROUP + i2

    row_e = lax.broadcasted_iota(jnp.int32, (N_EXPERTS, tb), 0)
    hot1 = (row_e == e1).astype(F32)
    hot2 = (row_e == e2).astype(F32)
    both = hot1 + hot2
    n_gran = jnp.ceil(jnp.sum(both, axis=1, keepdims=True) * (1.0 / SLOT_GRAN)) * SLOT_GRAN
    ei = lax.broadcasted_iota(jnp.int32, (N_EXPERTS, N_EXPERTS), 0)
    ej = lax.broadcasted_iota(jnp.int32, (N_EXPERTS, N_EXPERTS), 1)
    seg_start = _dot((ei > ej).astype(F32), jnp.broadcast_to(n_gran, (N_EXPERTS, SLAB)), precision=HIGHEST)[:, :1]
    earlier = _dot(both.astype(BF16), before_ref[...])
    where_to = earlier + seg_start
    pos1 = col_sum(hot1 * where_to)
    pos2 = col_sum(hot2 * where_to)

    info = jnp.concatenate([pos1, pos2, w1, w2, e1.astype(F32), e2.astype(F32),
                            jnp.zeros((SLAB - 6, tb), F32)], axis=0)
    rows_ref[0] = info[:8]
    cols_ref[0] = info.T
    cnt_ref[0] = jnp.broadcast_to(n_gran, (N_EXPERTS, SLAB))


def _out_router(x, oa, ob, w_o, gain, w_route, b_route, before, tb):
    b, s, d = x.shape
    nb = s // tb
    full = lambda shape: pl.BlockSpec(shape, lambda i, j: (0,) * len(shape))
    tokd = pl.BlockSpec((1, tb, d), lambda i, j: (i, j, 0))
    tokh = pl.BlockSpec((1, tb, MIX_HALF), lambda i, j: (i, j, 0))
    return pl.pallas_call(
        _out_router_kernel,
        grid=(b, nb),
        in_specs=[tokd, tokh, tokh, full(w_o.shape), full((1, d)), full(w_route.shape), full((1, ROUTE_LANES)),
                  full(before.shape)],
        out_specs=[tokd, tokd, pl.BlockSpec((1, 8, tb), lambda i, j: (i, 0, j)),
                   pl.BlockSpec((1, tb, SLAB), lambda i, j: (i, j, 0)),
                   pl.BlockSpec((1, N_EXPERTS, SLAB), lambda i, j: (i * nb + j, 0, 0))],
        out_shape=[jax.ShapeDtypeStruct((b, s, d), F32), jax.ShapeDtypeStruct((b, s, d), BF16),
                   jax.ShapeDtypeStruct((b, 8, s), F32), jax.ShapeDtypeStruct((b, s, SLAB), F32),
                   jax.ShapeDtypeStruct((b * nb, N_EXPERTS, SLAB), F32)],
        compiler_params=pltpu.CompilerParams(dimension_semantics=("parallel", "parallel"),
                                             vmem_limit_bytes=VMEM_LIMIT),
        name="out_router",
    )(x, oa, ob, w_o, gain, w_route, b_route, before)


def _granule_copy(granule_dst_ref, block, g, local_ref, global_ref, sem, to_global):
    dst = pl.multiple_of(granule_dst_ref[block, g], SLOT_GRAN)
    loc = local_ref.at[pl.ds(pl.multiple_of(g * SLOT_GRAN, SLOT_GRAN), SLOT_GRAN)]
    glob = global_ref.at[pl.ds(dst, SLOT_GRAN)]
    return pltpu.make_async_copy(loc, glob, sem) if to_global else pltpu.make_async_copy(glob, loc, sem)


def _dispatch_kernel(gdst_ref, t_ref, rows_ref, xg_init_ref, xg_ref, xs_scr, sem, *, n_slots):
    del xg_init_ref
    blk = pl.program_id(0)
    tb = t_ref.shape[0]
    info = rows_ref[0]
    slot = lax.broadcasted_iota(jnp.int32, (n_slots, tb), 0).astype(F32)
    onehot = jnp.where((slot == info[0:1]) | (slot == info[1:2]), 1.0, 0.0).astype(BF16)
    xs_scr[...] = _dot(onehot, t_ref[...]).astype(BF16)
    n_gran = n_slots // SLOT_GRAN

    def start(g, carry):
        @pl.when(gdst_ref[blk, g] >= 0)
        def _():
            _granule_copy(gdst_ref, blk, g, xs_scr, xg_ref, sem, True).start()
        return carry

    def wait(g, carry):
        @pl.when(gdst_ref[blk, g] >= 0)
        def _():
            _granule_copy(gdst_ref, blk, g, xs_scr, xg_ref, sem, True).wait()
        return carry

    lax.fori_loop(0, n_gran, start, 0)
    lax.fori_loop(0, n_gran, wait, 0)


def _dispatch(granule_dst, t, rows, xg_init, tb, n_slots):
    n, d = t.shape
    nblk = n // tb
    nb = rows.shape[2] // tb
    grid_spec = pltpu.PrefetchScalarGridSpec(
        num_scalar_prefetch=1,
        grid=(nblk,),
        in_specs=[pl.BlockSpec((tb, d), lambda i, gd: (i, 0)),
                  pl.BlockSpec((1, 8, tb), lambda i, gd: (i // nb, 0, i % nb)),
                  pl.BlockSpec(memory_space=pl.ANY)],
        out_specs=pl.BlockSpec(memory_space=pl.ANY),
        scratch_shapes=[pltpu.VMEM((n_slots, d), BF16), pltpu.SemaphoreType.DMA(())],
    )
    return pl.pallas_call(
        functools.partial(_dispatch_kernel, n_slots=n_slots),
        grid_spec=grid_spec,
        out_shape=jax.ShapeDtypeStruct(xg_init.shape, BF16),
        input_output_aliases={3: 0},
        compiler_params=pltpu.CompilerParams(dimension_semantics=("arbitrary",), vmem_limit_bytes=VMEM_LIMIT),
        name="moe_dispatch",
    )(granule_dst, t, rows, xg_init)


def _experts_kernel(tile_expert_ref, n_tiles_ref, x_ref, wg_ref, wu_ref, wd_ref, y_ref):
    del tile_expert_ref

    @pl.when(pl.program_id(0) < n_tiles_ref[0])
    def _():
        x = x_ref[...]
        act = _silu(_dot(x, wg_ref[0])) * _dot(x, wu_ref[0])
        y_ref[...] = _dot(act.astype(BF16), wd_ref[0]).astype(BF16)


def _experts(tile_expert, n_tiles, xg, w_gate, w_up, w_down, tile):
    rows, d = xg.shape
    used = lambda i, te, nt: jnp.minimum(i, nt[0] - 1)
    wspec = lambda shape: pl.BlockSpec((1,) + shape, lambda i, te, nt: (te[used(i, te, nt)], 0, 0))
    grid_spec = pltpu.PrefetchScalarGridSpec(
        num_scalar_prefetch=2,
        grid=(rows // tile,),
        in_specs=[pl.BlockSpec((tile, d), lambda i, te, nt: (used(i, te, nt), 0)),
                  wspec((d, D_EXPERT)), wspec((d, D_EXPERT)), wspec((D_EXPERT, d))],
        out_specs=pl.BlockSpec((tile, d), lambda i, te, nt: (used(i, te, nt), 0)),
    )
    return pl.pallas_call(
        _experts_kernel,
        grid_spec=grid_spec,
        out_shape=jax.ShapeDtypeStruct((rows, d), BF16),
        input_output_aliases={2: 0},
        compiler_params=pltpu.CompilerParams(dimension_semantics=("arbitrary",), vmem_limit_bytes=VMEM_LIMIT),
        name="moe_experts",
    )(tile_expert, n_tiles, xg, w_gate, w_up, w_down)


def _combine_kernel(gdst_ref, yg_ref, cols_ref, h_ref, gain_ref, o_ref, ys_scr, sem, *, n_slots):
    blk = pl.program_id(0)
    tb = h_ref.shape[0]
    n_gran = n_slots // SLOT_GRAN

    def start(g, carry):
        used = gdst_ref[blk, g] >= 0

        @pl.when(used)
        def _():
            _granule_copy(gdst_ref, blk, g, ys_scr, yg_ref, sem, False).start()

        @pl.when(jnp.logical_not(used))
        def _():
            ys_scr[pl.ds(pl.multiple_of(g * SLOT_GRAN, SLOT_GRAN), SLOT_GRAN), :] = jnp.zeros(
                (SLOT_GRAN, ys_scr.shape[1]), BF16)
        return carry

    def wait(g, carry):
        @pl.when(gdst_ref[blk, g] >= 0)
        def _():
            _granule_copy(gdst_ref, blk, g, ys_scr, yg_ref, sem, False).wait()
        return carry

    lax.fori_loop(0, n_gran, start, 0)
    info = cols_ref[...]
    slot = lax.broadcasted_iota(jnp.int32, (tb, n_slots), 1).astype(F32)
    weights = (jnp.where(slot == info[:, 0:1], info[:, 2:3], 0.0)
               + jnp.where(slot == info[:, 1:2], info[:, 3:4], 0.0)).astype(BF16)
    lax.fori_loop(0, n_gran, wait, 0)
    y = _dot(weights, ys_scr[...])
    o_ref[...] = _rms_scale(h_ref[...] + y) * gain_ref[...]


def _combine(granule_dst, yg, cols, h, gain, tb, n_slots):
    n, d = h.shape
    grid_spec = pltpu.PrefetchScalarGridSpec(
        num_scalar_prefetch=1,
        grid=(n // tb,),
        in_specs=[pl.BlockSpec(memory_space=pl.ANY),
                  pl.BlockSpec((tb, SLAB), lambda i, gd: (i, 0)),
                  pl.BlockSpec((tb, d), lambda i, gd: (i, 0)),
                  pl.BlockSpec((1, d), lambda i, gd: (0, 0))],
        out_specs=pl.BlockSpec((tb, d), lambda i, gd: (i, 0)),
        scratch_shapes=[pltpu.VMEM((n_slots, d), BF16), pltpu.SemaphoreType.DMA(())],
    )
    return pl.pallas_call(
        functools.partial(_combine_kernel, n_slots=n_slots),
        grid_spec=grid_spec,
        out_shape=jax.ShapeDtypeStruct((n, d), F32),
        compiler_params=pltpu.CompilerParams(dimension_semantics=("arbitrary",), vmem_limit_bytes=VMEM_LIMIT),
        name="moe_combine",
    )(granule_dst, yg, cols, h, gain)


def _regroup_plan(n_gran_be, n_slots, tile, n_tiles_max):
    cnt = n_gran_be.astype(jnp.int32)
    region = (jnp.sum(cnt, axis=0) + tile - 1) // tile * tile
    region_end = jnp.cumsum(region)
    first_row = (region_end - region)[None, :] + jnp.cumsum(cnt, axis=0) - cnt
    seg_end = jnp.cumsum(cnt, axis=1)
    seg_start = seg_end - cnt
    g_row = jnp.arange(n_slots // SLOT_GRAN, dtype=jnp.int32) * SLOT_GRAN
    owner = jnp.sum(g_row[None, :, None] >= seg_end[:, None, :], axis=-1)
    in_use = g_row[None, :] < seg_end[:, -1:]
    owner = jnp.minimum(owner, N_EXPERTS - 1)
    dst = (jnp.take_along_axis(first_row, owner, axis=1) + g_row[None, :]
           - jnp.take_along_axis(seg_start, owner, axis=1))
    granule_dst = jnp.where(in_use, dst, -1).astype(jnp.int32)
    tile_row = jnp.arange(n_tiles_max, dtype=jnp.int32) * tile
    tile_expert = jnp.minimum(jnp.sum(tile_row[:, None] >= region_end[None, :], axis=-1), N_EXPERTS - 1)
    n_tiles = (region_end[-1] // tile).astype(jnp.int32)[None]
    return granule_dst, tile_expert.astype(jnp.int32), n_tiles


def _rope_tables(length):
    pos = jnp.arange(length, dtype=F32)
    inv_freq = ROPE_THETA ** (-jnp.arange(0, HEAD_DIM, 2, dtype=F32) / HEAD_DIM)
    ang = pos[:, None] * inv_freq[None, :]
    cos, sin = jnp.cos(ang), jnp.sin(ang)
    zero = jnp.zeros_like(sin)
    cos_t = jnp.tile(cos, (1, 4))
    sina_t = jnp.tile(jnp.concatenate([-sin, zero], axis=1), (1, 2))
    sinb_t = jnp.tile(jnp.concatenate([zero, sin], axis=1), (1, 2))
    return cos_t, sina_t, sinb_t


def _block(total, want):
    blk = min(total, want)
    assert total % blk == 0, (total, blk)
    return blk


def kernel(x, meta, norm_mix, w_in, lambda_q1, lambda_k1, lambda_q2, lambda_k2, diff_norm, conv_w, a_log, dt_bias,
           gdn_norm, w_out, norm_ffn, w_group, b_group, w_router, b_router, w_gate, w_up, w_down, norm_final):
    b, s, d = x.shape
    assert d == D_MODEL and meta.shape == (N_META, D_MODEL) and s % CHUNK == 0
    assert norm_mix.shape[0] == 1, "single-layer block"
    l = 0
    rep = lambda a: jnp.repeat(a, HEAD_DIM, axis=-1)

    w = w_in[l]
    w_qk = w[:, :2 * MIX_HALF].astype(BF16)
    w_vt = w[:, 2 * MIX_HALF:3 * MIX_HALF].T.astype(BF16)
    w_b = w[:, 3 * MIX_HALF:6 * MIX_HALF].astype(BF16)
    w_z = w[:, 6 * MIX_HALF:7 * MIX_HALF].astype(BF16)
    c_ab = 7 * MIX_HALF
    w_g = jnp.concatenate([rep(w[:, c_ab:c_ab + GDN_HEADS]), rep(w[:, c_ab + GDN_HEADS:])], axis=1).astype(BF16)
    alog_rep = rep(a_log[l])[None]
    dtb_rep = rep(dt_bias[l])[None]
    gn_rep = jnp.tile(gdn_norm[l], GDN_HEADS)[None]
    g512 = (jnp.arange(MIX_HALF)[:, None] // HEAD_DIM == jnp.arange(MIX_HALF)[None, :] // HEAD_DIM).astype(BF16)
    gain_mix = norm_mix[l][None]
    w_o = w_out[l].astype(BF16)
    route_pad = lambda g, e: jnp.concatenate(
        [g, jnp.zeros(g.shape[:-1] + (EXPERT_ROW0 - N_GROUPS,), F32), e,
         jnp.zeros(g.shape[:-1] + (ROUTE_LANES - EXPERT_ROW0 - N_EXPERTS,), F32)], axis=-1)
    w_route_f = route_pad(w_group[l], w_router[l])
    w_route_hi = w_route_f.astype(BF16)
    w_route = jnp.concatenate([w_route_hi, (w_route_f - w_route_hi.astype(F32)).astype(BF16)], axis=1)
    b_route = route_pad(b_group[l], b_router[l])[None]
    lam_vecs = [v[l][None] for v in (lambda_q1, lambda_k1, lambda_q2, lambda_k2)]
    cos_t, sina_t, sinb_t = _rope_tables(N_META + s)

    meta3 = meta[None]
    _, ka_m, vat_m = _proj_attn(meta3, gain_mix, w_qk, w_vt, cos_t[:N_META], sina_t[:N_META], sinb_t[:N_META],
                                N_META)
    zero_halo = jnp.zeros((1, 1, N_META, d), F32)
    gdn_m = _proj_gdn(meta3, zero_halo, gain_mix, w_b, w_z, w_g, conv_w[l], alog_rep, dtb_rep, g512, N_META)
    pad_m = lambda a: jnp.pad(a, ((0, 0), (0, CHUNK - N_META), (0, 0)))
    s_zero = jnp.zeros((GDN_HEADS // 2, SLAB, SLAB), F32)
    _, s_meta = _gated_delta(*[pad_m(a) for a in gdn_m], gn_rep, g512, s_zero, CHUNK)

    tb_a = _block(s, 1024)
    qa, ka, vat = _proj_attn(x, gain_mix, w_qk, w_vt, cos_t[N_META:], sina_t[N_META:], sinb_t[N_META:], tb_a)
    tb_g = _block(s, 512)
    nb_g = s // tb_g
    tails = x.reshape(b, nb_g, tb_g, d)[:, :-1, tb_g - N_META:, :]
    halo = jnp.concatenate([jnp.broadcast_to(meta[None, None], (b, 1, N_META, d)), tails], axis=1)
    gdn_f = _proj_gdn(x, halo, gain_mix, w_b, w_z, w_g, conv_w[l], alog_rep, dtb_rep, g512, tb_g)

    oa = _diff_attention(qa, ka, vat, ka_m, vat_m, *lam_vecs, diff_norm[l][None], _block(s, 512), 512)
    ob, _ = _gated_delta(*gdn_f, gn_rep, g512, s_meta[0], _block(s, 256))

    tb_r = _block(s, MOE_BLOCK)
    tok = jnp.arange(tb_r)
    before = (tok[:, None] < tok[None, :]).astype(BF16)
    h1, t, rows, cols, cnt = _out_router(x, oa, ob, w_o, norm_ffn[l][None], w_route, b_route, before, tb_r)

    n = b * s
    nblk = n // tb_r
    n_slots = 2 * tb_r + N_EXPERTS * SLOT_GRAN
    rows_max = 2 * n + (SLOT_GRAN - 1) * N_EXPERTS * nblk + (MOE_TILE - 1) * N_EXPERTS
    n_tiles_max = -(-rows_max // MOE_TILE)
    granule_dst, tile_expert, n_tiles = _regroup_plan(cnt[:, :, 0], n_slots, MOE_TILE, n_tiles_max)
    xg = _dispatch(granule_dst, t.reshape(n, d), rows, jnp.zeros((n_tiles_max * MOE_TILE, d), BF16), tb_r, n_slots)
    yg = _experts(tile_expert, n_tiles, xg, w_gate[l].astype(BF16), w_up[l].astype(BF16), w_down[l].astype(BF16),
                  MOE_TILE)
    out = _combine(granule_dst, yg, cols.reshape(n, SLAB), h1.reshape(n, d), norm_final[None], tb_r, n_slots)
    return out.reshape(b, s, d)
```

```python
import functools
import math

import jax
import jax.numpy as jnp
from jax import lax
from jax.experimental import pallas as pl
from jax.experimental.pallas import tpu as pltpu

F32 = jnp.float32
BF16 = jnp.bfloat16
HIGHEST = lax.Precision.HIGHEST

D_MODEL = 1024
N_META = 16
CHUNK = 64
EPS = 1e-6
ROPE_THETA = 10000.0
HEAD_DIM = 64
SLAB = 128
DA_HEADS = 4
GDN_HEADS = 8
MIX_HALF = 512
N_GROUPS = 4
EXPERTS_PER_GROUP = 8
N_EXPERTS = 32
D_EXPERT = 256
CONV_K = 4
LAM_INIT = 0.8 - 0.6 * math.exp(-0.3 * 0)
ROUTE_LANES = 128
EXPERT_ROW0 = 8
SLOT_GRAN = 16
MOE_BLOCK = 512
MOE_TILE = 512
NEG_BIG = -1e30
LOG2E = math.log2(math.e)
ONES_ROWS = 16
VMEM_LIMIT = 56 * 1024 * 1024


def _dot(a, b, precision=None):
    return jnp.dot(a, b, preferred_element_type=F32, precision=precision)


def _dot_nt(a, b):
    return lax.dot_general(a, b, (((1,), (1,)), ((), ())), preferred_element_type=F32)


def _dot_tn(a, b):
    return lax.dot_general(a, b, (((0,), (0,)), ((), ())), preferred_element_type=F32)


def _rms_scale(x):
    return x * lax.rsqrt(jnp.mean(x * x, axis=-1, keepdims=True) + EPS)


def _silu(x):
    return x * jax.nn.sigmoid(x)


def _group_sumsq(x, g_ref):
    return _dot((x * x).astype(BF16), g_ref[...])


def _proj_attn_kernel(x_ref, gain_ref, w_ref, wvt_ref, cos_ref, sina_ref, sinb_ref, qa_ref, ka_ref, vt_ref):
    u = (_rms_scale(x_ref[0]) * gain_ref[...]).astype(BF16)
    proj = _dot(u, w_ref[...])
    vt_ref[0] = _dot_nt(wvt_ref[...], u).astype(BF16)
    cos, sina, sinb = cos_ref[...], sina_ref[...], sinb_ref[...]
    for s in range(2 * DA_HEADS):
        xs = proj[:, SLAB * s:SLAB * (s + 1)]
        r = xs * cos + pltpu.roll(xs, SLAB - 32, 1) * sina + pltpu.roll(xs, 32, 1) * sinb
        if s < DA_HEADS:
            qa_ref[0, :, SLAB * s:SLAB * (s + 1)] = (r * (HEAD_DIM ** -0.5 * LOG2E)).astype(BF16)
        else:
            t = s - DA_HEADS
            ka_ref[0, :, SLAB * t:SLAB * (t + 1)] = r.astype(BF16)


def _proj_attn(x, gain, w_qk, w_vt, cos, sina, sinb, tb):
    b, s, d = x.shape
    nb = s // tb
    slab_out = jax.ShapeDtypeStruct((b, s, MIX_HALF), BF16)
    full = lambda shape: pl.BlockSpec(shape, lambda i, j: (0,) * len(shape))
    tok = pl.BlockSpec((1, tb, MIX_HALF), lambda i, j: (i, j, 0))
    tab = pl.BlockSpec((tb, SLAB), lambda i, j: (j, 0))
    return pl.pallas_call(
        _proj_attn_kernel,
        grid=(b, nb),
        in_specs=[pl.BlockSpec((1, tb, d), lambda i, j: (i, j, 0)), full((1, d)), full(w_qk.shape),
                  full(w_vt.shape), tab, tab, tab],
        out_specs=[tok, tok, pl.BlockSpec((1, MIX_HALF, tb), lambda i, j: (i, 0, j))],
        out_shape=[slab_out, slab_out, jax.ShapeDtypeStruct((b, MIX_HALF, s), BF16)],
        compiler_params=pltpu.CompilerParams(dimension_semantics=("parallel", "parallel"),
                                             vmem_limit_bytes=VMEM_LIMIT),
        name="proj_attn",
    )(x, gain, w_qk, w_vt, cos, sina, sinb)


def _proj_gdn_kernel(x_ref, halo_ref, gain_ref, wb_ref, wz_ref, wg_ref, convw_ref, alog_ref, dtb_ref, g_ref,
                     q_ref, k_ref, v_ref, z_ref, gb_ref, beta_ref, pb_scr):
    halo_rows = halo_ref.shape[2]
    xe = jnp.concatenate([halo_ref[0, 0], x_ref[0]], axis=0)
    ue = (_rms_scale(xe) * gain_ref[...]).astype(BF16)
    u = ue[halo_rows:]
    cw = convw_ref[...]
    outs = (q_ref, k_ref, v_ref)
    for part in range(3):
        cols = slice(MIX_HALF * part, MIX_HALF * (part + 1))
        pb_scr[...] = _dot(ue, wb_ref[:, cols])
        c = cw[:, cols]
        tb = pb_scr.shape[0] - halo_rows
        y = pb_scr[halo_rows:, :] * c[CONV_K - 1:CONV_K]
        for back in range(1, CONV_K):
            tap = CONV_K - 1 - back
            y = y + pb_scr[pl.ds(halo_rows - back, tb), :] * c[tap:tap + 1]
        y = _silu(y)
        if part < 2:
            y = y * lax.rsqrt(_group_sumsq(y, g_ref) + EPS)
            if part == 0:
                y = y * (HEAD_DIM ** -0.5)
        outs[part][0] = y.astype(BF16)
    z_ref[0] = _silu(_dot(u, wz_ref[...])).astype(BF16)
    gates = _dot(u, wg_ref[...])
    ab = gates + dtb_ref[...]
    softplus = jnp.maximum(ab, 0.0) + jnp.log1p(jnp.exp(-jnp.abs(ab)))
    g = -jnp.exp(alog_ref[...]) * softplus
    beta = jax.nn.sigmoid(gates)
    tb = g.shape[0]
    first_head = lax.broadcasted_iota(jnp.int32, (tb, SLAB), 1) < HEAD_DIM
    spread = lambda v, h: jnp.broadcast_to(v[:, h:h + 1], (tb, SLAB))
    for p in range(GDN_HEADS // 2):
        cols = slice(SLAB * p, SLAB * (p + 1))
        gb_ref[0, :, cols] = jnp.where(first_head, spread(g, 2 * p), spread(g, 2 * p + 1))
        beta_ref[0, :, cols] = jnp.where(first_head, spread(beta, GDN_HEADS + 2 * p),
                                         spread(beta, GDN_HEADS + 2 * p + 1)).astype(BF16)


def _proj_gdn(x, halo, gain, w_b, w_z, w_g, conv_w, alog_rep, dtb_rep, g512, tb):
    b, s, d = x.shape
    nb = s // tb
    full = lambda shape: pl.BlockSpec(shape, lambda i, j: (0,) * len(shape))
    tok = pl.BlockSpec((1, tb, MIX_HALF), lambda i, j: (i, j, 0))
    bf = jax.ShapeDtypeStruct((b, s, MIX_HALF), BF16)
    return pl.pallas_call(
        _proj_gdn_kernel,
        grid=(b, nb),
        in_specs=[pl.BlockSpec((1, tb, d), lambda i, j: (i, j, 0)),
                  pl.BlockSpec((1, 1) + halo.shape[2:], lambda i, j: (i, j, 0, 0)),
                  full((1, d)), full(w_b.shape), full(w_z.shape), full(w_g.shape), full(conv_w.shape),
                  full((1, SLAB)), full((1, SLAB)), full(g512.shape)],
        out_specs=[tok] * 6,
        out_shape=[bf, bf, bf, bf, jax.ShapeDtypeStruct((b, s, MIX_HALF), F32), bf],
        scratch_shapes=[pltpu.VMEM((halo.shape[2] + tb, MIX_HALF), F32)],
        compiler_params=pltpu.CompilerParams(dimension_semantics=("parallel", "parallel"),
                                             vmem_limit_bytes=VMEM_LIMIT),
        name="proj_gdn",
    )(x, halo, gain, w_b, w_z, w_g, conv_w, alog_rep, dtb_rep, g512)


def _attn_kernel(q_ref, k_ref, vt_ref, km_ref, vtm_ref, lq1_ref, lk1_ref, lq2_ref, lk2_ref, dn_ref, o_ref,
                 qqt_scr, m_scr, acc_scr, st_scr, p_scr, *, qb, cw):
    i = pl.program_id(2)
    n_col = 2 * qb // cw
    per_map = qb // cw
    qt = q_ref[0].astype(F32).T
    feat = lax.broadcasted_iota(jnp.int32, (SLAB, qb), 0)
    qqt_scr[...] = jnp.concatenate([jnp.where(feat < HEAD_DIM, qt, 0.0), jnp.where(feat < HEAD_DIM, 0.0, qt)],
                                   axis=1).astype(BF16)

    def with_ones(vt_blk):
        return jnp.concatenate([vt_blk, jnp.ones((ONES_ROWS, vt_blk.shape[1]), BF16)], axis=0)

    v1t_meta = with_ones(vtm_ref[0])
    for c in range(n_col):
        cs = slice(cw * c, cw * (c + 1))
        st = _dot(km_ref[0], qqt_scr[:, cs])
        m_new = jnp.max(st, axis=0, keepdims=True)
        acc_scr[:, cs] = _dot(v1t_meta, jnp.exp2((st - m_new).astype(BF16)))
        m_scr[:, cs] = m_new

    def scores_into(slot, start, c):
        st_scr[slot] = _dot(k_ref[0, pl.ds(start, cw), :], qqt_scr[:, cw * c:cw * (c + 1)])

    def softmax_pv(slot, v1t, c, masked):
        alphas = []
        for r in range(cw // SLAB):
            ls = slice(SLAB * r, SLAB * (r + 1))
            gs = slice(cw * c + SLAB * r, cw * c + SLAB * (r + 1))
            s = st_scr[slot, :, ls]
            if masked:
                k_chunk = lax.broadcasted_iota(jnp.int32, s.shape, 0) // CHUNK
                q_chunk = (lax.broadcasted_iota(jnp.int32, s.shape, 1) + SLAB * r) // CHUNK
                s = jnp.where(k_chunk <= q_chunk, s, NEG_BIG)
            m_prev = m_scr[:, gs]
            m_new = jnp.maximum(m_prev, jnp.max(s, axis=0, keepdims=True))
            p_scr[:, ls] = jnp.exp2((s - m_new).astype(BF16))
            alphas.append(jnp.exp2(m_prev - m_new))
            m_scr[:, gs] = m_new
        cs = slice(cw * c, cw * (c + 1))
        acc_scr[:, cs] = acc_scr[:, cs] * jnp.concatenate(alphas, axis=1) + _dot(v1t, p_scr[...])

    parity = [0]
    scores_into(0, 0, 0)

    def key_block(start, groups, masked_groups, following):
        v1t = with_ones(vt_ref[0, :, pl.ds(start, cw)])
        for idx, c in enumerate(groups):
            slot = parity[0]
            if idx + 1 < len(groups):
                scores_into(1 - slot, start, groups[idx + 1])
            elif following is not None:
                scores_into(1 - slot, *following)
            softmax_pv(slot, v1t, c, c in masked_groups)
            parity[0] = 1 - slot

    all_groups = list(range(n_col))

    def full_blocks(j, carry):
        for d in range(per_map):
            start = pl.multiple_of((per_map * j + d) * cw, cw)
            key_block(start, all_groups, (), (start + cw, 0))
        return carry

    lax.fori_loop(0, i, full_blocks, 0)
    for d in range(per_map):
        start = pl.multiple_of((per_map * i + d) * cw, cw)
        groups = [c for c in all_groups if c % per_map >= d]
        following = (start + cw, d + 1) if d + 1 < per_map else None
        key_block(start, groups, [c for c in groups if c % per_map == d], following)

    acc = acc_scr[...]
    o1 = acc[:SLAB, :qb] / acc[SLAB:SLAB + 1, :qb]
    o2 = acc[:SLAB, qb:] / acc[SLAB:SLAB + 1, qb:]
    lam = (jnp.exp(jnp.sum(lq1_ref[...] * lk1_ref[...], axis=1, keepdims=True))
           - jnp.exp(jnp.sum(lq2_ref[...] * lk2_ref[...], axis=1, keepdims=True)) + LAM_INIT)
    o = (o1 - lam * o2).T
    o_ref[0] = (_rms_scale(o) * dn_ref[...] * (1.0 - LAM_INIT)).astype(BF16)


def _diff_attention(qa, ka, vat, ka_meta, vat_meta, lq1, lk1, lq2, lk2, diff_norm, qb, cw):
    b, s, _ = qa.shape
    nq = s // qb
    vec = lambda n: pl.BlockSpec((1, n), lambda bi, h, i: (0, 0))
    return pl.pallas_call(
        functools.partial(_attn_kernel, qb=qb, cw=cw),
        grid=(b, DA_HEADS, nq),
        in_specs=[pl.BlockSpec((1, qb, SLAB), lambda bi, h, i: (bi, i, h)),
                  pl.BlockSpec((1, s, SLAB), lambda bi, h, i: (bi, 0, h)),
                  pl.BlockSpec((1, SLAB, s), lambda bi, h, i: (bi, h, 0)),
                  pl.BlockSpec((1, N_META, SLAB), lambda bi, h, i: (0, 0, h)),
                  pl.BlockSpec((1, SLAB, N_META), lambda bi, h, i: (0, h, 0)),
                  vec(HEAD_DIM), vec(HEAD_DIM), vec(HEAD_DIM), vec(HEAD_DIM), vec(SLAB)],
        out_specs=pl.BlockSpec((1, qb, SLAB), lambda bi, h, i: (bi, i, h)),
        out_shape=jax.ShapeDtypeStruct((b, s, MIX_HALF), BF16),
        scratch_shapes=[pltpu.VMEM((SLAB, 2 * qb), BF16), pltpu.VMEM((1, 2 * qb), F32),
                        pltpu.VMEM((SLAB + ONES_ROWS, 2 * qb), F32), pltpu.VMEM((2, cw, cw), F32),
                        pltpu.VMEM((cw, cw), BF16)],
        compiler_params=pltpu.CompilerParams(dimension_semantics=("parallel", "parallel", "arbitrary"),
                                             vmem_limit_bytes=VMEM_LIMIT),
        name="diff_attention",
    )(qa, ka, vat, ka_meta, vat_meta, lq1, lk1, lq2, lk2, diff_norm)


def _gdn_kernel(q_ref, k_ref, v_ref, z_ref, g_ref, beta_ref, gn_ref, g512_ref, s0_ref, o_ref, sfin_ref,
                s_scr, o_scr, *, n_chunks):
    j = pl.program_id(1)
    n_pairs = GDN_HEADS // 2

    @pl.when(j == 0)
    def _():
        s_scr[...] = s0_ref[...]

    row = lax.broadcasted_iota(jnp.int32, (CHUNK, SLAB), 0)
    lane = lax.broadcasted_iota(jnp.int32, (CHUNK, SLAB), 1)
    col = lane % CHUNK
    head_a = lane < HEAD_DIM
    tri_incl = row >= col
    tri_strict = row > col
    eye = (row == col).astype(F32)
    ltri = (lax.broadcasted_iota(jnp.int32, (CHUNK, CHUNK), 0)
            >= lax.broadcasted_iota(jnp.int32, (CHUNK, CHUNK), 1)).astype(F32)
    bd_mask = ((lax.broadcasted_iota(jnp.int32, (SLAB, SLAB), 0) < HEAD_DIM)
               == (lax.broadcasted_iota(jnp.int32, (SLAB, SLAB), 1) < HEAD_DIM))

    def block_diag(x):
        xb = x.astype(BF16)
        zero = jnp.zeros_like(xb)
        return jnp.concatenate([jnp.where(head_a, xb, zero), jnp.where(head_a, zero, xb)], axis=0)

    def pair_mm(x, y):
        return _dot(x.astype(BF16), block_diag(y))

    items = [(c, p) for c in range(n_chunks) for p in range(n_pairs)]
    rows_of = lambda c: slice(CHUNK * c, CHUNK * (c + 1))
    cols_of = lambda p: slice(SLAB * p, SLAB * (p + 1))
    load = lambda ref, c, p: ref[0, rows_of(c), cols_of(p)]

    g_col, decay, k_f, k_beta, k_bd, q_f, vb = {}, {}, {}, {}, {}, {}, {}
    for it in items:
        c, p = it
        g = load(g_ref, c, p)
        g_col[it] = _dot(ltri, g, precision=HIGHEST)
        g_row = jnp.sum(jnp.where(row <= col, g, 0.0), axis=0, keepdims=True)
        decay[it] = jnp.where(tri_incl, jnp.exp(jnp.where(tri_incl, g_col[it] - g_row, 0.0)), 0.0)
        beta = load(beta_ref, c, p).astype(F32)
        k_f[it] = load(k_ref, c, p).astype(F32)
        q_f[it] = load(q_ref, c, p).astype(F32)
        k_beta[it] = k_f[it] * beta
        vb[it] = load(v_ref, c, p).astype(F32) * beta
        k_bd[it] = block_diag(k_f[it])

    a, a_qk = {}, {}
    for it in items:
        lhs = jnp.concatenate([k_beta[it].astype(BF16), q_f[it].astype(BF16)], axis=0)
        kk = _dot_nt(lhs, k_bd[it])
        a[it] = jnp.where(tri_strict, kk[:CHUNK] * decay[it], 0.0)
        a_qk[it] = jnp.where(tri_incl, kk[CHUNK:] * decay[it], 0.0)

    t = {it: eye - a[it] for it in items}
    pw = dict(a)
    for _ in range(5):
        for it in items:
            pw[it] = pair_mm(pw[it], pw[it])
        for it in items:
            t[it] = t[it] + pair_mm(t[it], pw[it])

    u, w, k_g, q_g, e_last = {}, {}, {}, {}, {}
    for it in items:
        e_col = jnp.exp(g_col[it])
        rhs = jnp.concatenate([block_diag(vb[it]), block_diag(k_beta[it] * e_col)], axis=1)
        uw = _dot(t[it].astype(BF16), rhs)
        u[it], w[it] = uw[:, :SLAB], uw[:, SLAB:]
        g_last = g_col[it][CHUNK - 1:CHUNK, :]
        k_g[it] = (k_f[it] * jnp.exp(g_last - g_col[it])).astype(BF16)
        q_g[it] = q_f[it] * e_col
        e_last[it] = jnp.exp(g_last)

    state = [s_scr[p] for p in range(n_pairs)]
    for c in range(n_chunks):
        its = [(c, p) for p in range(n_pairs)]
        ws_qs = [_dot(jnp.concatenate([w[it], q_g[it]], axis=0).astype(BF16), state[it[1]].astype(BF16))
                 for it in its]
        v_new = [u[it] - sq[:CHUNK] for it, sq in zip(its, ws_qs)]
        for it, sq, vn in zip(its, ws_qs, v_new):
            o_scr[rows_of(c), cols_of(it[1])] = sq[CHUNK:] + pair_mm(a_qk[it], vn)
        for it, vn in zip(its, v_new):
            upd = _dot_tn(k_g[it], vn.astype(BF16))
            state[it[1]] = state[it[1]] * e_last[it] + jnp.where(bd_mask, upd, 0.0)
    for p in range(n_pairs):
        s_scr[p] = state[p]

    o = o_scr[...]
    ss = _group_sumsq(o, g512_ref)
    y = o * lax.rsqrt(ss * (1.0 / HEAD_DIM) + EPS) * gn_ref[...]
    o_ref[0] = (y * z_ref[0].astype(F32)).astype(BF16)

    @pl.when(j == pl.num_programs(1) - 1)
    def _():
        sfin_ref[0] = s_scr[...]


def _gated_delta(q, k, v, z, gb, beta, gn_rep, g512, s0, tbg):
    b, s, _ = q.shape
    nb = s // tbg
    n_pairs = GDN_HEADS // 2
    tok = pl.BlockSpec((1, tbg, MIX_HALF), lambda bi, j: (bi, j, 0))
    full = lambda shape: pl.BlockSpec(shape, lambda bi, j: (0,) * len(shape))
    return pl.pallas_call(
        functools.partial(_gdn_kernel, n_chunks=tbg // CHUNK),
        grid=(b, nb),
        in_specs=[tok] * 6 + [full((1, MIX_HALF)), full(g512.shape), full(s0.shape)],
        out_specs=[tok, pl.BlockSpec((1, n_pairs, SLAB, SLAB), lambda bi, j: (bi, 0, 0, 0))],
        out_shape=[jax.ShapeDtypeStruct((b, s, MIX_HALF), BF16),
                   jax.ShapeDtypeStruct((b, n_pairs, SLAB, SLAB), F32)],
        scratch_shapes=[pltpu.VMEM((n_pairs, SLAB, SLAB), F32), pltpu.VMEM((tbg, MIX_HALF), F32)],
        compiler_params=pltpu.CompilerParams(dimension_semantics=("parallel", "arbitrary"),
                                             vmem_limit_bytes=VMEM_LIMIT),
        name="gated_delta",
    )(q, k, v, z, gb, beta, gn_rep, g512, s0)


def _out_router_kernel(x_ref, oa_ref, ob_ref, wo_ref, gain_ref, wr_ref, br_ref, before_ref,
                       h_ref, t_ref, rows_ref, cols_ref, cnt_ref):
    h = x_ref[0] + _dot(jnp.concatenate([oa_ref[0], ob_ref[0]], axis=1), wo_ref[...])
    h_ref[0] = h
    t = _rms_scale(h) * gain_ref[...]
    t_hi = t.astype(BF16)
    t_ref[0] = t_hi
    t_lo = (t - t_hi.astype(F32)).astype(BF16)
    wr = wr_ref[...]
    hi_part = _dot(t_hi, wr)
    logits = hi_part[:, :ROUTE_LANES] + hi_part[:, ROUTE_LANES:] + _dot(t_lo, wr)[:, :ROUTE_LANES] + br_ref[...]

    lt = logits.T
    tb = lt.shape[1]
    row8 = lax.broadcasted_iota(jnp.int32, (EXPERTS_PER_GROUP, tb), 0)
    col_max = lambda v: jnp.max(v, axis=0, keepdims=True)
    col_sum = lambda v: jnp.sum(v, axis=0, keepdims=True)
    first_argmax = lambda v, vmax: jnp.min(jnp.where(v == vmax, row8, EXPERTS_PER_GROUP), axis=0, keepdims=True)

    gl = jnp.where(row8 < N_GROUPS, lt[:EXPERTS_PER_GROUP], NEG_BIG)
    gmax = col_max(gl)
    gsel = first_argmax(gl, gmax)
    psel = 1.0 / col_sum(jnp.exp(gl - gmax))
    el = jnp.zeros((EXPERTS_PER_GROUP, tb), F32)
    for g in range(N_GROUPS):
        lo = EXPERT_ROW0 + EXPERTS_PER_GROUP * g
        el = jnp.where(gsel == g, lt[lo:lo + EXPERTS_PER_GROUP], el)
    m1 = col_max(el)
    i1 = first_argmax(el, m1)
    el2 = jnp.where(row8 == i1, NEG_BIG, el)
    m2 = col_max(el2)
    i2 = first_argmax(el2, m2)
    denom = col_sum(jnp.exp(el - m1))
    p1 = 1.0 / denom
    p2 = jnp.exp(m2 - m1) / denom
    w1 = p1 / (p1 + p2) * psel
    w2 = p2 / (p1 + p2) * psel
    e1 = gsel * EXPERTS_PER_GROUP + i1
    e2 = gsel * EXPERTS_PER_GROUP + i2

    row_e = lax.broadcasted_iota(jnp.int32, (N_EXPERTS, tb), 0)
    hot1 = (row_e == e1).astype(F32)
    hot2 = (row_e == e2).astype(F32)
    both = hot1 + hot2
    n_gran = jnp.ceil(jnp.sum(both, axis=1, keepdims=True) * (1.0 / SLOT_GRAN)) * SLOT_GRAN
    ei = lax.broadcasted_iota(jnp.int32, (N_EXPERTS, N_EXPERTS), 0)
    ej = lax.broadcasted_iota(jnp.int32, (N_EXPERTS, N_EXPERTS), 1)
    seg_start = _dot((ei > ej).astype(F32), jnp.broadcast_to(n_gran, (N_EXPERTS, SLAB)), precision=HIGHEST)[:, :1]
    earlier = _dot(both.astype(BF16), before_ref[...])
    where_to = earlier + seg_start
    pos1 = col_sum(hot1 * where_to)
    pos2 = col_sum(hot2 * where_to)

    info = jnp.concatenate([pos1, pos2, w1, w2, e1.astype(F32), e2.astype(F32),
                            jnp.zeros((SLAB - 6, tb), F32)], axis=0)
    rows_ref[0] = info[:8]
    cols_ref[0] = info.T
    cnt_ref[0] = jnp.broadcast_to(n_gran, (N_EXPERTS, SLAB))


def _out_router(x, oa, ob, w_o, gain, w_route, b_route, before, tb):
    b, s, d = x.shape
    nb = s // tb
    full = lambda shape: pl.BlockSpec(shape, lambda i, j: (0,) * len(shape))
    tokd = pl.BlockSpec((1, tb, d), lambda i, j: (i, j, 0))
    tokh = pl.BlockSpec((1, tb, MIX_HALF), lambda i, j: (i, j, 0))
    return pl.pallas_call(
        _out_router_kernel,
        grid=(b, nb),
        in_specs=[tokd, tokh, tokh, full(w_o.shape), full((1, d)), full(w_route.shape), full((1, ROUTE_LANES)),
                  full(before.shape)],
        out_specs=[tokd, tokd, pl.BlockSpec((1, 8, tb), lambda i, j: (i, 0, j)),
                   pl.BlockSpec((1, tb, SLAB), lambda i, j: (i, j, 0)),
                   pl.BlockSpec((1, N_EXPERTS, SLAB), lambda i, j: (i * nb + j, 0, 0))],
        out_shape=[jax.ShapeDtypeStruct((b, s, d), F32), jax.ShapeDtypeStruct((b, s, d), BF16),
                   jax.ShapeDtypeStruct((b, 8, s), F32), jax.ShapeDtypeStruct((b, s, SLAB), F32),
                   jax.ShapeDtypeStruct((b * nb, N_EXPERTS, SLAB), F32)],
        compiler_params=pltpu.CompilerParams(dimension_semantics=("parallel", "parallel"),
                                             vmem_limit_bytes=VMEM_LIMIT),
        name="out_router",
    )(x, oa, ob, w_o, gain, w_route, b_route, before)


def _granule_copies(table_ref, block, buf, local_ref, global_ref, sem, to_global, n_gran):
    copies = []
    for g in range(n_gran):
        loc = local_ref.at[buf, pl.ds(g * SLOT_GRAN, SLOT_GRAN)]
        glob = global_ref.at[pl.ds(pl.multiple_of(table_ref[block, g], SLOT_GRAN), SLOT_GRAN)]
        copies.append(pltpu.make_async_copy(loc, glob, sem.at[buf]) if to_global
                      else pltpu.make_async_copy(glob, loc, sem.at[buf]))
    return copies


def _dispatch_kernel(gdst_ref, t_ref, rows_ref, xg_init_ref, xg_ref, xs_scr, sem, *, n_slots):
    del xg_init_ref
    blk = pl.program_id(0)
    last = pl.num_programs(0) - 1
    buf = blk % 2
    tb = t_ref.shape[0]
    n_gran = n_slots // SLOT_GRAN
    info = rows_ref[0]
    slot = lax.broadcasted_iota(jnp.int32, (n_slots, tb), 0).astype(F32)
    onehot = jnp.where((slot == info[0:1]) | (slot == info[1:2]), 1.0, 0.0).astype(BF16)
    xs_scr[buf] = _dot(onehot, t_ref[...]).astype(BF16)
    for copy in _granule_copies(gdst_ref, blk, buf, xs_scr, xg_ref, sem, True, n_gran):
        copy.start()

    @pl.when(blk > 0)
    def _():
        for copy in _granule_copies(gdst_ref, blk - 1, 1 - buf, xs_scr, xg_ref, sem, True, n_gran):
            copy.wait()

    @pl.when(blk == last)
    def _():
        for copy in _granule_copies(gdst_ref, blk, buf, xs_scr, xg_ref, sem, True, n_gran):
            copy.wait()


def _dispatch(granule_dst, t, rows, xg_init, tb, n_slots):
    n, d = t.shape
    nblk = n // tb
    nb = rows.shape[2] // tb
    grid_spec = pltpu.PrefetchScalarGridSpec(
        num_scalar_prefetch=1,
        grid=(nblk,),
        in_specs=[pl.BlockSpec((tb, d), lambda i, gd: (i, 0)),
                  pl.BlockSpec((1, 8, tb), lambda i, gd: (i // nb, 0, i % nb)),
                  pl.BlockSpec(memory_space=pl.ANY)],
        out_specs=pl.BlockSpec(memory_space=pl.ANY),
        scratch_shapes=[pltpu.VMEM((2, n_slots, d), BF16), pltpu.SemaphoreType.DMA((2,))],
    )
    return pl.pallas_call(
        functools.partial(_dispatch_kernel, n_slots=n_slots),
        grid_spec=grid_spec,
        out_shape=jax.ShapeDtypeStruct(xg_init.shape, BF16),
        input_output_aliases={3: 0},
        compiler_params=pltpu.CompilerParams(dimension_semantics=("arbitrary",), vmem_limit_bytes=VMEM_LIMIT),
        name="moe_dispatch",
    )(granule_dst, t, rows, xg_init)


def _experts_kernel(tile_expert_ref, n_tiles_ref, x_ref, wg_ref, wu_ref, wd_ref, y_ref):
    del tile_expert_ref

    @pl.when(pl.program_id(0) < n_tiles_ref[0])
    def _():
        x = x_ref[...]
        act = _silu(_dot(x, wg_ref[0])) * _dot(x, wu_ref[0])
        y_ref[...] = _dot(act.astype(BF16), wd_ref[0]).astype(BF16)


def _experts(tile_expert, n_tiles, xg, w_gate, w_up, w_down, tile):
    rows, d = xg.shape
    used = lambda i, te, nt: jnp.minimum(i, nt[0] - 1)
    wspec = lambda shape: pl.BlockSpec((1,) + shape, lambda i, te, nt: (te[used(i, te, nt)], 0, 0))
    grid_spec = pltpu.PrefetchScalarGridSpec(
        num_scalar_prefetch=2,
        grid=(tile_expert.shape[0],),
        in_specs=[pl.BlockSpec((tile, d), lambda i, te, nt: (used(i, te, nt), 0)),
                  wspec((d, D_EXPERT)), wspec((d, D_EXPERT)), wspec((D_EXPERT, d))],
        out_specs=pl.BlockSpec((tile, d), lambda i, te, nt: (used(i, te, nt), 0)),
    )
    return pl.pallas_call(
        _experts_kernel,
        grid_spec=grid_spec,
        out_shape=jax.ShapeDtypeStruct((rows, d), BF16),
        input_output_aliases={2: 0},
        compiler_params=pltpu.CompilerParams(dimension_semantics=("arbitrary",), vmem_limit_bytes=VMEM_LIMIT),
        name="moe_experts",
    )(tile_expert, n_tiles, xg, w_gate, w_up, w_down)


def _combine_kernel(gsrc_ref, yg_ref, cols_ref, h_ref, gain_ref, o_ref, ys_scr, sem, *, n_slots):
    blk = pl.program_id(0)
    last = pl.num_programs(0) - 1
    buf = blk % 2
    tb = h_ref.shape[0]
    n_gran = n_slots // SLOT_GRAN

    @pl.when(blk == 0)
    def _():
        for copy in _granule_copies(gsrc_ref, 0, 0, ys_scr, yg_ref, sem, False, n_gran):
            copy.start()

    @pl.when(blk < last)
    def _():
        for copy in _granule_copies(gsrc_ref, blk + 1, 1 - buf, ys_scr, yg_ref, sem, False, n_gran):
            copy.start()

    info = cols_ref[...]
    slot = lax.broadcasted_iota(jnp.int32, (tb, n_slots), 1).astype(F32)
    weights = (jnp.where(slot == info[:, 0:1], info[:, 2:3], 0.0)
               + jnp.where(slot == info[:, 1:2], info[:, 3:4], 0.0)).astype(BF16)
    for copy in _granule_copies(gsrc_ref, blk, buf, ys_scr, yg_ref, sem, False, n_gran):
        copy.wait()
    y = _dot(weights, ys_scr[buf])
    o_ref[...] = _rms_scale(h_ref[...] + y) * gain_ref[...]


def _combine(granule_dst, yg, cols, h, gain, tb, n_slots):
    n, d = h.shape
    grid_spec = pltpu.PrefetchScalarGridSpec(
        num_scalar_prefetch=1,
        grid=(n // tb,),
        in_specs=[pl.BlockSpec(memory_space=pl.ANY),
                  pl.BlockSpec((tb, SLAB), lambda i, gd: (i, 0)),
                  pl.BlockSpec((tb, d), lambda i, gd: (i, 0)),
                  pl.BlockSpec((1, d), lambda i, gd: (0, 0))],
        out_specs=pl.BlockSpec((tb, d), lambda i, gd: (i, 0)),
        scratch_shapes=[pltpu.VMEM((2, n_slots, d), BF16), pltpu.SemaphoreType.DMA((2,))],
    )
    return pl.pallas_call(
        functools.partial(_combine_kernel, n_slots=n_slots),
        grid_spec=grid_spec,
        out_shape=jax.ShapeDtypeStruct((n, d), F32),
        compiler_params=pltpu.CompilerParams(dimension_semantics=("arbitrary",), vmem_limit_bytes=VMEM_LIMIT),
        name="moe_combine",
    )(granule_dst, yg, cols, h, gain)


def _regroup_plan(n_gran_be, n_slots, tile, n_tiles_max):
    cnt = n_gran_be.astype(jnp.int32)
    region = (jnp.sum(cnt, axis=0) + tile - 1) // tile * tile
    region_end = jnp.cumsum(region)
    first_row = (region_end - region)[None, :] + jnp.cumsum(cnt, axis=0) - cnt
    seg_end = jnp.cumsum(cnt, axis=1)
    seg_start = seg_end - cnt
    g_row = jnp.arange(n_slots // SLOT_GRAN, dtype=jnp.int32) * SLOT_GRAN
    owned = ((g_row[None, :, None] >= seg_start[:, None, :]) & (g_row[None, :, None] < seg_end[:, None, :]))
    in_use = g_row[None, :] < seg_end[:, -1:]
    row = g_row[None, :] + jnp.sum(jnp.where(owned, (first_row - seg_start)[:, None, :], 0), axis=-1)
    zero_row = n_tiles_max * tile
    spill_row = zero_row + SLOT_GRAN + (jnp.arange(cnt.shape[0], dtype=jnp.int32) % 2)[:, None] * n_slots + g_row
    dispatch_dst = jnp.where(in_use, row, spill_row).astype(jnp.int32)
    combine_src = jnp.where(in_use, row, zero_row).astype(jnp.int32)
    tile_row = jnp.arange(n_tiles_max, dtype=jnp.int32) * tile
    tile_expert = jnp.minimum(jnp.sum(tile_row[:, None] >= region_end[None, :], axis=-1), N_EXPERTS - 1)
    n_tiles = (region_end[-1] // tile).astype(jnp.int32)[None]
    return dispatch_dst, combine_src, tile_expert.astype(jnp.int32), n_tiles


def _rope_tables(length):
    pos = jnp.arange(length, dtype=F32)
    inv_freq = ROPE_THETA ** (-jnp.arange(0, HEAD_DIM, 2, dtype=F32) / HEAD_DIM)
    ang = pos[:, None] * inv_freq[None, :]
    cos, sin = jnp.cos(ang), jnp.sin(ang)
    zero = jnp.zeros_like(sin)
    cos_t = jnp.tile(cos, (1, 4))
    sina_t = jnp.tile(jnp.concatenate([-sin, zero], axis=1), (1, 2))
    sinb_t = jnp.tile(jnp.concatenate([zero, sin], axis=1), (1, 2))
    return cos_t, sina_t, sinb_t


def _block(total, want):
    blk = min(total, want)
    assert total % blk == 0, (total, blk)
    return blk


def kernel(x, meta, norm_mix, w_in, lambda_q1, lambda_k1, lambda_q2, lambda_k2, diff_norm, conv_w, a_log, dt_bias,
           gdn_norm, w_out, norm_ffn, w_group, b_group, w_router, b_router, w_gate, w_up, w_down, norm_final):
    b, s, d = x.shape
    assert d == D_MODEL and meta.shape == (N_META, D_MODEL) and s % CHUNK == 0
    assert norm_mix.shape[0] == 1, "single-layer block"
    l = 0

    w = w_in[l]
    w_qk = w[:, :2 * MIX_HALF].astype(BF16)
    w_vt = w[:, 2 * MIX_HALF:3 * MIX_HALF].T.astype(BF16)
    w_b = w[:, 3 * MIX_HALF:6 * MIX_HALF].astype(BF16)
    w_z = w[:, 6 * MIX_HALF:7 * MIX_HALF].astype(BF16)
    c_ab = 7 * MIX_HALF
    lane_pad = lambda a: jnp.pad(a, [(0, 0)] * (a.ndim - 1) + [(0, SLAB - a.shape[-1])])
    w_g = lane_pad(w[:, c_ab:c_ab + 2 * GDN_HEADS]).astype(BF16)
    alog_rep = lane_pad(a_log[l])[None]
    dtb_rep = lane_pad(dt_bias[l])[None]
    gn_rep = jnp.tile(gdn_norm[l], GDN_HEADS)[None]
    g512 = (jnp.arange(MIX_HALF)[:, None] // HEAD_DIM == jnp.arange(MIX_HALF)[None, :] // HEAD_DIM).astype(BF16)
    gain_mix = norm_mix[l][None]
    w_o = w_out[l].astype(BF16)
    route_pad = lambda g, e: jnp.concatenate(
        [g, jnp.zeros(g.shape[:-1] + (EXPERT_ROW0 - N_GROUPS,), F32), e,
         jnp.zeros(g.shape[:-1] + (ROUTE_LANES - EXPERT_ROW0 - N_EXPERTS,), F32)], axis=-1)
    w_route_f = route_pad(w_group[l], w_router[l])
    w_route_hi = w_route_f.astype(BF16)
    w_route = jnp.concatenate([w_route_hi, (w_route_f - w_route_hi.astype(F32)).astype(BF16)], axis=1)
    b_route = route_pad(b_group[l], b_router[l])[None]
    lam_vecs = [v[l][None] for v in (lambda_q1, lambda_k1, lambda_q2, lambda_k2)]
    cos_t, sina_t, sinb_t = _rope_tables(N_META + s)

    meta3 = meta[None]
    _, ka_m, vat_m = _proj_attn(meta3, gain_mix, w_qk, w_vt, cos_t[:N_META], sina_t[:N_META], sinb_t[:N_META],
                                N_META)
    zero_halo = jnp.zeros((1, 1, N_META, d), F32)
    gdn_m = _proj_gdn(meta3, zero_halo, gain_mix, w_b, w_z, w_g, conv_w[l], alog_rep, dtb_rep, g512, N_META)
    pad_m = lambda a: jnp.pad(a, ((0, 0), (0, CHUNK - N_META), (0, 0)))
    s_zero = jnp.zeros((GDN_HEADS // 2, SLAB, SLAB), F32)
    _, s_meta = _gated_delta(*[pad_m(a) for a in gdn_m], gn_rep, g512, s_zero, CHUNK)

    tb_a = _block(s, 1024)
    qa, ka, vat = _proj_attn(x, gain_mix, w_qk, w_vt, cos_t[N_META:], sina_t[N_META:], sinb_t[N_META:], tb_a)
    tb_g = _block(s, 512)
    nb_g = s // tb_g
    tails = x.reshape(b, nb_g, tb_g, d)[:, :-1, tb_g - N_META:, :]
    halo = jnp.concatenate([jnp.broadcast_to(meta[None, None], (b, 1, N_META, d)), tails], axis=1)
    gdn_f = _proj_gdn(x, halo, gain_mix, w_b, w_z, w_g, conv_w[l], alog_rep, dtb_rep, g512, tb_g)

    oa = _diff_attention(qa, ka, vat, ka_m, vat_m, *lam_vecs, diff_norm[l][None], _block(s, 1024), 512)
    ob, _ = _gated_delta(*gdn_f, gn_rep, g512, s_meta[0], _block(s, 256))

    tb_r = _block(s, MOE_BLOCK)
    tok = jnp.arange(tb_r)
    before = (tok[:, None] < tok[None, :]).astype(BF16)
    h1, t, rows, cols, cnt = _out_router(x, oa, ob, w_o, norm_ffn[l][None], w_route, b_route, before, tb_r)

    n = b * s
    nblk = n // tb_r
    n_slots = 2 * tb_r + N_EXPERTS * SLOT_GRAN
    rows_max = 2 * n + (SLOT_GRAN - 1) * N_EXPERTS * nblk + (MOE_TILE - 1) * N_EXPERTS
    n_tiles_max = -(-rows_max // MOE_TILE)
    dispatch_dst, combine_src, tile_expert, n_tiles = _regroup_plan(cnt[:, :, 0], n_slots, MOE_TILE, n_tiles_max)
    buf_rows = n_tiles_max * MOE_TILE + SLOT_GRAN + 2 * n_slots
    xg = _dispatch(dispatch_dst, t.reshape(n, d), rows, jnp.zeros((buf_rows, d), BF16), tb_r, n_slots)
    yg = _experts(tile_expert, n_tiles, xg, w_gate[l].astype(BF16), w_up[l].astype(BF16), w_down[l].astype(BF16),
                  MOE_TILE)
    out = _combine(combine_src, yg, cols.reshape(n, SLAB), h1.reshape(n, d), norm_final[None], tb_r, n_slots)
    return out.reshape(b, s, d)
```

```python
import functools
import math

import jax
import jax.numpy as jnp
from jax import lax
from jax.experimental import pallas as pl
from jax.experimental.pallas import tpu as pltpu

F32 = jnp.float32
BF16 = jnp.bfloat16
HIGHEST = lax.Precision.HIGHEST

D_MODEL = 1024
N_META = 16
CHUNK = 64
EPS = 1e-6
ROPE_THETA = 10000.0
HEAD_DIM = 64
SLAB = 128
DA_HEADS = 4
GDN_HEADS = 8
MIX_HALF = 512
N_GROUPS = 4
EXPERTS_PER_GROUP = 8
N_EXPERTS = 32
D_EXPERT = 256
CONV_K = 4
LAM_INIT = 0.8 - 0.6 * math.exp(-0.3 * 0)
ROUTE_LANES = 128
EXPERT_ROW0 = 8
SLOT_GRAN = 16
MOE_BLOCK = 512
MOE_TILE = 512
ROW_STRIP = 64
NEG_BIG = -1e30
LOG2E = math.log2(math.e)
VMEM_LIMIT = 56 * 1024 * 1024


def _dot(a, b, precision=None):
    return jnp.dot(a, b, preferred_element_type=F32, precision=precision)


def _dot_nt(a, b):
    return lax.dot_general(a, b, (((1,), (1,)), ((), ())), preferred_element_type=F32)


def _dot_tn(a, b):
    return lax.dot_general(a, b, (((0,), (0,)), ((), ())), preferred_element_type=F32)


def _rms_scale(x):
    return x * lax.rsqrt(jnp.mean(x * x, axis=-1, keepdims=True) + EPS)


def _silu(x):
    return x * jax.nn.sigmoid(x)


def _group_sumsq(x, g_ref):
    return _dot((x * x).astype(BF16), g_ref[...])


def _proj_attn_kernel(x_ref, gain_ref, w_ref, cos_ref, sina_ref, sinb_ref, qa_ref, ka_ref, va_ref):
    u = (_rms_scale(x_ref[0]) * gain_ref[...]).astype(BF16)
    proj = _dot(u, w_ref[...])
    va_ref[0] = proj[:, 2 * MIX_HALF:].astype(BF16)
    cos, sina, sinb = cos_ref[...], sina_ref[...], sinb_ref[...]
    for s in range(2 * DA_HEADS):
        xs = proj[:, SLAB * s:SLAB * (s + 1)]
        r = xs * cos + pltpu.roll(xs, SLAB - 32, 1) * sina + pltpu.roll(xs, 32, 1) * sinb
        if s < DA_HEADS:
            qa_ref[0, :, SLAB * s:SLAB * (s + 1)] = (r * (HEAD_DIM ** -0.5 * LOG2E)).astype(BF16)
        else:
            t = s - DA_HEADS
            ka_ref[0, :, SLAB * t:SLAB * (t + 1)] = r.astype(BF16)


def _proj_attn(x, gain, w_a, cos, sina, sinb, tb):
    b, s, d = x.shape
    nb = s // tb
    slab_out = jax.ShapeDtypeStruct((b, s, MIX_HALF), BF16)
    full = lambda shape: pl.BlockSpec(shape, lambda i, j: (0,) * len(shape))
    tok = pl.BlockSpec((1, tb, MIX_HALF), lambda i, j: (i, j, 0))
    tab = pl.BlockSpec((tb, SLAB), lambda i, j: (j, 0))
    return pl.pallas_call(
        _proj_attn_kernel,
        grid=(b, nb),
        in_specs=[pl.BlockSpec((1, tb, d), lambda i, j: (i, j, 0)), full((1, d)), full(w_a.shape), tab, tab, tab],
        out_specs=[tok, tok, tok],
        out_shape=[slab_out, slab_out, slab_out],
        compiler_params=pltpu.CompilerParams(dimension_semantics=("parallel", "parallel"),
                                             vmem_limit_bytes=VMEM_LIMIT),
        name="proj_attn",
    )(x, gain, w_a, cos, sina, sinb)


def _proj_gdn_kernel(x_ref, halo_ref, gain_ref, wb_ref, wz_ref, wg_ref, convw_ref, alog_ref, dtb_ref, g_ref,
                     q_ref, k_ref, v_ref, z_ref, gb_ref, beta_ref, pb_scr):
    halo_rows = halo_ref.shape[2]
    xe = jnp.concatenate([halo_ref[0, 0], x_ref[0]], axis=0)
    ue = (_rms_scale(xe) * gain_ref[...]).astype(BF16)
    u = ue[halo_rows:]
    cw = convw_ref[...]
    outs = (q_ref, k_ref, v_ref)
    for part in range(3):
        cols = slice(MIX_HALF * part, MIX_HALF * (part + 1))
        pb_scr[...] = _dot(ue, wb_ref[:, cols])
        c = cw[:, cols]
        tb = pb_scr.shape[0] - halo_rows
        y = pb_scr[halo_rows:, :] * c[CONV_K - 1:CONV_K]
        for back in range(1, CONV_K):
            tap = CONV_K - 1 - back
            y = y + pb_scr[pl.ds(halo_rows - back, tb), :] * c[tap:tap + 1]
        y = _silu(y)
        if part < 2:
            y = y * lax.rsqrt(_group_sumsq(y, g_ref) + EPS)
            if part == 0:
                y = y * (HEAD_DIM ** -0.5)
        outs[part][0] = y.astype(BF16)
    z_ref[0] = _silu(_dot(u, wz_ref[...])).astype(BF16)
    gates = _dot(u, wg_ref[...])
    ab = gates + dtb_ref[...]
    softplus = jnp.maximum(ab, 0.0) + jnp.log1p(jnp.exp(-jnp.abs(ab)))
    g = -jnp.exp(alog_ref[...]) * softplus
    beta = jax.nn.sigmoid(gates)
    tb = g.shape[0]
    first_head = lax.broadcasted_iota(jnp.int32, (tb, SLAB), 1) < HEAD_DIM
    spread = lambda v, h: jnp.broadcast_to(v[:, h:h + 1], (tb, SLAB))
    for p in range(GDN_HEADS // 2):
        cols = slice(SLAB * p, SLAB * (p + 1))
        gb_ref[0, :, cols] = jnp.where(first_head, spread(g, 2 * p), spread(g, 2 * p + 1))
        beta_ref[0, :, cols] = jnp.where(first_head, spread(beta, GDN_HEADS + 2 * p),
                                         spread(beta, GDN_HEADS + 2 * p + 1)).astype(BF16)


def _proj_gdn(x, halo, gain, w_b, w_z, w_g, conv_w, alog_rep, dtb_rep, g512, tb):
    b, s, d = x.shape
    nb = s // tb
    full = lambda shape: pl.BlockSpec(shape, lambda i, j: (0,) * len(shape))
    tok = pl.BlockSpec((1, tb, MIX_HALF), lambda i, j: (i, j, 0))
    bf = jax.ShapeDtypeStruct((b, s, MIX_HALF), BF16)
    return pl.pallas_call(
        _proj_gdn_kernel,
        grid=(b, nb),
        in_specs=[pl.BlockSpec((1, tb, d), lambda i, j: (i, j, 0)),
                  pl.BlockSpec((1, 1) + halo.shape[2:], lambda i, j: (i, j, 0, 0)),
                  full((1, d)), full(w_b.shape), full(w_z.shape), full(w_g.shape), full(conv_w.shape),
                  full((1, SLAB)), full((1, SLAB)), full(g512.shape)],
        out_specs=[tok] * 6,
        out_shape=[bf, bf, bf, bf, jax.ShapeDtypeStruct((b, s, MIX_HALF), F32), bf],
        scratch_shapes=[pltpu.VMEM((halo.shape[2] + tb, MIX_HALF), F32)],
        compiler_params=pltpu.CompilerParams(dimension_semantics=("parallel", "parallel"),
                                             vmem_limit_bytes=VMEM_LIMIT),
        name="proj_gdn",
    )(x, halo, gain, w_b, w_z, w_g, conv_w, alog_rep, dtb_rep, g512)


def _attn_kernel(q_ref, k_ref, v_ref, km_ref, vm_ref, lq1_ref, lk1_ref, lq2_ref, lk2_ref, dn_ref, o_ref,
                 qq_scr, m_scr, acc_scr, st_scr, *, qb, cw):
    i = pl.program_id(2)
    n_col = 2 * qb // cw
    per_map = qb // cw
    q = q_ref[0]
    lane = lax.broadcasted_iota(jnp.int32, (qb, SLAB), 1)
    zero = jnp.zeros_like(q)
    qq_scr[...] = jnp.concatenate([jnp.where(lane < HEAD_DIM, q, zero), jnp.where(lane < HEAD_DIM, zero, q)], axis=0)

    def with_ones(v_blk):
        return jnp.concatenate([v_blk, jnp.ones_like(v_blk)], axis=1)

    v1_meta = with_ones(vm_ref[0])
    for c in range(n_col):
        rs = slice(cw * c, cw * (c + 1))
        s = _dot_nt(qq_scr[rs, :], km_ref[0])
        m_new = jnp.max(s, axis=1, keepdims=True)
        acc_scr[rs, :] = _dot(jnp.exp2((s - m_new).astype(BF16)), v1_meta)
        m_scr[rs, :] = jnp.broadcast_to(m_new, (cw, SLAB))

    def scores_into(slot, start, c):
        st_scr[slot] = _dot_nt(qq_scr[cw * c:cw * (c + 1), :], k_ref[0, pl.ds(start, cw), :])

    def softmax_pv(slot, v1, c, masked):
        alphas = []
        for t in range(cw // ROW_STRIP):
            ls = slice(ROW_STRIP * t, ROW_STRIP * (t + 1))
            gs = slice(cw * c + ROW_STRIP * t, cw * c + ROW_STRIP * (t + 1))
            s = st_scr[slot, ls, :]
            if masked:
                q_chunk = (lax.broadcasted_iota(jnp.int32, s.shape, 0) + ROW_STRIP * t) // CHUNK
                k_chunk = lax.broadcasted_iota(jnp.int32, s.shape, 1) // CHUNK
                s = jnp.where(k_chunk <= q_chunk, s, NEG_BIG)
            parts = [s[:, SLAB * k:SLAB * (k + 1)] for k in range(cw // SLAB)]
            lane_max = functools.reduce(jnp.maximum, parts)
            m_prev = m_scr[gs, :]
            m_new = jnp.maximum(m_prev, jnp.max(lane_max, axis=1, keepdims=True))
            for k, part in enumerate(parts):
                st_scr[slot, ls, SLAB * k:SLAB * (k + 1)] = jnp.exp2(part - m_new)
            alphas.append(jnp.exp2(m_prev - m_new))
            m_scr[gs, :] = m_new
        rs = slice(cw * c, cw * (c + 1))
        alpha = jnp.concatenate(alphas, axis=0)
        pv = _dot(st_scr[slot].astype(BF16), v1)
        acc_scr[rs, :] = acc_scr[rs, :] * jnp.concatenate([alpha, alpha], axis=1) + pv

    parity = [0]
    scores_into(0, 0, 0)

    def key_block(start, groups, masked_groups, following):
        v1 = with_ones(v_ref[0, pl.ds(start, cw), :])
        for idx, c in enumerate(groups):
            slot = parity[0]
            if idx + 1 < len(groups):
                scores_into(1 - slot, start, groups[idx + 1])
            elif following is not None:
                scores_into(1 - slot, *following)
            softmax_pv(slot, v1, c, c in masked_groups)
            parity[0] = 1 - slot

    all_groups = list(range(n_col))

    def full_blocks(j, carry):
        for d in range(per_map):
            start = pl.multiple_of((per_map * j + d) * cw, cw)
            key_block(start, all_groups, (), (start + cw, 0))
        return carry

    lax.fori_loop(0, i, full_blocks, 0)
    for d in range(per_map):
        start = pl.multiple_of((per_map * i + d) * cw, cw)
        groups = [c for c in all_groups if c % per_map >= d]
        following = (start + cw, d + 1) if d + 1 < per_map else None
        key_block(start, groups, [c for c in groups if c % per_map == d], following)

    acc = acc_scr[...]
    o1 = acc[:qb, :SLAB] / acc[:qb, SLAB:]
    o2 = acc[qb:, :SLAB] / acc[qb:, SLAB:]
    lam = (jnp.exp(jnp.sum(lq1_ref[...] * lk1_ref[...], axis=1, keepdims=True))
           - jnp.exp(jnp.sum(lq2_ref[...] * lk2_ref[...], axis=1, keepdims=True)) + LAM_INIT)
    o = o1 - lam * o2
    o_ref[0] = (_rms_scale(o) * dn_ref[...] * (1.0 - LAM_INIT)).astype(BF16)


def _diff_attention(qa, ka, va, ka_meta, va_meta, lq1, lk1, lq2, lk2, diff_norm, qb, cw):
    b, s, _ = qa.shape
    nq = s // qb
    vec = lambda n: pl.BlockSpec((1, n), lambda bi, h, i: (0, 0))
    seq = pl.BlockSpec((1, s, SLAB), lambda bi, h, i: (bi, 0, h))
    meta = pl.BlockSpec((1, N_META, SLAB), lambda bi, h, i: (0, 0, h))
    return pl.pallas_call(
        functools.partial(_attn_kernel, qb=qb, cw=cw),
        grid=(b, DA_HEADS, nq),
        in_specs=[pl.BlockSpec((1, qb, SLAB), lambda bi, h, i: (bi, i, h)), seq, seq, meta, meta,
                  vec(HEAD_DIM), vec(HEAD_DIM), vec(HEAD_DIM), vec(HEAD_DIM), vec(SLAB)],
        out_specs=pl.BlockSpec((1, qb, SLAB), lambda bi, h, i: (bi, i, h)),
        out_shape=jax.ShapeDtypeStruct((b, s, MIX_HALF), BF16),
        scratch_shapes=[pltpu.VMEM((2 * qb, SLAB), BF16), pltpu.VMEM((2 * qb, SLAB), F32),
                        pltpu.VMEM((2 * qb, 2 * SLAB), F32), pltpu.VMEM((2, cw, cw), F32)],
        compiler_params=pltpu.CompilerParams(dimension_semantics=("parallel", "parallel", "arbitrary"),
                                             vmem_limit_bytes=VMEM_LIMIT),
        name="diff_attention",
    )(qa, ka, va, ka_meta, va_meta, lq1, lk1, lq2, lk2, diff_norm)


def _gdn_kernel(q_ref, k_ref, v_ref, z_ref, g_ref, beta_ref, gn_ref, g512_ref, s0_ref, o_ref, sfin_ref,
                s_scr, o_scr, *, n_chunks):
    j = pl.program_id(1)
    n_pairs = GDN_HEADS // 2

    @pl.when(j == 0)
    def _():
        s_scr[...] = s0_ref[...]

    row = lax.broadcasted_iota(jnp.int32, (CHUNK, SLAB), 0)
    lane = lax.broadcasted_iota(jnp.int32, (CHUNK, SLAB), 1)
    col = lane % CHUNK
    head_a = lane < HEAD_DIM
    tri_incl = row >= col
    tri_strict = row > col
    eye = (row == col).astype(F32)
    ltri = (lax.broadcasted_iota(jnp.int32, (CHUNK, CHUNK), 0)
            >= lax.broadcasted_iota(jnp.int32, (CHUNK, CHUNK), 1)).astype(F32)
    bd_mask = ((lax.broadcasted_iota(jnp.int32, (SLAB, SLAB), 0) < HEAD_DIM)
               == (lax.broadcasted_iota(jnp.int32, (SLAB, SLAB), 1) < HEAD_DIM))

    def block_diag(x):
        xb = x.astype(BF16)
        zero = jnp.zeros_like(xb)
        return jnp.concatenate([jnp.where(head_a, xb, zero), jnp.where(head_a, zero, xb)], axis=0)

    def pair_mm(x, y):
        return _dot(x.astype(BF16), block_diag(y))

    items = [(c, p) for c in range(n_chunks) for p in range(n_pairs)]
    rows_of = lambda c: slice(CHUNK * c, CHUNK * (c + 1))
    cols_of = lambda p: slice(SLAB * p, SLAB * (p + 1))
    load = lambda ref, c, p: ref[0, rows_of(c), cols_of(p)]

    g_col, decay, k_f, k_beta, k_bd, q_f, vb = {}, {}, {}, {}, {}, {}, {}
    for it in items:
        c, p = it
        g = load(g_ref, c, p)
        g_col[it] = _dot(ltri, g, precision=HIGHEST)
        g_row = jnp.sum(jnp.where(row <= col, g, 0.0), axis=0, keepdims=True)
        decay[it] = jnp.where(tri_incl, jnp.exp(jnp.where(tri_incl, g_col[it] - g_row, 0.0)), 0.0)
        beta = load(beta_ref, c, p).astype(F32)
        k_f[it] = load(k_ref, c, p).astype(F32)
        q_f[it] = load(q_ref, c, p).astype(F32)
        k_beta[it] = k_f[it] * beta
        vb[it] = load(v_ref, c, p).astype(F32) * beta
        k_bd[it] = block_diag(k_f[it])

    a, a_qk = {}, {}
    for it in items:
        lhs = jnp.concatenate([k_beta[it].astype(BF16), q_f[it].astype(BF16)], axis=0)
        kk = _dot_nt(lhs, k_bd[it])
        a[it] = jnp.where(tri_strict, kk[:CHUNK] * decay[it], 0.0)
        a_qk[it] = jnp.where(tri_incl, kk[CHUNK:] * decay[it], 0.0)

    t = {it: eye - a[it] for it in items}
    pw = dict(a)
    for _ in range(5):
        for it in items:
            pw[it] = pair_mm(pw[it], pw[it])
        for it in items:
            t[it] = t[it] + pair_mm(t[it], pw[it])

    u, w, k_g, q_g, e_last = {}, {}, {}, {}, {}
    for it in items:
        e_col = jnp.exp(g_col[it])
        rhs = jnp.concatenate([block_diag(vb[it]), block_diag(k_beta[it] * e_col)], axis=1)
        uw = _dot(t[it].astype(BF16), rhs)
        u[it], w[it] = uw[:, :SLAB], uw[:, SLAB:]
        g_last = g_col[it][CHUNK - 1:CHUNK, :]
        k_g[it] = (k_f[it] * jnp.exp(g_last - g_col[it])).astype(BF16)
        q_g[it] = q_f[it] * e_col
        e_last[it] = jnp.exp(g_last)

    state = [s_scr[p] for p in range(n_pairs)]
    for c in range(n_chunks):
        its = [(c, p) for p in range(n_pairs)]
        ws_qs = [_dot(jnp.concatenate([w[it], q_g[it]], axis=0).astype(BF16), state[it[1]].astype(BF16))
                 for it in its]
        v_new = [u[it] - sq[:CHUNK] for it, sq in zip(its, ws_qs)]
        for it, sq, vn in zip(its, ws_qs, v_new):
            o_scr[rows_of(c), cols_of(it[1])] = sq[CHUNK:] + pair_mm(a_qk[it], vn)
        for it, vn in zip(its, v_new):
            upd = _dot_tn(k_g[it], vn.astype(BF16))
            state[it[1]] = state[it[1]] * e_last[it] + jnp.where(bd_mask, upd, 0.0)
    for p in range(n_pairs):
        s_scr[p] = state[p]

    o = o_scr[...]
    ss = _group_sumsq(o, g512_ref)
    y = o * lax.rsqrt(ss * (1.0 / HEAD_DIM) + EPS) * gn_ref[...]
    o_ref[0] = (y * z_ref[0].astype(F32)).astype(BF16)

    @pl.when(j == pl.num_programs(1) - 1)
    def _():
        sfin_ref[0] = s_scr[...]


def _gated_delta(q, k, v, z, gb, beta, gn_rep, g512, s0, tbg):
    b, s, _ = q.shape
    nb = s // tbg
    n_pairs = GDN_HEADS // 2
    tok = pl.BlockSpec((1, tbg, MIX_HALF), lambda bi, j: (bi, j, 0))
    full = lambda shape: pl.BlockSpec(shape, lambda bi, j: (0,) * len(shape))
    return pl.pallas_call(
        functools.partial(_gdn_kernel, n_chunks=tbg // CHUNK),
        grid=(b, nb),
        in_specs=[tok] * 6 + [full((1, MIX_HALF)), full(g512.shape), full(s0.shape)],
        out_specs=[tok, pl.BlockSpec((1, n_pairs, SLAB, SLAB), lambda bi, j: (bi, 0, 0, 0))],
        out_shape=[jax.ShapeDtypeStruct((b, s, MIX_HALF), BF16),
                   jax.ShapeDtypeStruct((b, n_pairs, SLAB, SLAB), F32)],
        scratch_shapes=[pltpu.VMEM((n_pairs, SLAB, SLAB), F32), pltpu.VMEM((tbg, MIX_HALF), F32)],
        compiler_params=pltpu.CompilerParams(dimension_semantics=("parallel", "arbitrary"),
                                             vmem_limit_bytes=VMEM_LIMIT),
        name="gated_delta",
    )(q, k, v, z, gb, beta, gn_rep, g512, s0)


def _out_router_kernel(x_ref, oa_ref, ob_ref, wo_ref, gain_ref, wr_ref, br_ref, before_ref,
                       h_ref, t_ref, rows_ref, cols_ref, cnt_ref):
    h = x_ref[0] + _dot(jnp.concatenate([oa_ref[0], ob_ref[0]], axis=1), wo_ref[...])
    h_ref[0] = h
    t = _rms_scale(h) * gain_ref[...]
    t_hi = t.astype(BF16)
    t_ref[0] = t_hi
    t_lo = (t - t_hi.astype(F32)).astype(BF16)
    wr = wr_ref[...]
    hi_part = _dot(t_hi, wr)
    logits = hi_part[:, :ROUTE_LANES] + hi_part[:, ROUTE_LANES:] + _dot(t_lo, wr)[:, :ROUTE_LANES] + br_ref[...]

    lt = logits.T
    tb = lt.shape[1]
    row8 = lax.broadcasted_iota(jnp.int32, (EXPERTS_PER_GROUP, tb), 0)
    col_max = lambda v: jnp.max(v, axis=0, keepdims=True)
    col_sum = lambda v: jnp.sum(v, axis=0, keepdims=True)
    first_argmax = lambda v, vmax: jnp.min(jnp.where(v == vmax, row8, EXPERTS_PER_GROUP), axis=0, keepdims=True)

    gl = jnp.where(row8 < N_GROUPS, lt[:EXPERTS_PER_GROUP], NEG_BIG)
    gmax = col_max(gl)
    gsel = first_argmax(gl, gmax)
    psel = 1.0 / col_sum(jnp.exp(gl - gmax))
    el = jnp.zeros((EXPERTS_PER_GROUP, tb), F32)
    for g in range(N_GROUPS):
        lo = EXPERT_ROW0 + EXPERTS_PER_GROUP * g
        el = jnp.where(gsel == g, lt[lo:lo + EXPERTS_PER_GROUP], el)
    m1 = col_max(el)
    i1 = first_argmax(el, m1)
    el2 = jnp.where(row8 == i1, NEG_BIG, el)
    m2 = col_max(el2)
    i2 = first_argmax(el2, m2)
    denom = col_sum(jnp.exp(el - m1))
    p1 = 1.0 / denom
    p2 = jnp.exp(m2 - m1) / denom
    w1 = p1 / (p1 + p2) * psel
    w2 = p2 / (p1 + p2) * psel
    e1 = gsel * EXPERTS_PER_GROUP + i1
    e2 = gsel * EXPERTS_PER_GROUP + i2

    row_e = lax.broadcasted_iota(jnp.int32, (N_EXPERTS, tb), 0)
    hot1 = (row_e == e1).astype(F32)
    hot2 = (row_e == e2).astype(F32)
    both = hot1 + hot2
    n_gran = jnp.ceil(jnp.sum(both, axis=1, keepdims=True) * (1.0 / SLOT_GRAN)) * SLOT_GRAN
    ei = lax.broadcasted_iota(jnp.int32, (N_EXPERTS, N_EXPERTS), 0)
    ej = lax.broadcasted_iota(jnp.int32, (N_EXPERTS, N_EXPERTS), 1)
    seg_start = _dot((ei > ej).astype(F32), jnp.broadcast_to(n_gran, (N_EXPERTS, SLAB)), precision=HIGHEST)[:, :1]
    earlier = _dot(both.astype(BF16), before_ref[...])
    where_to = earlier + seg_start
    pos1 = col_sum(hot1 * where_to)
    pos2 = col_sum(hot2 * where_to)

    info = jnp.concatenate([pos1, pos2, w1, w2, e1.astype(F32), e2.astype(F32),
                            jnp.zeros((SLAB - 6, tb), F32)], axis=0)
    rows_ref[0] = info[:8]
    cols_ref[0] = info.T
    cnt_ref[0] = jnp.broadcast_to(n_gran, (N_EXPERTS, SLAB))


def _out_router(x, oa, ob, w_o, gain, w_route, b_route, before, tb):
    b, s, d = x.shape
    nb = s // tb
    full = lambda shape: pl.BlockSpec(shape, lambda i, j: (0,) * len(shape))
    tokd = pl.BlockSpec((1, tb, d), lambda i, j: (i, j, 0))
    tokh = pl.BlockSpec((1, tb, MIX_HALF), lambda i, j: (i, j, 0))
    return pl.pallas_call(
        _out_router_kernel,
        grid=(b, nb),
        in_specs=[tokd, tokh, tokh, full(w_o.shape), full((1, d)), full(w_route.shape), full((1, ROUTE_LANES)),
                  full(before.shape)],
        out_specs=[tokd, tokd, pl.BlockSpec((1, 8, tb), lambda i, j: (i, 0, j)),
                   pl.BlockSpec((1, tb, SLAB), lambda i, j: (i, j, 0)),
                   pl.BlockSpec((1, N_EXPERTS, SLAB), lambda i, j: (i * nb + j, 0, 0))],
        out_shape=[jax.ShapeDtypeStruct((b, s, d), F32), jax.ShapeDtypeStruct((b, s, d), BF16),
                   jax.ShapeDtypeStruct((b, 8, s), F32), jax.ShapeDtypeStruct((b, s, SLAB), F32),
                   jax.ShapeDtypeStruct((b * nb, N_EXPERTS, SLAB), F32)],
        compiler_params=pltpu.CompilerParams(dimension_semantics=("parallel", "parallel"),
                                             vmem_limit_bytes=VMEM_LIMIT),
        name="out_router",
    )(x, oa, ob, w_o, gain, w_route, b_route, before)


def _granule_copies(table_ref, block, buf, local_ref, global_ref, sem, to_global, n_gran):
    copies = []
    for g in range(n_gran):
        loc = local_ref.at[buf, pl.ds(g * SLOT_GRAN, SLOT_GRAN)]
        glob = global_ref.at[pl.ds(pl.multiple_of(table_ref[block, g], SLOT_GRAN), SLOT_GRAN)]
        copies.append(pltpu.make_async_copy(loc, glob, sem.at[buf]) if to_global
                      else pltpu.make_async_copy(glob, loc, sem.at[buf]))
    return copies


def _dispatch_kernel(gdst_ref, t_ref, rows_ref, xg_init_ref, xg_ref, xs_scr, sem, *, n_slots):
    del xg_init_ref
    blk = pl.program_id(0)
    last = pl.num_programs(0) - 1
    buf = blk % 2
    tb = t_ref.shape[0]
    n_gran = n_slots // SLOT_GRAN
    info = rows_ref[0]
    slot = lax.broadcasted_iota(jnp.int32, (n_slots, tb), 0).astype(F32)
    onehot = jnp.where((slot == info[0:1]) | (slot == info[1:2]), 1.0, 0.0).astype(BF16)
    xs_scr[buf] = _dot(onehot, t_ref[...]).astype(BF16)
    for copy in _granule_copies(gdst_ref, blk, buf, xs_scr, xg_ref, sem, True, n_gran):
        copy.start()

    @pl.when(blk > 0)
    def _():
        for copy in _granule_copies(gdst_ref, blk - 1, 1 - buf, xs_scr, xg_ref, sem, True, n_gran):
            copy.wait()

    @pl.when(blk == last)
    def _():
        for copy in _granule_copies(gdst_ref, blk, buf, xs_scr, xg_ref, sem, True, n_gran):
            copy.wait()


def _dispatch(granule_dst, t, rows, xg_init, tb, n_slots):
    n, d = t.shape
    nblk = n // tb
    nb = rows.shape[2] // tb
    grid_spec = pltpu.PrefetchScalarGridSpec(
        num_scalar_prefetch=1,
        grid=(nblk,),
        in_specs=[pl.BlockSpec((tb, d), lambda i, gd: (i, 0)),
                  pl.BlockSpec((1, 8, tb), lambda i, gd: (i // nb, 0, i % nb)),
                  pl.BlockSpec(memory_space=pl.ANY)],
        out_specs=pl.BlockSpec(memory_space=pl.ANY),
        scratch_shapes=[pltpu.VMEM((2, n_slots, d), BF16), pltpu.SemaphoreType.DMA((2,))],
    )
    return pl.pallas_call(
        functools.partial(_dispatch_kernel, n_slots=n_slots),
        grid_spec=grid_spec,
        out_shape=jax.ShapeDtypeStruct(xg_init.shape, BF16),
        input_output_aliases={3: 0},
        compiler_params=pltpu.CompilerParams(dimension_semantics=("arbitrary",), vmem_limit_bytes=VMEM_LIMIT),
        name="moe_dispatch",
    )(granule_dst, t, rows, xg_init)


def _experts_kernel(tile_expert_ref, n_tiles_ref, x_ref, wg_ref, wu_ref, wd_ref, y_ref):
    del tile_expert_ref

    @pl.when(pl.program_id(0) < n_tiles_ref[0])
    def _():
        x = x_ref[...]
        act = _silu(_dot(x, wg_ref[0])) * _dot(x, wu_ref[0])
        y_ref[...] = _dot(act.astype(BF16), wd_ref[0]).astype(BF16)


def _experts(tile_expert, n_tiles, xg, w_gate, w_up, w_down, tile):
    rows, d = xg.shape
    used = lambda i, te, nt: jnp.minimum(i, nt[0] - 1)
    wspec = lambda shape: pl.BlockSpec((1,) + shape, lambda i, te, nt: (te[used(i, te, nt)], 0, 0))
    grid_spec = pltpu.PrefetchScalarGridSpec(
        num_scalar_prefetch=2,
        grid=(tile_expert.shape[0],),
        in_specs=[pl.BlockSpec((tile, d), lambda i, te, nt: (used(i, te, nt), 0)),
                  wspec((d, D_EXPERT)), wspec((d, D_EXPERT)), wspec((D_EXPERT, d))],
        out_specs=pl.BlockSpec((tile, d), lambda i, te, nt: (used(i, te, nt), 0)),
    )
    return pl.pallas_call(
        _experts_kernel,
        grid_spec=grid_spec,
        out_shape=jax.ShapeDtypeStruct((rows, d), BF16),
        input_output_aliases={2: 0},
        compiler_params=pltpu.CompilerParams(dimension_semantics=("arbitrary",), vmem_limit_bytes=VMEM_LIMIT),
        name="moe_experts",
    )(tile_expert, n_tiles, xg, w_gate, w_up, w_down)


def _combine_kernel(gsrc_ref, yg_ref, cols_ref, h_ref, gain_ref, o_ref, ys_scr, sem, *, n_slots):
    blk = pl.program_id(0)
    last = pl.num_programs(0) - 1
    buf = blk % 2
    tb = h_ref.shape[0]
    n_gran = n_slots // SLOT_GRAN

    @pl.when(blk == 0)
    def _():
        for copy in _granule_copies(gsrc_ref, 0, 0, ys_scr, yg_ref, sem, False, n_gran):
            copy.start()

    @pl.when(blk < last)
    def _():
        for copy in _granule_copies(gsrc_ref, blk + 1, 1 - buf, ys_scr, yg_ref, sem, False, n_gran):
            copy.start()

    info = cols_ref[...]
    slot = lax.broadcasted_iota(jnp.int32, (tb, n_slots), 1).astype(F32)
    weights = (jnp.where(slot == info[:, 0:1], info[:, 2:3], 0.0)
               + jnp.where(slot == info[:, 1:2], info[:, 3:4], 0.0)).astype(BF16)
    for copy in _granule_copies(gsrc_ref, blk, buf, ys_scr, yg_ref, sem, False, n_gran):
        copy.wait()
    y = _dot(weights, ys_scr[buf])
    o_ref[...] = _rms_scale(h_ref[...] + y) * gain_ref[...]


def _combine(granule_dst, yg, cols, h, gain, tb, n_slots):
    n, d = h.shape
    grid_spec = pltpu.PrefetchScalarGridSpec(
        num_scalar_prefetch=1,
        grid=(n // tb,),
        in_specs=[pl.BlockSpec(memory_space=pl.ANY),
                  pl.BlockSpec((tb, SLAB), lambda i, gd: (i, 0)),
                  pl.BlockSpec((tb, d), lambda i, gd: (i, 0)),
                  pl.BlockSpec((1, d), lambda i, gd: (0, 0))],
        out_specs=pl.BlockSpec((tb, d), lambda i, gd: (i, 0)),
        scratch_shapes=[pltpu.VMEM((2, n_slots, d), BF16), pltpu.SemaphoreType.DMA((2,))],
    )
    return pl.pallas_call(
        functools.partial(_combine_kernel, n_slots=n_slots),
        grid_spec=grid_spec,
        out_shape=jax.ShapeDtypeStruct((n, d), F32),
        compiler_params=pltpu.CompilerParams(dimension_semantics=("arbitrary",), vmem_limit_bytes=VMEM_LIMIT),
        name="moe_combine",
    )(granule_dst, yg, cols, h, gain)


def _regroup_plan(n_gran_be, n_slots, tile, n_tiles_max):
    cnt = n_gran_be.astype(jnp.int32)
    region = (jnp.sum(cnt, axis=0) + tile - 1) // tile * tile
    region_end = jnp.cumsum(region)
    first_row = (region_end - region)[None, :] + jnp.cumsum(cnt, axis=0) - cnt
    seg_end = jnp.cumsum(cnt, axis=1)
    seg_start = seg_end - cnt
    g_row = jnp.arange(n_slots // SLOT_GRAN, dtype=jnp.int32) * SLOT_GRAN
    owned = ((g_row[None, :, None] >= seg_start[:, None, :]) & (g_row[None, :, None] < seg_end[:, None, :]))
    in_use = g_row[None, :] < seg_end[:, -1:]
    row = g_row[None, :] + jnp.sum(jnp.where(owned, (first_row - seg_start)[:, None, :], 0), axis=-1)
    zero_row = n_tiles_max * tile
    spill_row = zero_row + SLOT_GRAN + (jnp.arange(cnt.shape[0], dtype=jnp.int32) % 2)[:, None] * n_slots + g_row
    dispatch_dst = jnp.where(in_use, row, spill_row).astype(jnp.int32)
    combine_src = jnp.where(in_use, row, zero_row).astype(jnp.int32)
    tile_row = jnp.arange(n_tiles_max, dtype=jnp.int32) * tile
    tile_expert = jnp.minimum(jnp.sum(tile_row[:, None] >= region_end[None, :], axis=-1), N_EXPERTS - 1)
    n_tiles = (region_end[-1] // tile).astype(jnp.int32)[None]
    return dispatch_dst, combine_src, tile_expert.astype(jnp.int32), n_tiles


def _rope_tables(length):
    pos = jnp.arange(length, dtype=F32)
    inv_freq = ROPE_THETA ** (-jnp.arange(0, HEAD_DIM, 2, dtype=F32) / HEAD_DIM)
    ang = pos[:, None] * inv_freq[None, :]
    cos, sin = jnp.cos(ang), jnp.sin(ang)
    zero = jnp.zeros_like(sin)
    cos_t = jnp.tile(cos, (1, 4))
    sina_t = jnp.tile(jnp.concatenate([-sin, zero], axis=1), (1, 2))
    sinb_t = jnp.tile(jnp.concatenate([zero, sin], axis=1), (1, 2))
    return cos_t, sina_t, sinb_t


def _block(total, want):
    blk = min(total, want)
    assert total % blk == 0, (total, blk)
    return blk


def kernel(x, meta, norm_mix, w_in, lambda_q1, lambda_k1, lambda_q2, lambda_k2, diff_norm, conv_w, a_log, dt_bias,
           gdn_norm, w_out, norm_ffn, w_group, b_group, w_router, b_router, w_gate, w_up, w_down, norm_final):
    b, s, d = x.shape
    assert d == D_MODEL and meta.shape == (N_META, D_MODEL) and s % CHUNK == 0
    assert norm_mix.shape[0] == 1, "single-layer block"
    l = 0

    w = w_in[l]
    w_a = w[:, :3 * MIX_HALF].astype(BF16)
    w_b = w[:, 3 * MIX_HALF:6 * MIX_HALF].astype(BF16)
    w_z = w[:, 6 * MIX_HALF:7 * MIX_HALF].astype(BF16)
    c_ab = 7 * MIX_HALF
    lane_pad = lambda a: jnp.pad(a, [(0, 0)] * (a.ndim - 1) + [(0, SLAB - a.shape[-1])])
    w_g = lane_pad(w[:, c_ab:c_ab + 2 * GDN_HEADS]).astype(BF16)
    alog_rep = lane_pad(a_log[l])[None]
    dtb_rep = lane_pad(dt_bias[l])[None]
    gn_rep = jnp.tile(gdn_norm[l], GDN_HEADS)[None]
    g512 = (jnp.arange(MIX_HALF)[:, None] // HEAD_DIM == jnp.arange(MIX_HALF)[None, :] // HEAD_DIM).astype(BF16)
    gain_mix = norm_mix[l][None]
    w_o = w_out[l].astype(BF16)
    route_pad = lambda g, e: jnp.concatenate(
        [g, jnp.zeros(g.shape[:-1] + (EXPERT_ROW0 - N_GROUPS,), F32), e,
         jnp.zeros(g.shape[:-1] + (ROUTE_LANES - EXPERT_ROW0 - N_EXPERTS,), F32)], axis=-1)
    w_route_f = route_pad(w_group[l], w_router[l])
    w_route_hi = w_route_f.astype(BF16)
    w_route = jnp.concatenate([w_route_hi, (w_route_f - w_route_hi.astype(F32)).astype(BF16)], axis=1)
    b_route = route_pad(b_group[l], b_router[l])[None]
    lam_vecs = [v[l][None] for v in (lambda_q1, lambda_k1, lambda_q2, lambda_k2)]
    cos_t, sina_t, sinb_t = _rope_tables(N_META + s)

    meta3 = meta[None]
    _, ka_m, va_m = _proj_attn(meta3, gain_mix, w_a, cos_t[:N_META], sina_t[:N_META], sinb_t[:N_META], N_META)
    zero_halo = jnp.zeros((1, 1, N_META, d), F32)
    gdn_m = _proj_gdn(meta3, zero_halo, gain_mix, w_b, w_z, w_g, conv_w[l], alog_rep, dtb_rep, g512, N_META)
    pad_m = lambda a: jnp.pad(a, ((0, 0), (0, CHUNK - N_META), (0, 0)))
    s_zero = jnp.zeros((GDN_HEADS // 2, SLAB, SLAB), F32)
    _, s_meta = _gated_delta(*[pad_m(a) for a in gdn_m], gn_rep, g512, s_zero, CHUNK)

    tb_a = _block(s, 1024)
    qa, ka, va = _proj_attn(x, gain_mix, w_a, cos_t[N_META:], sina_t[N_META:], sinb_t[N_META:], tb_a)
    tb_g = _block(s, 512)
    nb_g = s // tb_g
    tails = x.reshape(b, nb_g, tb_g, d)[:, :-1, tb_g - N_META:, :]
    halo = jnp.concatenate([jnp.broadcast_to(meta[None, None], (b, 1, N_META, d)), tails], axis=1)
    gdn_f = _proj_gdn(x, halo, gain_mix, w_b, w_z, w_g, conv_w[l], alog_rep, dtb_rep, g512, tb_g)

    oa = _diff_attention(qa, ka, va, ka_m, va_m, *lam_vecs, diff_norm[l][None], _block(s, 1024), 512)
    ob, _ = _gated_delta(*gdn_f, gn_rep, g512, s_meta[0], _block(s, 256))

    tb_r = _block(s, MOE_BLOCK)
    tok = jnp.arange(tb_r)
    before = (tok[:, None] < tok[None, :]).astype(BF16)
    h1, t, rows, cols, cnt = _out_router(x, oa, ob, w_o, norm_ffn[l][None], w_route, b_route, before, tb_r)

    n = b * s
    nblk = n // tb_r
    n_slots = 2 * tb_r + N_EXPERTS * SLOT_GRAN
    rows_max = 2 * n + (SLOT_GRAN - 1) * N_EXPERTS * nblk + (MOE_TILE - 1) * N_EXPERTS
    n_tiles_max = -(-rows_max // MOE_TILE)
    dispatch_dst, combine_src, tile_expert, n_tiles = _regroup_plan(cnt[:, :, 0], n_slots, MOE_TILE, n_tiles_max)
    buf_rows = n_tiles_max * MOE_TILE + SLOT_GRAN + 2 * n_slots
    xg = _dispatch(dispatch_dst, t.reshape(n, d), rows, jnp.zeros((buf_rows, d), BF16), tb_r, n_slots)
    yg = _experts(tile_expert, n_tiles, xg, w_gate[l].astype(BF16), w_up[l].astype(BF16), w_down[l].astype(BF16),
                  MOE_TILE)
    out = _combine(combine_src, yg, cols.reshape(n, SLAB), h1.reshape(n, d), norm_final[None], tb_r, n_slots)
    return out.reshape(b, s, d)
```

```python
import functools
import math

import jax
import jax.numpy as jnp
from jax import lax
from jax.experimental import pallas as pl
from jax.experimental.pallas import tpu as pltpu

F32 = jnp.float32
BF16 = jnp.bfloat16
HIGHEST = lax.Precision.HIGHEST

D_MODEL = 1024
N_META = 16
CHUNK = 64
EPS = 1e-6
ROPE_THETA = 10000.0
HEAD_DIM = 64
SLAB = 128
DA_HEADS = 4
GDN_HEADS = 8
MIX_HALF = 512
N_GROUPS = 4
EXPERTS_PER_GROUP = 8
N_EXPERTS = 32
D_EXPERT = 256
CONV_K = 4
LAM_INIT = 0.8 - 0.6 * math.exp(-0.3 * 0)
ROUTE_LANES = 128
EXPERT_ROW0 = 8
SLOT_GRAN = 16
MOE_BLOCK = 512
MOE_TILE = 512
ROW_STRIP = 64
NEG_BIG = -1e30
LOG2E = math.log2(math.e)
VMEM_LIMIT = 56 * 1024 * 1024


def _dot(a, b, precision=None):
    return jnp.dot(a, b, preferred_element_type=F32, precision=precision)


def _dot_nt(a, b):
    return lax.dot_general(a, b, (((1,), (1,)), ((), ())), preferred_element_type=F32)


def _dot_tn(a, b):
    return lax.dot_general(a, b, (((0,), (0,)), ((), ())), preferred_element_type=F32)


def _rms_scale(x):
    return x * lax.rsqrt(jnp.mean(x * x, axis=-1, keepdims=True) + EPS)


def _silu(x):
    return x * jax.nn.sigmoid(x)


def _group_sumsq(x, g_ref):
    return _dot((x * x).astype(BF16), g_ref[...])


def _proj_attn_kernel(x_ref, gain_ref, w_ref, cos_ref, sina_ref, sinb_ref, qa_ref, ka_ref, va_ref):
    u = (_rms_scale(x_ref[0]) * gain_ref[...]).astype(BF16)
    proj = _dot(u, w_ref[...])
    va_ref[0] = proj[:, 2 * MIX_HALF:].astype(BF16)
    cos, sina, sinb = cos_ref[...], sina_ref[...], sinb_ref[...]
    for s in range(2 * DA_HEADS):
        xs = proj[:, SLAB * s:SLAB * (s + 1)]
        r = xs * cos + pltpu.roll(xs, SLAB - 32, 1) * sina + pltpu.roll(xs, 32, 1) * sinb
        if s < DA_HEADS:
            qa_ref[0, :, SLAB * s:SLAB * (s + 1)] = (r * (HEAD_DIM ** -0.5 * LOG2E)).astype(BF16)
        else:
            t = s - DA_HEADS
            ka_ref[0, :, SLAB * t:SLAB * (t + 1)] = r.astype(BF16)


def _proj_attn(x, gain, w_a, cos, sina, sinb, tb):
    b, s, d = x.shape
    nb = s // tb
    slab_out = jax.ShapeDtypeStruct((b, s, MIX_HALF), BF16)
    full = lambda shape: pl.BlockSpec(shape, lambda i, j: (0,) * len(shape))
    tok = pl.BlockSpec((1, tb, MIX_HALF), lambda i, j: (i, j, 0))
    tab = pl.BlockSpec((tb, SLAB), lambda i, j: (j, 0))
    return pl.pallas_call(
        _proj_attn_kernel,
        grid=(b, nb),
        in_specs=[pl.BlockSpec((1, tb, d), lambda i, j: (i, j, 0)), full((1, d)), full(w_a.shape), tab, tab, tab],
        out_specs=[tok, tok, tok],
        out_shape=[slab_out, slab_out, slab_out],
        compiler_params=pltpu.CompilerParams(dimension_semantics=("parallel", "parallel"),
                                             vmem_limit_bytes=VMEM_LIMIT),
        name="proj_attn",
    )(x, gain, w_a, cos, sina, sinb)


def _proj_gdn_kernel(x_ref, halo_ref, gain_ref, wb_ref, wz_ref, wg_ref, convw_ref, alog_ref, dtb_ref, g_ref,
                     q_ref, k_ref, v_ref, z_ref, gb_ref, beta_ref, pb_scr):
    halo_rows = halo_ref.shape[2]
    xe = jnp.concatenate([halo_ref[0, 0], x_ref[0]], axis=0)
    ue = (_rms_scale(xe) * gain_ref[...]).astype(BF16)
    u = ue[halo_rows:]
    cw = convw_ref[...]
    outs = (q_ref, k_ref, v_ref)
    for part in range(3):
        cols = slice(MIX_HALF * part, MIX_HALF * (part + 1))
        pb_scr[...] = _dot(ue, wb_ref[:, cols])
        c = cw[:, cols]
        tb = pb_scr.shape[0] - halo_rows
        y = pb_scr[halo_rows:, :] * c[CONV_K - 1:CONV_K]
        for back in range(1, CONV_K):
            tap = CONV_K - 1 - back
            y = y + pb_scr[pl.ds(halo_rows - back, tb), :] * c[tap:tap + 1]
        y = _silu(y)
        if part < 2:
            y = y * lax.rsqrt(_group_sumsq(y, g_ref) + EPS)
            if part == 0:
                y = y * (HEAD_DIM ** -0.5)
        outs[part][0] = y.astype(BF16)
    z_ref[0] = _silu(_dot(u, wz_ref[...])).astype(BF16)
    gates = _dot(u, wg_ref[...])
    ab = gates + dtb_ref[...]
    softplus = jnp.maximum(ab, 0.0) + jnp.log1p(jnp.exp(-jnp.abs(ab)))
    g = -jnp.exp(alog_ref[...]) * softplus
    beta = jax.nn.sigmoid(gates)
    tb = g.shape[0]
    first_head = lax.broadcasted_iota(jnp.int32, (tb, SLAB), 1) < HEAD_DIM
    spread = lambda v, h: jnp.broadcast_to(v[:, h:h + 1], (tb, SLAB))
    for p in range(GDN_HEADS // 2):
        cols = slice(SLAB * p, SLAB * (p + 1))
        gb_ref[0, :, cols] = jnp.where(first_head, spread(g, 2 * p), spread(g, 2 * p + 1))
        beta_ref[0, :, cols] = jnp.where(first_head, spread(beta, GDN_HEADS + 2 * p),
                                         spread(beta, GDN_HEADS + 2 * p + 1)).astype(BF16)


def _proj_gdn(x, halo, gain, w_b, w_z, w_g, conv_w, alog_rep, dtb_rep, g512, tb):
    b, s, d = x.shape
    nb = s // tb
    full = lambda shape: pl.BlockSpec(shape, lambda i, j: (0,) * len(shape))
    tok = pl.BlockSpec((1, tb, MIX_HALF), lambda i, j: (i, j, 0))
    bf = jax.ShapeDtypeStruct((b, s, MIX_HALF), BF16)
    return pl.pallas_call(
        _proj_gdn_kernel,
        grid=(b, nb),
        in_specs=[pl.BlockSpec((1, tb, d), lambda i, j: (i, j, 0)),
                  pl.BlockSpec((1, 1) + halo.shape[2:], lambda i, j: (i, j, 0, 0)),
                  full((1, d)), full(w_b.shape), full(w_z.shape), full(w_g.shape), full(conv_w.shape),
                  full((1, SLAB)), full((1, SLAB)), full(g512.shape)],
        out_specs=[tok] * 6,
        out_shape=[bf, bf, bf, bf, jax.ShapeDtypeStruct((b, s, MIX_HALF), F32), bf],
        scratch_shapes=[pltpu.VMEM((halo.shape[2] + tb, MIX_HALF), F32)],
        compiler_params=pltpu.CompilerParams(dimension_semantics=("parallel", "parallel"),
                                             vmem_limit_bytes=VMEM_LIMIT),
        name="proj_gdn",
    )(x, halo, gain, w_b, w_z, w_g, conv_w, alog_rep, dtb_rep, g512)


def _attn_kernel(q_ref, k_ref, v_ref, km_ref, vm_ref, lq1_ref, lk1_ref, lq2_ref, lk2_ref, dn_ref, o_ref,
                 qq_scr, m_scr, acc_scr, st_scr, *, qb, cw):
    i = pl.program_id(2)
    n_col = 2 * qb // cw
    per_map = qb // cw
    q = q_ref[0]
    lane = lax.broadcasted_iota(jnp.int32, (qb, SLAB), 1)
    zero = jnp.zeros_like(q)
    qq_scr[...] = jnp.concatenate([jnp.where(lane < HEAD_DIM, q, zero), jnp.where(lane < HEAD_DIM, zero, q)], axis=0)

    def with_ones(v_blk):
        return jnp.concatenate([v_blk, jnp.ones_like(v_blk)], axis=1)

    v1_meta = with_ones(vm_ref[0])
    for c in range(n_col):
        rs = slice(cw * c, cw * (c + 1))
        s = _dot_nt(qq_scr[rs, :], km_ref[0])
        m_new = jnp.max(s, axis=1, keepdims=True)
        acc_scr[rs, :] = _dot(jnp.exp2((s - m_new).astype(BF16)), v1_meta)
        m_scr[rs, :] = jnp.broadcast_to(m_new, (cw, SLAB))

    def scores_into(slot, start, c):
        st_scr[slot] = _dot_nt(qq_scr[cw * c:cw * (c + 1), :], k_ref[0, pl.ds(start, cw), :])

    def softmax_pv(slot, v1, c, masked):
        alphas = []
        for t in range(cw // ROW_STRIP):
            ls = slice(ROW_STRIP * t, ROW_STRIP * (t + 1))
            gs = slice(cw * c + ROW_STRIP * t, cw * c + ROW_STRIP * (t + 1))
            s = st_scr[slot, ls, :]
            if masked:
                q_chunk = (lax.broadcasted_iota(jnp.int32, s.shape, 0) + ROW_STRIP * t) // CHUNK
                k_chunk = lax.broadcasted_iota(jnp.int32, s.shape, 1) // CHUNK
                s = jnp.where(k_chunk <= q_chunk, s, NEG_BIG)
            parts = [s[:, SLAB * k:SLAB * (k + 1)] for k in range(cw // SLAB)]
            lane_max = functools.reduce(jnp.maximum, parts)
            m_prev = m_scr[gs, :]
            m_new = jnp.maximum(m_prev, jnp.max(lane_max, axis=1, keepdims=True))
            for k, part in enumerate(parts):
                st_scr[slot, ls, SLAB * k:SLAB * (k + 1)] = jnp.exp2(part - m_new)
            alphas.append(jnp.exp2(m_prev - m_new))
            m_scr[gs, :] = m_new
        rs = slice(cw * c, cw * (c + 1))
        alpha = jnp.concatenate(alphas, axis=0)
        pv = _dot(st_scr[slot].astype(BF16), v1)
        acc_scr[rs, :] = acc_scr[rs, :] * jnp.concatenate([alpha, alpha], axis=1) + pv

    parity = [0]
    scores_into(0, 0, 0)

    def key_block(start, groups, masked_groups, following):
        v1 = with_ones(v_ref[0, pl.ds(start, cw), :])
        for idx, c in enumerate(groups):
            slot = parity[0]
            if idx + 1 < len(groups):
                scores_into(1 - slot, start, groups[idx + 1])
            elif following is not None:
                scores_into(1 - slot, *following)
            softmax_pv(slot, v1, c, c in masked_groups)
            parity[0] = 1 - slot

    all_groups = list(range(n_col))

    def full_blocks(j, carry):
        for d in range(per_map):
            start = pl.multiple_of((per_map * j + d) * cw, cw)
            key_block(start, all_groups, (), (start + cw, 0))
        return carry

    lax.fori_loop(0, i, full_blocks, 0)
    for d in range(per_map):
        start = pl.multiple_of((per_map * i + d) * cw, cw)
        groups = [c for c in all_groups if c % per_map >= d]
        following = (start + cw, d + 1) if d + 1 < per_map else None
        key_block(start, groups, [c for c in groups if c % per_map == d], following)

    acc = acc_scr[...]
    o1 = acc[:qb, :SLAB] / acc[:qb, SLAB:]
    o2 = acc[qb:, :SLAB] / acc[qb:, SLAB:]
    lam = (jnp.exp(jnp.sum(lq1_ref[...] * lk1_ref[...], axis=1, keepdims=True))
           - jnp.exp(jnp.sum(lq2_ref[...] * lk2_ref[...], axis=1, keepdims=True)) + LAM_INIT)
    o = o1 - lam * o2
    o_ref[0] = (_rms_scale(o) * dn_ref[...] * (1.0 - LAM_INIT)).astype(BF16)


def _diff_attention(qa, ka, va, ka_meta, va_meta, lq1, lk1, lq2, lk2, diff_norm, qb, cw):
    b, s, _ = qa.shape
    nq = s // qb
    vec = lambda n: pl.BlockSpec((1, n), lambda bi, h, i: (0, 0))
    seq = pl.BlockSpec((1, s, SLAB), lambda bi, h, i: (bi, 0, h))
    meta = pl.BlockSpec((1, N_META, SLAB), lambda bi, h, i: (0, 0, h))
    return pl.pallas_call(
        functools.partial(_attn_kernel, qb=qb, cw=cw),
        grid=(b, DA_HEADS, nq),
        in_specs=[pl.BlockSpec((1, qb, SLAB), lambda bi, h, i: (bi, i, h)), seq, seq, meta, meta,
                  vec(HEAD_DIM), vec(HEAD_DIM), vec(HEAD_DIM), vec(HEAD_DIM), vec(SLAB)],
        out_specs=pl.BlockSpec((1, qb, SLAB), lambda bi, h, i: (bi, i, h)),
        out_shape=jax.ShapeDtypeStruct((b, s, MIX_HALF), BF16),
        scratch_shapes=[pltpu.VMEM((2 * qb, SLAB), BF16), pltpu.VMEM((2 * qb, SLAB), F32),
                        pltpu.VMEM((2 * qb, 2 * SLAB), F32), pltpu.VMEM((2, cw, cw), F32)],
        compiler_params=pltpu.CompilerParams(dimension_semantics=("parallel", "parallel", "arbitrary"),
                                             vmem_limit_bytes=VMEM_LIMIT),
        name="diff_attention",
    )(qa, ka, va, ka_meta, va_meta, lq1, lk1, lq2, lk2, diff_norm)


def _gdn_kernel(q_ref, k_ref, v_ref, z_ref, g_ref, beta_ref, gn_ref, g512_ref, s0_ref, o_ref, sfin_ref,
                s_scr, o_scr, *, n_chunks):
    j = pl.program_id(0)
    n_pairs = GDN_HEADS // 2
    n_seq = q_ref.shape[0]

    @pl.when(j == 0)
    def _():
        for bi in range(n_seq):
            s_scr[bi] = s0_ref[...]

    row = lax.broadcasted_iota(jnp.int32, (CHUNK, SLAB), 0)
    lane = lax.broadcasted_iota(jnp.int32, (CHUNK, SLAB), 1)
    col = lane % CHUNK
    head_a = lane < HEAD_DIM
    tri_incl = row >= col
    tri_strict = row > col
    eye = (row == col).astype(F32)
    ltri = (lax.broadcasted_iota(jnp.int32, (CHUNK, CHUNK), 0)
            >= lax.broadcasted_iota(jnp.int32, (CHUNK, CHUNK), 1)).astype(F32)
    bd_mask = ((lax.broadcasted_iota(jnp.int32, (SLAB, SLAB), 0) < HEAD_DIM)
               == (lax.broadcasted_iota(jnp.int32, (SLAB, SLAB), 1) < HEAD_DIM))

    def block_diag(x):
        xb = x.astype(BF16)
        zero = jnp.zeros_like(xb)
        return jnp.concatenate([jnp.where(head_a, xb, zero), jnp.where(head_a, zero, xb)], axis=0)

    def pair_mm(x, y):
        return _dot(x.astype(BF16), block_diag(y))

    items = [(bi, c, p) for bi in range(n_seq) for c in range(n_chunks) for p in range(n_pairs)]
    rows_of = lambda c: slice(CHUNK * c, CHUNK * (c + 1))
    cols_of = lambda p: slice(SLAB * p, SLAB * (p + 1))
    load = lambda ref, it: ref[it[0], rows_of(it[1]), cols_of(it[2])]

    g_col, decay, k_f, k_beta, k_bd, q_f, vb = {}, {}, {}, {}, {}, {}, {}
    for it in items:
        g = load(g_ref, it)
        g_col[it] = _dot(ltri, g, precision=HIGHEST)
        g_row = jnp.sum(jnp.where(row <= col, g, 0.0), axis=0, keepdims=True)
        decay[it] = jnp.where(tri_incl, jnp.exp(jnp.where(tri_incl, g_col[it] - g_row, 0.0)), 0.0)
        beta = load(beta_ref, it).astype(F32)
        k_f[it] = load(k_ref, it).astype(F32)
        q_f[it] = load(q_ref, it).astype(F32)
        k_beta[it] = k_f[it] * beta
        vb[it] = load(v_ref, it).astype(F32) * beta
        k_bd[it] = block_diag(k_f[it])

    a, a_qk = {}, {}
    for it in items:
        lhs = jnp.concatenate([k_beta[it].astype(BF16), q_f[it].astype(BF16)], axis=0)
        kk = _dot_nt(lhs, k_bd[it])
        a[it] = jnp.where(tri_strict, kk[:CHUNK] * decay[it], 0.0)
        a_qk[it] = jnp.where(tri_incl, kk[CHUNK:] * decay[it], 0.0)

    t = {it: eye - a[it] for it in items}
    pw = dict(a)
    for _ in range(5):
        for it in items:
            pw[it] = pair_mm(pw[it], pw[it])
        for it in items:
            t[it] = t[it] + pair_mm(t[it], pw[it])

    u, w, k_g, q_g, e_last = {}, {}, {}, {}, {}
    for it in items:
        e_col = jnp.exp(g_col[it])
        rhs = jnp.concatenate([block_diag(vb[it]), block_diag(k_beta[it] * e_col)], axis=1)
        uw = _dot(t[it].astype(BF16), rhs)
        u[it], w[it] = uw[:, :SLAB], uw[:, SLAB:]
        g_last = g_col[it][CHUNK - 1:CHUNK, :]
        k_g[it] = (k_f[it] * jnp.exp(g_last - g_col[it])).astype(BF16)
        q_g[it] = q_f[it] * e_col
        e_last[it] = jnp.exp(g_last)

    state = {(bi, p): s_scr[bi, p] for bi in range(n_seq) for p in range(n_pairs)}
    sp = lambda it: (it[0], it[2])
    for c in range(n_chunks):
        its = [(bi, c, p) for bi in range(n_seq) for p in range(n_pairs)]
        ws_qs = [_dot(jnp.concatenate([w[it], q_g[it]], axis=0).astype(BF16), state[sp(it)].astype(BF16))
                 for it in its]
        v_new = [u[it] - sq[:CHUNK] for it, sq in zip(its, ws_qs)]
        for it, sq, vn in zip(its, ws_qs, v_new):
            o_scr[it[0], rows_of(c), cols_of(it[2])] = sq[CHUNK:] + pair_mm(a_qk[it], vn)
        for it, vn in zip(its, v_new):
            upd = _dot_tn(k_g[it], vn.astype(BF16))
            state[sp(it)] = state[sp(it)] * e_last[it] + jnp.where(bd_mask, upd, 0.0)
    for (bi, p), val in state.items():
        s_scr[bi, p] = val

    for bi in range(n_seq):
        o = o_scr[bi]
        ss = _group_sumsq(o, g512_ref)
        y = o * lax.rsqrt(ss * (1.0 / HEAD_DIM) + EPS) * gn_ref[...]
        o_ref[bi] = (y * z_ref[bi].astype(F32)).astype(BF16)

    @pl.when(j == pl.num_programs(0) - 1)
    def _():
        sfin_ref[...] = s_scr[...]


def _gated_delta(q, k, v, z, gb, beta, gn_rep, g512, s0, tbg):
    b, s, _ = q.shape
    nb = s // tbg
    n_pairs = GDN_HEADS // 2
    tok = pl.BlockSpec((b, tbg, MIX_HALF), lambda j: (0, j, 0))
    full = lambda shape: pl.BlockSpec(shape, lambda j: (0,) * len(shape))
    return pl.pallas_call(
        functools.partial(_gdn_kernel, n_chunks=tbg // CHUNK),
        grid=(nb,),
        in_specs=[tok] * 6 + [full((1, MIX_HALF)), full(g512.shape), full(s0.shape)],
        out_specs=[tok, pl.BlockSpec((b, n_pairs, SLAB, SLAB), lambda j: (0, 0, 0, 0))],
        out_shape=[jax.ShapeDtypeStruct((b, s, MIX_HALF), BF16),
                   jax.ShapeDtypeStruct((b, n_pairs, SLAB, SLAB), F32)],
        scratch_shapes=[pltpu.VMEM((b, n_pairs, SLAB, SLAB), F32), pltpu.VMEM((b, tbg, MIX_HALF), F32)],
        compiler_params=pltpu.CompilerParams(dimension_semantics=("arbitrary",),
                                             vmem_limit_bytes=VMEM_LIMIT),
        name="gated_delta",
    )(q, k, v, z, gb, beta, gn_rep, g512, s0)


def _out_router_kernel(x_ref, oa_ref, ob_ref, wo_ref, gain_ref, wr_ref, br_ref, before_ref,
                       h_ref, t_ref, rows_ref, cols_ref, cnt_ref):
    h = x_ref[0] + _dot(jnp.concatenate([oa_ref[0], ob_ref[0]], axis=1), wo_ref[...])
    h_ref[0] = h
    t = _rms_scale(h) * gain_ref[...]
    t_hi = t.astype(BF16)
    t_ref[0] = t_hi
    t_lo = (t - t_hi.astype(F32)).astype(BF16)
    wr = wr_ref[...]
    hi_part = _dot(t_hi, wr)
    logits = hi_part[:, :ROUTE_LANES] + hi_part[:, ROUTE_LANES:] + _dot(t_lo, wr)[:, :ROUTE_LANES] + br_ref[...]

    lt = logits.T
    tb = lt.shape[1]
    row8 = lax.broadcasted_iota(jnp.int32, (EXPERTS_PER_GROUP, tb), 0)
    col_max = lambda v: jnp.max(v, axis=0, keepdims=True)
    col_sum = lambda v: jnp.sum(v, axis=0, keepdims=True)
    first_argmax = lambda v, vmax: jnp.min(jnp.where(v == vmax, row8, EXPERTS_PER_GROUP), axis=0, keepdims=True)

    gl = jnp.where(row8 < N_GROUPS, lt[:EXPERTS_PER_GROUP], NEG_BIG)
    gmax = col_max(gl)
    gsel = first_argmax(gl, gmax)
    psel = 1.0 / col_sum(jnp.exp(gl - gmax))
    el = jnp.zeros((EXPERTS_PER_GROUP, tb), F32)
    for g in range(N_GROUPS):
        lo = EXPERT_ROW0 + EXPERTS_PER_GROUP * g
        el = jnp.where(gsel == g, lt[lo:lo + EXPERTS_PER_GROUP], el)
    m1 = col_max(el)
    i1 = first_argmax(el, m1)
    el2 = jnp.where(row8 == i1, NEG_BIG, el)
    m2 = col_max(el2)
    i2 = first_argmax(el2, m2)
    denom = col_sum(jnp.exp(el - m1))
    p1 = 1.0 / denom
    p2 = jnp.exp(m2 - m1) / denom
    w1 = p1 / (p1 + p2) * psel
    w2 = p2 / (p1 + p2) * psel
    e1 = gsel * EXPERTS_PER_GROUP + i1
    e2 = gsel * EXPERTS_PER_GROUP + i2

    row_e = lax.broadcasted_iota(jnp.int32, (N_EXPERTS, tb), 0)
    hot1 = (row_e == e1).astype(F32)
    hot2 = (row_e == e2).astype(F32)
    both = hot1 + hot2
    n_gran = jnp.ceil(jnp.sum(both, axis=1, keepdims=True) * (1.0 / SLOT_GRAN)) * SLOT_GRAN
    ei = lax.broadcasted_iota(jnp.int32, (N_EXPERTS, N_EXPERTS), 0)
    ej = lax.broadcasted_iota(jnp.int32, (N_EXPERTS, N_EXPERTS), 1)
    seg_start = _dot((ei > ej).astype(F32), jnp.broadcast_to(n_gran, (N_EXPERTS, SLAB)), precision=HIGHEST)[:, :1]
    earlier = _dot(both.astype(BF16), before_ref[...])
    where_to = earlier + seg_start
    pos1 = col_sum(hot1 * where_to)
    pos2 = col_sum(hot2 * where_to)

    info = jnp.concatenate([pos1, pos2, w1, w2, e1.astype(F32), e2.astype(F32),
                            jnp.zeros((SLAB - 6, tb), F32)], axis=0)
    rows_ref[0] = info[:8]
    cols_ref[0] = info.T
    cnt_ref[0] = jnp.broadcast_to(n_gran, (N_EXPERTS, SLAB))


def _out_router(x, oa, ob, w_o, gain, w_route, b_route, before, tb):
    b, s, d = x.shape
    nb = s // tb
    full = lambda shape: pl.BlockSpec(shape, lambda i, j: (0,) * len(shape))
    tokd = pl.BlockSpec((1, tb, d), lambda i, j: (i, j, 0))
    tokh = pl.BlockSpec((1, tb, MIX_HALF), lambda i, j: (i, j, 0))
    return pl.pallas_call(
        _out_router_kernel,
        grid=(b, nb),
        in_specs=[tokd, tokh, tokh, full(w_o.shape), full((1, d)), full(w_route.shape), full((1, ROUTE_LANES)),
                  full(before.shape)],
        out_specs=[tokd, tokd, pl.BlockSpec((1, 8, tb), lambda i, j: (i, 0, j)),
                   pl.BlockSpec((1, tb, SLAB), lambda i, j: (i, j, 0)),
                   pl.BlockSpec((1, N_EXPERTS, SLAB), lambda i, j: (i * nb + j, 0, 0))],
        out_shape=[jax.ShapeDtypeStruct((b, s, d), F32), jax.ShapeDtypeStruct((b, s, d), BF16),
                   jax.ShapeDtypeStruct((b, 8, s), F32), jax.ShapeDtypeStruct((b, s, SLAB), F32),
                   jax.ShapeDtypeStruct((b * nb, N_EXPERTS, SLAB), F32)],
        compiler_params=pltpu.CompilerParams(dimension_semantics=("parallel", "parallel"),
                                             vmem_limit_bytes=VMEM_LIMIT),
        name="out_router",
    )(x, oa, ob, w_o, gain, w_route, b_route, before)


def _granule_copies(table_ref, block, buf, local_ref, global_ref, sem, to_global, n_gran):
    copies = []
    for g in range(n_gran):
        loc = local_ref.at[buf, pl.ds(g * SLOT_GRAN, SLOT_GRAN)]
        glob = global_ref.at[pl.ds(pl.multiple_of(table_ref[block, g], SLOT_GRAN), SLOT_GRAN)]
        copies.append(pltpu.make_async_copy(loc, glob, sem.at[buf]) if to_global
                      else pltpu.make_async_copy(glob, loc, sem.at[buf]))
    return copies


def _dispatch_kernel(gdst_ref, t_ref, rows_ref, xg_init_ref, xg_ref, xs_scr, sem, *, n_slots):
    del xg_init_ref
    blk = pl.program_id(0)
    last = pl.num_programs(0) - 1
    buf = blk % 2
    tb = t_ref.shape[0]
    n_gran = n_slots // SLOT_GRAN
    info = rows_ref[0]
    slot = lax.broadcasted_iota(jnp.int32, (n_slots, tb), 0).astype(F32)
    onehot = jnp.where((slot == info[0:1]) | (slot == info[1:2]), 1.0, 0.0).astype(BF16)
    xs_scr[buf] = _dot(onehot, t_ref[...]).astype(BF16)
    for copy in _granule_copies(gdst_ref, blk, buf, xs_scr, xg_ref, sem, True, n_gran):
        copy.start()

    @pl.when(blk > 0)
    def _():
        for copy in _granule_copies(gdst_ref, blk - 1, 1 - buf, xs_scr, xg_ref, sem, True, n_gran):
            copy.wait()

    @pl.when(blk == last)
    def _():
        for copy in _granule_copies(gdst_ref, blk, buf, xs_scr, xg_ref, sem, True, n_gran):
            copy.wait()


def _dispatch(granule_dst, t, rows, xg_init, tb, n_slots):
    n, d = t.shape
    nblk = n // tb
    nb = rows.shape[2] // tb
    grid_spec = pltpu.PrefetchScalarGridSpec(
        num_scalar_prefetch=1,
        grid=(nblk,),
        in_specs=[pl.BlockSpec((tb, d), lambda i, gd: (i, 0)),
                  pl.BlockSpec((1, 8, tb), lambda i, gd: (i // nb, 0, i % nb)),
                  pl.BlockSpec(memory_space=pl.ANY)],
        out_specs=pl.BlockSpec(memory_space=pl.ANY),
        scratch_shapes=[pltpu.VMEM((2, n_slots, d), BF16), pltpu.SemaphoreType.DMA((2,))],
    )
    return pl.pallas_call(
        functools.partial(_dispatch_kernel, n_slots=n_slots),
        grid_spec=grid_spec,
        out_shape=jax.ShapeDtypeStruct(xg_init.shape, BF16),
        input_output_aliases={3: 0},
        compiler_params=pltpu.CompilerParams(dimension_semantics=("arbitrary",), vmem_limit_bytes=VMEM_LIMIT),
        name="moe_dispatch",
    )(granule_dst, t, rows, xg_init)


def _experts_kernel(tile_expert_ref, n_tiles_ref, x_ref, wg_ref, wu_ref, wd_ref, y_ref):
    del tile_expert_ref

    @pl.when(pl.program_id(0) < n_tiles_ref[0])
    def _():
        x = x_ref[...]
        act = _silu(_dot(x, wg_ref[0])) * _dot(x, wu_ref[0])
        y_ref[...] = _dot(act.astype(BF16), wd_ref[0]).astype(BF16)


def _experts(tile_expert, n_tiles, xg, w_gate, w_up, w_down, tile):
    rows, d = xg.shape
    used = lambda i, te, nt: jnp.minimum(i, nt[0] - 1)
    wspec = lambda shape: pl.BlockSpec((1,) + shape, lambda i, te, nt: (te[used(i, te, nt)], 0, 0))
    grid_spec = pltpu.PrefetchScalarGridSpec(
        num_scalar_prefetch=2,
        grid=(tile_expert.shape[0],),
        in_specs=[pl.BlockSpec((tile, d), lambda i, te, nt: (used(i, te, nt), 0)),
                  wspec((d, D_EXPERT)), wspec((d, D_EXPERT)), wspec((D_EXPERT, d))],
        out_specs=pl.BlockSpec((tile, d), lambda i, te, nt: (used(i, te, nt), 0)),
    )
    return pl.pallas_call(
        _experts_kernel,
        grid_spec=grid_spec,
        out_shape=jax.ShapeDtypeStruct((rows, d), BF16),
        input_output_aliases={2: 0},
        compiler_params=pltpu.CompilerParams(dimension_semantics=("arbitrary",), vmem_limit_bytes=VMEM_LIMIT),
        name="moe_experts",
    )(tile_expert, n_tiles, xg, w_gate, w_up, w_down)


def _combine_kernel(gsrc_ref, yg_ref, cols_ref, h_ref, gain_ref, o_ref, ys_scr, sem, *, n_slots):
    blk = pl.program_id(0)
    last = pl.num_programs(0) - 1
    buf = blk % 2
    tb = h_ref.shape[0]
    n_gran = n_slots // SLOT_GRAN

    @pl.when(blk == 0)
    def _():
        for copy in _granule_copies(gsrc_ref, 0, 0, ys_scr, yg_ref, sem, False, n_gran):
            copy.start()

    @pl.when(blk < last)
    def _():
        for copy in _granule_copies(gsrc_ref, blk + 1, 1 - buf, ys_scr, yg_ref, sem, False, n_gran):
            copy.start()

    info = cols_ref[...]
    slot = lax.broadcasted_iota(jnp.int32, (tb, n_slots), 1).astype(F32)
    weights = (jnp.where(slot == info[:, 0:1], info[:, 2:3], 0.0)
               + jnp.where(slot == info[:, 1:2], info[:, 3:4], 0.0)).astype(BF16)
    for copy in _granule_copies(gsrc_ref, blk, buf, ys_scr, yg_ref, sem, False, n_gran):
        copy.wait()
    y = _dot(weights, ys_scr[buf])
    o_ref[...] = _rms_scale(h_ref[...] + y) * gain_ref[...]


def _combine(granule_dst, yg, cols, h, gain, tb, n_slots):
    n, d = h.shape
    grid_spec = pltpu.PrefetchScalarGridSpec(
        num_scalar_prefetch=1,
        grid=(n // tb,),
        in_specs=[pl.BlockSpec(memory_space=pl.ANY),
                  pl.BlockSpec((tb, SLAB), lambda i, gd: (i, 0)),
                  pl.BlockSpec((tb, d), lambda i, gd: (i, 0)),
                  pl.BlockSpec((1, d), lambda i, gd: (0, 0))],
        out_specs=pl.BlockSpec((tb, d), lambda i, gd: (i, 0)),
        scratch_shapes=[pltpu.VMEM((2, n_slots, d), BF16), pltpu.SemaphoreType.DMA((2,))],
    )
    return pl.pallas_call(
        functools.partial(_combine_kernel, n_slots=n_slots),
        grid_spec=grid_spec,
        out_shape=jax.ShapeDtypeStruct((n, d), F32),
        compiler_params=pltpu.CompilerParams(dimension_semantics=("arbitrary",), vmem_limit_bytes=VMEM_LIMIT),
        name="moe_combine",
    )(granule_dst, yg, cols, h, gain)


def _regroup_plan(n_gran_be, n_slots, tile, n_tiles_max):
    cnt = n_gran_be.astype(jnp.int32)
    region = (jnp.sum(cnt, axis=0) + tile - 1) // tile * tile
    region_end = jnp.cumsum(region)
    first_row = (region_end - region)[None, :] + jnp.cumsum(cnt, axis=0) - cnt
    seg_end = jnp.cumsum(cnt, axis=1)
    seg_start = seg_end - cnt
    g_row = jnp.arange(n_slots // SLOT_GRAN, dtype=jnp.int32) * SLOT_GRAN
    owned = ((g_row[None, :, None] >= seg_start[:, None, :]) & (g_row[None, :, None] < seg_end[:, None, :]))
    in_use = g_row[None, :] < seg_end[:, -1:]
    row = g_row[None, :] + jnp.sum(jnp.where(owned, (first_row - seg_start)[:, None, :], 0), axis=-1)
    zero_row = n_tiles_max * tile
    spill_row = zero_row + SLOT_GRAN + (jnp.arange(cnt.shape[0], dtype=jnp.int32) % 2)[:, None] * n_slots + g_row
    dispatch_dst = jnp.where(in_use, row, spill_row).astype(jnp.int32)
    combine_src = jnp.where(in_use, row, zero_row).astype(jnp.int32)
    tile_row = jnp.arange(n_tiles_max, dtype=jnp.int32) * tile
    tile_expert = jnp.minimum(jnp.sum(tile_row[:, None] >= region_end[None, :], axis=-1), N_EXPERTS - 1)
    n_tiles = (region_end[-1] // tile).astype(jnp.int32)[None]
    return dispatch_dst, combine_src, tile_expert.astype(jnp.int32), n_tiles


def _rope_tables(length):
    pos = jnp.arange(length, dtype=F32)
    inv_freq = ROPE_THETA ** (-jnp.arange(0, HEAD_DIM, 2, dtype=F32) / HEAD_DIM)
    ang = pos[:, None] * inv_freq[None, :]
    cos, sin = jnp.cos(ang), jnp.sin(ang)
    zero = jnp.zeros_like(sin)
    cos_t = jnp.tile(cos, (1, 4))
    sina_t = jnp.tile(jnp.concatenate([-sin, zero], axis=1), (1, 2))
    sinb_t = jnp.tile(jnp.concatenate([zero, sin], axis=1), (1, 2))
    return cos_t, sina_t, sinb_t


def _block(total, want):
    blk = min(total, want)
    assert total % blk == 0, (total, blk)
    return blk


def kernel(x, meta, norm_mix, w_in, lambda_q1, lambda_k1, lambda_q2, lambda_k2, diff_norm, conv_w, a_log, dt_bias,
           gdn_norm, w_out, norm_ffn, w_group, b_group, w_router, b_router, w_gate, w_up, w_down, norm_final):
    b, s, d = x.shape
    assert d == D_MODEL and meta.shape == (N_META, D_MODEL) and s % CHUNK == 0
    assert norm_mix.shape[0] == 1, "single-layer block"
    l = 0

    w = w_in[l]
    w_a = w[:, :3 * MIX_HALF].astype(BF16)
    w_b = w[:, 3 * MIX_HALF:6 * MIX_HALF].astype(BF16)
    w_z = w[:, 6 * MIX_HALF:7 * MIX_HALF].astype(BF16)
    c_ab = 7 * MIX_HALF
    lane_pad = lambda a: jnp.pad(a, [(0, 0)] * (a.ndim - 1) + [(0, SLAB - a.shape[-1])])
    w_g = lane_pad(w[:, c_ab:c_ab + 2 * GDN_HEADS]).astype(BF16)
    alog_rep = lane_pad(a_log[l])[None]
    dtb_rep = lane_pad(dt_bias[l])[None]
    gn_rep = jnp.tile(gdn_norm[l], GDN_HEADS)[None]
    g512 = (jnp.arange(MIX_HALF)[:, None] // HEAD_DIM == jnp.arange(MIX_HALF)[None, :] // HEAD_DIM).astype(BF16)
    gain_mix = norm_mix[l][None]
    w_o = w_out[l].astype(BF16)
    route_pad = lambda g, e: jnp.concatenate(
        [g, jnp.zeros(g.shape[:-1] + (EXPERT_ROW0 - N_GROUPS,), F32), e,
         jnp.zeros(g.shape[:-1] + (ROUTE_LANES - EXPERT_ROW0 - N_EXPERTS,), F32)], axis=-1)
    w_route_f = route_pad(w_group[l], w_router[l])
    w_route_hi = w_route_f.astype(BF16)
    w_route = jnp.concatenate([w_route_hi, (w_route_f - w_route_hi.astype(F32)).astype(BF16)], axis=1)
    b_route = route_pad(b_group[l], b_router[l])[None]
    lam_vecs = [v[l][None] for v in (lambda_q1, lambda_k1, lambda_q2, lambda_k2)]
    cos_t, sina_t, sinb_t = _rope_tables(N_META + s)

    meta3 = meta[None]
    _, ka_m, va_m = _proj_attn(meta3, gain_mix, w_a, cos_t[:N_META], sina_t[:N_META], sinb_t[:N_META], N_META)
    zero_halo = jnp.zeros((1, 1, N_META, d), F32)
    gdn_m = _proj_gdn(meta3, zero_halo, gain_mix, w_b, w_z, w_g, conv_w[l], alog_rep, dtb_rep, g512, N_META)
    pad_m = lambda a: jnp.pad(a, ((0, 0), (0, CHUNK - N_META), (0, 0)))
    s_zero = jnp.zeros((GDN_HEADS // 2, SLAB, SLAB), F32)
    _, s_meta = _gated_delta(*[pad_m(a) for a in gdn_m], gn_rep, g512, s_zero, CHUNK)

    tb_a = _block(s, 1024)
    qa, ka, va = _proj_attn(x, gain_mix, w_a, cos_t[N_META:], sina_t[N_META:], sinb_t[N_META:], tb_a)
    tb_g = _block(s, 512)
    nb_g = s // tb_g
    tails = x.reshape(b, nb_g, tb_g, d)[:, :-1, tb_g - N_META:, :]
    halo = jnp.concatenate([jnp.broadcast_to(meta[None, None], (b, 1, N_META, d)), tails], axis=1)
    gdn_f = _proj_gdn(x, halo, gain_mix, w_b, w_z, w_g, conv_w[l], alog_rep, dtb_rep, g512, tb_g)

    oa = _diff_attention(qa, ka, va, ka_m, va_m, *lam_vecs, diff_norm[l][None], _block(s, 2048), 512)
    ob, _ = _gated_delta(*gdn_f, gn_rep, g512, s_meta[0], _block(s, 256))

    tb_r = _block(s, MOE_BLOCK)
    tok = jnp.arange(tb_r)
    before = (tok[:, None] < tok[None, :]).astype(BF16)
    h1, t, rows, cols, cnt = _out_router(x, oa, ob, w_o, norm_ffn[l][None], w_route, b_route, before, tb_r)

    n = b * s
    nblk = n // tb_r
    n_slots = 2 * tb_r + N_EXPERTS * SLOT_GRAN
    rows_max = 2 * n + (SLOT_GRAN - 1) * N_EXPERTS * nblk + (MOE_TILE - 1) * N_EXPERTS
    n_tiles_max = -(-rows_max // MOE_TILE)
    dispatch_dst, combine_src, tile_expert, n_tiles = _regroup_plan(cnt[:, :, 0], n_slots, MOE_TILE, n_tiles_max)
    buf_rows = n_tiles_max * MOE_TILE + SLOT_GRAN + 2 * n_slots
    xg = _dispatch(dispatch_dst, t.reshape(n, d), rows, jnp.zeros((buf_rows, d), BF16), tb_r, n_slots)
    yg = _experts(tile_expert, n_tiles, xg, w_gate[l].astype(BF16), w_up[l].astype(BF16), w_down[l].astype(BF16),
                  MOE_TILE)
    out = _combine(combine_src, yg, cols.reshape(n, SLAB), h1.reshape(n, d), norm_final[None], tb_r, n_slots)
    return out.reshape(b, s, d)
```

```python
import functools
import math

import jax
import jax.numpy as jnp
from jax import lax
from jax.experimental import pallas as pl
from jax.experimental.pallas import tpu as pltpu

F32 = jnp.float32
BF16 = jnp.bfloat16
HIGHEST = lax.Precision.HIGHEST

D_MODEL = 1024
N_META = 16
CHUNK = 64
EPS = 1e-6
ROPE_THETA = 10000.0
HEAD_DIM = 64
SLAB = 128
DA_HEADS = 4
GDN_HEADS = 8
MIX_HALF = 512
GDN_GROUP = 2
GDN_GW = GDN_GROUP * HEAD_DIM
N_GROUPS = 4
EXPERTS_PER_GROUP = 8
N_EXPERTS = 32
D_EXPERT = 256
CONV_K = 4
LAM_INIT = 0.8 - 0.6 * math.exp(-0.3 * 0)
ROUTE_LANES = 128
EXPERT_ROW0 = 8
SLOT_GRAN = 16
MOE_BLOCK = 512
MOE_TILE = 512
ROW_STRIP = 64
NEG_BIG = -1e30
LOG2E = math.log2(math.e)
VMEM_LIMIT = 56 * 1024 * 1024


def _dot(a, b, precision=None):
    return jnp.dot(a, b, preferred_element_type=F32, precision=precision)


def _dot_nt(a, b):
    return lax.dot_general(a, b, (((1,), (1,)), ((), ())), preferred_element_type=F32)


def _dot_tn(a, b):
    return lax.dot_general(a, b, (((0,), (0,)), ((), ())), preferred_element_type=F32)


def _rms_scale(x):
    return x * lax.rsqrt(jnp.mean(x * x, axis=-1, keepdims=True) + EPS)


def _silu(x):
    return x * jax.nn.sigmoid(x)


def _group_sumsq(x, g_ref):
    return _dot((x * x).astype(BF16), g_ref[...])


def _proj_attn_kernel(x_ref, gain_ref, w_ref, cos_ref, sina_ref, sinb_ref, qa_ref, ka_ref, va_ref):
    u = (_rms_scale(x_ref[0]) * gain_ref[...]).astype(BF16)
    proj = _dot(u, w_ref[...])
    va_ref[0] = proj[:, 2 * MIX_HALF:].astype(BF16)
    cos, sina, sinb = cos_ref[...], sina_ref[...], sinb_ref[...]
    for s in range(2 * DA_HEADS):
        xs = proj[:, SLAB * s:SLAB * (s + 1)]
        r = xs * cos + pltpu.roll(xs, SLAB - 32, 1) * sina + pltpu.roll(xs, 32, 1) * sinb
        if s < DA_HEADS:
            qa_ref[0, :, SLAB * s:SLAB * (s + 1)] = (r * (HEAD_DIM ** -0.5 * LOG2E)).astype(BF16)
        else:
            t = s - DA_HEADS
            ka_ref[0, :, SLAB * t:SLAB * (t + 1)] = r.astype(BF16)


def _proj_attn(x, gain, w_a, cos, sina, sinb, tb):
    b, s, d = x.shape
    nb = s // tb
    slab_out = jax.ShapeDtypeStruct((b, s, MIX_HALF), BF16)
    full = lambda shape: pl.BlockSpec(shape, lambda i, j: (0,) * len(shape))
    tok = pl.BlockSpec((1, tb, MIX_HALF), lambda i, j: (i, j, 0))
    tab = pl.BlockSpec((tb, SLAB), lambda i, j: (j, 0))
    return pl.pallas_call(
        _proj_attn_kernel,
        grid=(b, nb),
        in_specs=[pl.BlockSpec((1, tb, d), lambda i, j: (i, j, 0)), full((1, d)), full(w_a.shape), tab, tab, tab],
        out_specs=[tok, tok, tok],
        out_shape=[slab_out, slab_out, slab_out],
        compiler_params=pltpu.CompilerParams(dimension_semantics=("parallel", "parallel"),
                                             vmem_limit_bytes=VMEM_LIMIT),
        name="proj_attn",
    )(x, gain, w_a, cos, sina, sinb)


def _proj_gdn_kernel(x_ref, halo_ref, gain_ref, wb_ref, wz_ref, wg_ref, convw_ref, alog_ref, dtb_ref, g_ref,
                     q_ref, k_ref, v_ref, z_ref, gb_ref, beta_ref, pb_scr):
    halo_rows = halo_ref.shape[2]
    xe = jnp.concatenate([halo_ref[0, 0], x_ref[0]], axis=0)
    ue = (_rms_scale(xe) * gain_ref[...]).astype(BF16)
    u = ue[halo_rows:]
    cw = convw_ref[...]
    outs = (q_ref, k_ref, v_ref)
    for part in range(3):
        cols = slice(MIX_HALF * part, MIX_HALF * (part + 1))
        pb_scr[...] = _dot(ue, wb_ref[:, cols])
        c = cw[:, cols]
        tb = pb_scr.shape[0] - halo_rows
        y = pb_scr[halo_rows:, :] * c[CONV_K - 1:CONV_K]
        for back in range(1, CONV_K):
            tap = CONV_K - 1 - back
            y = y + pb_scr[pl.ds(halo_rows - back, tb), :] * c[tap:tap + 1]
        y = _silu(y)
        if part < 2:
            y = y * lax.rsqrt(_group_sumsq(y, g_ref) + EPS)
            if part == 0:
                y = y * (HEAD_DIM ** -0.5)
        outs[part][0] = y.astype(BF16)
    z_ref[0] = _silu(_dot(u, wz_ref[...])).astype(BF16)
    gates = _dot(u, wg_ref[...])
    ab = gates + dtb_ref[...]
    softplus = jnp.maximum(ab, 0.0) + jnp.log1p(jnp.exp(-jnp.abs(ab)))
    g = -jnp.exp(alog_ref[...]) * softplus
    beta = jax.nn.sigmoid(gates)
    tb = g.shape[0]
    first_head = lax.broadcasted_iota(jnp.int32, (tb, SLAB), 1) < HEAD_DIM
    spread = lambda v, h: jnp.broadcast_to(v[:, h:h + 1], (tb, SLAB))
    for p in range(GDN_HEADS // 2):
        cols = slice(SLAB * p, SLAB * (p + 1))
        gb_ref[0, :, cols] = jnp.where(first_head, spread(g, 2 * p), spread(g, 2 * p + 1))
        beta_ref[0, :, cols] = jnp.where(first_head, spread(beta, GDN_HEADS + 2 * p),
                                         spread(beta, GDN_HEADS + 2 * p + 1)).astype(BF16)


def _proj_gdn(x, halo, gain, w_b, w_z, w_g, conv_w, alog_rep, dtb_rep, g512, tb):
    b, s, d = x.shape
    nb = s // tb
    full = lambda shape: pl.BlockSpec(shape, lambda i, j: (0,) * len(shape))
    tok = pl.BlockSpec((1, tb, MIX_HALF), lambda i, j: (i, j, 0))
    bf = jax.ShapeDtypeStruct((b, s, MIX_HALF), BF16)
    return pl.pallas_call(
        _proj_gdn_kernel,
        grid=(b, nb),
        in_specs=[pl.BlockSpec((1, tb, d), lambda i, j: (i, j, 0)),
                  pl.BlockSpec((1, 1) + halo.shape[2:], lambda i, j: (i, j, 0, 0)),
                  full((1, d)), full(w_b.shape), full(w_z.shape), full(w_g.shape), full(conv_w.shape),
                  full((1, SLAB)), full((1, SLAB)), full(g512.shape)],
        out_specs=[tok] * 6,
        out_shape=[bf, bf, bf, bf, jax.ShapeDtypeStruct((b, s, MIX_HALF), F32), bf],
        scratch_shapes=[pltpu.VMEM((halo.shape[2] + tb, MIX_HALF), F32)],
        compiler_params=pltpu.CompilerParams(dimension_semantics=("parallel", "parallel"),
                                             vmem_limit_bytes=VMEM_LIMIT),
        name="proj_gdn",
    )(x, halo, gain, w_b, w_z, w_g, conv_w, alog_rep, dtb_rep, g512)


def _attn_kernel(q_ref, k_ref, v_ref, km_ref, vm_ref, lq1_ref, lk1_ref, lq2_ref, lk2_ref, dn_ref, o_ref,
                 qq_scr, m_scr, acc_scr, st_scr, *, qb, cw):
    i = pl.program_id(2)
    n_col = 2 * qb // cw
    per_map = qb // cw
    q = q_ref[0]
    lane = lax.broadcasted_iota(jnp.int32, (qb, SLAB), 1)
    zero = jnp.zeros_like(q)
    qq_scr[...] = jnp.concatenate([jnp.where(lane < HEAD_DIM, q, zero), jnp.where(lane < HEAD_DIM, zero, q)], axis=0)

    def with_ones(v_blk):
        return jnp.concatenate([v_blk, jnp.ones_like(v_blk)], axis=1)

    v1_meta = with_ones(vm_ref[0])
    for c in range(n_col):
        rs = slice(cw * c, cw * (c + 1))
        s = _dot_nt(qq_scr[rs, :], km_ref[0])
        m_new = jnp.max(s, axis=1, keepdims=True)
        acc_scr[rs, :] = _dot(jnp.exp2((s - m_new).astype(BF16)), v1_meta)
        m_scr[rs, :] = jnp.broadcast_to(m_new, (cw, SLAB))

    def scores_into(slot, start, c):
        st_scr[slot] = _dot_nt(qq_scr[cw * c:cw * (c + 1), :], k_ref[0, pl.ds(start, cw), :])

    def softmax_pv(slot, v1, c, masked):
        alphas = []
        for t in range(cw // ROW_STRIP):
            ls = slice(ROW_STRIP * t, ROW_STRIP * (t + 1))
            gs = slice(cw * c + ROW_STRIP * t, cw * c + ROW_STRIP * (t + 1))
            s = st_scr[slot, ls, :]
            if masked:
                q_chunk = (lax.broadcasted_iota(jnp.int32, s.shape, 0) + ROW_STRIP * t) // CHUNK
                k_chunk = lax.broadcasted_iota(jnp.int32, s.shape, 1) // CHUNK
                s = jnp.where(k_chunk <= q_chunk, s, NEG_BIG)
            parts = [s[:, SLAB * k:SLAB * (k + 1)] for k in range(cw // SLAB)]
            lane_max = functools.reduce(jnp.maximum, parts)
            m_prev = m_scr[gs, :]
            m_new = jnp.maximum(m_prev, jnp.max(lane_max, axis=1, keepdims=True))
            for k, part in enumerate(parts):
                st_scr[slot, ls, SLAB * k:SLAB * (k + 1)] = jnp.exp2(part - m_new)
            alphas.append(jnp.exp2(m_prev - m_new))
            m_scr[gs, :] = m_new
        rs = slice(cw * c, cw * (c + 1))
        alpha = jnp.concatenate(alphas, axis=0)
        pv = _dot(st_scr[slot].astype(BF16), v1)
        acc_scr[rs, :] = acc_scr[rs, :] * jnp.concatenate([alpha, alpha], axis=1) + pv

    parity = [0]
    scores_into(0, 0, 0)

    def key_block(start, groups, masked_groups, following):
        v1 = with_ones(v_ref[0, pl.ds(start, cw), :])
        for idx, c in enumerate(groups):
            slot = parity[0]
            if idx + 1 < len(groups):
                scores_into(1 - slot, start, groups[idx + 1])
            elif following is not None:
                scores_into(1 - slot, *following)
            softmax_pv(slot, v1, c, c in masked_groups)
            parity[0] = 1 - slot

    all_groups = list(range(n_col))

    def full_blocks(j, carry):
        for d in range(per_map):
            start = pl.multiple_of((per_map * j + d) * cw, cw)
            key_block(start, all_groups, (), (start + cw, 0))
        return carry

    lax.fori_loop(0, i, full_blocks, 0)
    for d in range(per_map):
        start = pl.multiple_of((per_map * i + d) * cw, cw)
        groups = [c for c in all_groups if c % per_map >= d]
        following = (start + cw, d + 1) if d + 1 < per_map else None
        key_block(start, groups, [c for c in groups if c % per_map == d], following)

    acc = acc_scr[...]
    o1 = acc[:qb, :SLAB] / acc[:qb, SLAB:]
    o2 = acc[qb:, :SLAB] / acc[qb:, SLAB:]
    lam = (jnp.exp(jnp.sum(lq1_ref[...] * lk1_ref[...], axis=1, keepdims=True))
           - jnp.exp(jnp.sum(lq2_ref[...] * lk2_ref[...], axis=1, keepdims=True)) + LAM_INIT)
    o = o1 - lam * o2
    o_ref[0] = (_rms_scale(o) * dn_ref[...] * (1.0 - LAM_INIT)).astype(BF16)


def _diff_attention(qa, ka, va, ka_meta, va_meta, lq1, lk1, lq2, lk2, diff_norm, qb, cw):
    b, s, _ = qa.shape
    nq = s // qb
    vec = lambda n: pl.BlockSpec((1, n), lambda bi, h, i: (0, 0))
    seq = pl.BlockSpec((1, s, SLAB), lambda bi, h, i: (bi, 0, h))
    meta = pl.BlockSpec((1, N_META, SLAB), lambda bi, h, i: (0, 0, h))
    return pl.pallas_call(
        functools.partial(_attn_kernel, qb=qb, cw=cw),
        grid=(b, DA_HEADS, nq),
        in_specs=[pl.BlockSpec((1, qb, SLAB), lambda bi, h, i: (bi, i, h)), seq, seq, meta, meta,
                  vec(HEAD_DIM), vec(HEAD_DIM), vec(HEAD_DIM), vec(HEAD_DIM), vec(SLAB)],
        out_specs=pl.BlockSpec((1, qb, SLAB), lambda bi, h, i: (bi, i, h)),
        out_shape=jax.ShapeDtypeStruct((b, s, MIX_HALF), BF16),
        scratch_shapes=[pltpu.VMEM((2 * qb, SLAB), BF16), pltpu.VMEM((2 * qb, SLAB), F32),
                        pltpu.VMEM((2 * qb, 2 * SLAB), F32), pltpu.VMEM((2, cw, cw), F32)],
        compiler_params=pltpu.CompilerParams(dimension_semantics=("parallel", "parallel", "arbitrary"),
                                             vmem_limit_bytes=VMEM_LIMIT),
        name="diff_attention",
    )(qa, ka, va, ka_meta, va_meta, lq1, lk1, lq2, lk2, diff_norm)


def _gdn_kernel(q_ref, k_ref, v_ref, z_ref, g_ref, beta_ref, gn_ref, g512_ref, s0_ref, o_ref, sfin_ref,
                s_scr, o_scr, *, n_chunks):
    j = pl.program_id(0)
    n_pairs = GDN_HEADS // GDN_GROUP
    n_seq = q_ref.shape[0]

    @pl.when(j == 0)
    def _():
        for bi in range(n_seq):
            s_scr[bi] = s0_ref[...]

    row = lax.broadcasted_iota(jnp.int32, (CHUNK, GDN_GW), 0)
    lane = lax.broadcasted_iota(jnp.int32, (CHUNK, GDN_GW), 1)
    col = lane % CHUNK
    lane_head = lane // HEAD_DIM
    tri_incl = row >= col
    tri_strict = row > col
    eye = (row == col).astype(F32)
    bd_mask = ((lax.broadcasted_iota(jnp.int32, (GDN_GW, GDN_GW), 0) // HEAD_DIM)
               == (lax.broadcasted_iota(jnp.int32, (GDN_GW, GDN_GW), 1) // HEAD_DIM))

    def block_diag(x):
        xb = x.astype(BF16)
        zero = jnp.zeros_like(xb)
        return jnp.concatenate([jnp.where(lane_head == h, xb, zero) for h in range(GDN_GROUP)], axis=0)

    def pair_mm(x, y):
        return _dot(x.astype(BF16), block_diag(y))

    items = [(bi, c, p) for bi in range(n_seq) for c in range(n_chunks) for p in range(n_pairs)]
    rows_of = lambda c: slice(CHUNK * c, CHUNK * (c + 1))
    cols_of = lambda p: slice(GDN_GW * p, GDN_GW * (p + 1))
    load = lambda ref, it: ref[it[0], rows_of(it[1]), cols_of(it[2])]

    tbg = n_chunks * CHUNK
    ri = lax.broadcasted_iota(jnp.int32, (tbg, tbg), 0)
    ci = lax.broadcasted_iota(jnp.int32, (tbg, tbg), 1)
    chunk_tri = ((ri >= ci) & (ri // CHUNK == ci // CHUNK)).astype(BF16)
    g_cum = []
    for bi in range(n_seq):
        rest = g_ref[bi]
        terms = []
        for _ in range(3):
            term = rest.astype(BF16)
            terms.append(term)
            rest = rest - term.astype(F32)
        cum = _dot(chunk_tri, jnp.concatenate(terms, axis=1))
        g_cum.append(cum[:, :MIX_HALF] + cum[:, MIX_HALF:2 * MIX_HALF] + cum[:, 2 * MIX_HALF:])

    g_col, decay, k_f, k_beta, k_bd, q_f, vb = {}, {}, {}, {}, {}, {}, {}
    for it in items:
        g = load(g_ref, it)
        g_col[it] = g_cum[it[0]][rows_of(it[1]), cols_of(it[2])]
        g_row = jnp.sum(jnp.where(row <= col, g, 0.0), axis=0, keepdims=True)
        decay[it] = jnp.where(tri_incl, jnp.exp(jnp.where(tri_incl, g_col[it] - g_row, 0.0)), 0.0)
        beta = load(beta_ref, it).astype(F32)
        k_f[it] = load(k_ref, it).astype(F32)
        q_f[it] = load(q_ref, it).astype(F32)
        k_beta[it] = k_f[it] * beta
        vb[it] = load(v_ref, it).astype(F32) * beta
        k_bd[it] = block_diag(k_f[it])

    a, a_qk = {}, {}
    for it in items:
        lhs = jnp.concatenate([k_beta[it].astype(BF16), q_f[it].astype(BF16)], axis=0)
        kk = _dot_nt(lhs, k_bd[it])
        a[it] = jnp.where(tri_strict, kk[:CHUNK] * decay[it], 0.0)
        a_qk[it] = jnp.where(tri_incl, kk[CHUNK:] * decay[it], 0.0)

    t = {it: eye - a[it] for it in items}
    pw = dict(a)
    for _ in range(5):
        for it in items:
            pw[it] = pair_mm(pw[it], pw[it])
        for it in items:
            t[it] = t[it] + pair_mm(t[it], pw[it])

    u, w, k_g, q_g, e_last = {}, {}, {}, {}, {}
    for it in items:
        e_col = jnp.exp(g_col[it])
        rhs = jnp.concatenate([block_diag(vb[it]), block_diag(k_beta[it] * e_col)], axis=1)
        uw = _dot(t[it].astype(BF16), rhs)
        u[it], w[it] = uw[:, :GDN_GW], uw[:, GDN_GW:]
        g_last = g_col[it][CHUNK - 1:CHUNK, :]
        k_g[it] = (k_f[it] * jnp.exp(g_last - g_col[it])).astype(BF16)
        q_g[it] = q_f[it] * e_col
        e_last[it] = jnp.exp(g_last)

    state = {(bi, p): s_scr[bi, p] for bi in range(n_seq) for p in range(n_pairs)}
    sp = lambda it: (it[0], it[2])
    for c in range(n_chunks):
        its = [(bi, c, p) for bi in range(n_seq) for p in range(n_pairs)]
        ws_qs = [_dot(jnp.concatenate([w[it], q_g[it]], axis=0).astype(BF16), state[sp(it)].astype(BF16))
                 for it in its]
        v_new = [u[it] - sq[:CHUNK] for it, sq in zip(its, ws_qs)]
        for it, sq, vn in zip(its, ws_qs, v_new):
            o_scr[it[0], rows_of(c), cols_of(it[2])] = sq[CHUNK:] + pair_mm(a_qk[it], vn)
        for it, vn in zip(its, v_new):
            upd = _dot_tn(k_g[it], vn.astype(BF16))
            state[sp(it)] = state[sp(it)] * e_last[it] + jnp.where(bd_mask, upd, 0.0)
    for (bi, p), val in state.items():
        s_scr[bi, p] = val

    for bi in range(n_seq):
        o = o_scr[bi]
        ss = _group_sumsq(o, g512_ref)
        y = o * lax.rsqrt(ss * (1.0 / HEAD_DIM) + EPS) * gn_ref[...]
        o_ref[bi] = (y * z_ref[bi].astype(F32)).astype(BF16)

    @pl.when(j == pl.num_programs(0) - 1)
    def _():
        sfin_ref[...] = s_scr[...]


def _gated_delta(q, k, v, z, gb, beta, gn_rep, g512, s0, tbg):
    b, s, _ = q.shape
    nb = s // tbg
    state_shape = (b,) + s0.shape
    tok = pl.BlockSpec((b, tbg, MIX_HALF), lambda j: (0, j, 0))
    full = lambda shape: pl.BlockSpec(shape, lambda j: (0,) * len(shape))
    return pl.pallas_call(
        functools.partial(_gdn_kernel, n_chunks=tbg // CHUNK),
        grid=(nb,),
        in_specs=[tok] * 6 + [full((1, MIX_HALF)), full(g512.shape), full(s0.shape)],
        out_specs=[tok, pl.BlockSpec(state_shape, lambda j: (0, 0, 0, 0))],
        out_shape=[jax.ShapeDtypeStruct((b, s, MIX_HALF), BF16), jax.ShapeDtypeStruct(state_shape, F32)],
        scratch_shapes=[pltpu.VMEM(state_shape, F32), pltpu.VMEM((b, tbg, MIX_HALF), F32)],
        compiler_params=pltpu.CompilerParams(dimension_semantics=("arbitrary",),
                                             vmem_limit_bytes=VMEM_LIMIT),
        name="gated_delta",
    )(q, k, v, z, gb, beta, gn_rep, g512, s0)


def _out_router_kernel(x_ref, oa_ref, ob_ref, wo_ref, gain_ref, wr_ref, br_ref, before_ref,
                       h_ref, t_ref, rows_ref, cols_ref, cnt_ref):
    h = x_ref[0] + _dot(jnp.concatenate([oa_ref[0], ob_ref[0]], axis=1), wo_ref[...])
    h_ref[0] = h
    t = _rms_scale(h) * gain_ref[...]
    t_hi = t.astype(BF16)
    t_ref[0] = t_hi
    t_lo = (t - t_hi.astype(F32)).astype(BF16)
    wr = wr_ref[...]
    hi_part = _dot(t_hi, wr)
    logits = hi_part[:, :ROUTE_LANES] + hi_part[:, ROUTE_LANES:] + _dot(t_lo, wr)[:, :ROUTE_LANES] + br_ref[...]

    lt = logits.T
    tb = lt.shape[1]
    row8 = lax.broadcasted_iota(jnp.int32, (EXPERTS_PER_GROUP, tb), 0)
    col_max = lambda v: jnp.max(v, axis=0, keepdims=True)
    col_sum = lambda v: jnp.sum(v, axis=0, keepdims=True)
    first_argmax = lambda v, vmax: jnp.min(jnp.where(v == vmax, row8, EXPERTS_PER_GROUP), axis=0, keepdims=True)

    gl = jnp.where(row8 < N_GROUPS, lt[:EXPERTS_PER_GROUP], NEG_BIG)
    gmax = col_max(gl)
    gsel = first_argmax(gl, gmax)
    psel = 1.0 / col_sum(jnp.exp(gl - gmax))
    el = jnp.zeros((EXPERTS_PER_GROUP, tb), F32)
    for g in range(N_GROUPS):
        lo = EXPERT_ROW0 + EXPERTS_PER_GROUP * g
        el = jnp.where(gsel == g, lt[lo:lo + EXPERTS_PER_GROUP], el)
    m1 = col_max(el)
    i1 = first_argmax(el, m1)
    el2 = jnp.where(row8 == i1, NEG_BIG, el)
    m2 = col_max(el2)
    i2 = first_argmax(el2, m2)
    denom = col_sum(jnp.exp(el - m1))
    p1 = 1.0 / denom
    p2 = jnp.exp(m2 - m1) / denom
    w1 = p1 / (p1 + p2) * psel
    w2 = p2 / (p1 + p2) * psel
    e1 = gsel * EXPERTS_PER_GROUP + i1
    e2 = gsel * EXPERTS_PER_GROUP + i2

    row_e = lax.broadcasted_iota(jnp.int32, (N_EXPERTS, tb), 0)
    hot1 = (row_e == e1).astype(F32)
    hot2 = (row_e == e2).astype(F32)
    both = hot1 + hot2
    n_gran = jnp.ceil(jnp.sum(both, axis=1, keepdims=True) * (1.0 / SLOT_GRAN)) * SLOT_GRAN
    ei = lax.broadcasted_iota(jnp.int32, (N_EXPERTS, N_EXPERTS), 0)
    ej = lax.broadcasted_iota(jnp.int32, (N_EXPERTS, N_EXPERTS), 1)
    seg_start = _dot((ei > ej).astype(F32), jnp.broadcast_to(n_gran, (N_EXPERTS, SLAB)), precision=HIGHEST)[:, :1]
    earlier = _dot(both.astype(BF16), before_ref[...])
    where_to = earlier + seg_start
    pos1 = col_sum(hot1 * where_to)
    pos2 = col_sum(hot2 * where_to)

    info = jnp.concatenate([pos1, pos2, w1, w2, e1.astype(F32), e2.astype(F32),
                            jnp.zeros((SLAB - 6, tb), F32)], axis=0)
    rows_ref[0] = info[:8]
    cols_ref[0] = info.T
    cnt_ref[0] = jnp.broadcast_to(n_gran, (N_EXPERTS, SLAB))


def _out_router(x, oa, ob, w_o, gain, w_route, b_route, before, tb):
    b, s, d = x.shape
    nb = s // tb
    full = lambda shape: pl.BlockSpec(shape, lambda i, j: (0,) * len(shape))
    tokd = pl.BlockSpec((1, tb, d), lambda i, j: (i, j, 0))
    tokh = pl.BlockSpec((1, tb, MIX_HALF), lambda i, j: (i, j, 0))
    return pl.pallas_call(
        _out_router_kernel,
        grid=(b, nb),
        in_specs=[tokd, tokh, tokh, full(w_o.shape), full((1, d)), full(w_route.shape), full((1, ROUTE_LANES)),
                  full(before.shape)],
        out_specs=[tokd, tokd, pl.BlockSpec((1, 8, tb), lambda i, j: (i, 0, j)),
                   pl.BlockSpec((1, tb, SLAB), lambda i, j: (i, j, 0)),
                   pl.BlockSpec((1, N_EXPERTS, SLAB), lambda i, j: (i * nb + j, 0, 0))],
        out_shape=[jax.ShapeDtypeStruct((b, s, d), F32), jax.ShapeDtypeStruct((b, s, d), BF16),
                   jax.ShapeDtypeStruct((b, 8, s), F32), jax.ShapeDtypeStruct((b, s, SLAB), F32),
                   jax.ShapeDtypeStruct((b * nb, N_EXPERTS, SLAB), F32)],
        compiler_params=pltpu.CompilerParams(dimension_semantics=("parallel", "parallel"),
                                             vmem_limit_bytes=VMEM_LIMIT),
        name="out_router",
    )(x, oa, ob, w_o, gain, w_route, b_route, before)


def _granule_copies(table_ref, block, buf, local_ref, global_ref, sem, to_global, n_gran):
    copies = []
    for g in range(n_gran):
        loc = local_ref.at[buf, pl.ds(g * SLOT_GRAN, SLOT_GRAN)]
        glob = global_ref.at[pl.ds(pl.multiple_of(table_ref[block, g], SLOT_GRAN), SLOT_GRAN)]
        copies.append(pltpu.make_async_copy(loc, glob, sem.at[buf]) if to_global
                      else pltpu.make_async_copy(glob, loc, sem.at[buf]))
    return copies


def _dispatch_kernel(gdst_ref, t_ref, rows_ref, xg_init_ref, xg_ref, xs_scr, sem, *, n_slots):
    del xg_init_ref
    blk = pl.program_id(0)
    last = pl.num_programs(0) - 1
    buf = blk % 2
    tb = t_ref.shape[0]
    n_gran = n_slots // SLOT_GRAN
    info = rows_ref[0]
    slot = lax.broadcasted_iota(jnp.int32, (n_slots, tb), 0).astype(F32)
    onehot = jnp.where((slot == info[0:1]) | (slot == info[1:2]), 1.0, 0.0).astype(BF16)
    xs_scr[buf] = _dot(onehot, t_ref[...]).astype(BF16)
    for copy in _granule_copies(gdst_ref, blk, buf, xs_scr, xg_ref, sem, True, n_gran):
        copy.start()

    @pl.when(blk > 0)
    def _():
        for copy in _granule_copies(gdst_ref, blk - 1, 1 - buf, xs_scr, xg_ref, sem, True, n_gran):
            copy.wait()

    @pl.when(blk == last)
    def _():
        for copy in _granule_copies(gdst_ref, blk, buf, xs_scr, xg_ref, sem, True, n_gran):
            copy.wait()


def _dispatch(granule_dst, t, rows, xg_init, tb, n_slots):
    n, d = t.shape
    nblk = n // tb
    nb = rows.shape[2] // tb
    grid_spec = pltpu.PrefetchScalarGridSpec(
        num_scalar_prefetch=1,
        grid=(nblk,),
        in_specs=[pl.BlockSpec((tb, d), lambda i, gd: (i, 0)),
                  pl.BlockSpec((1, 8, tb), lambda i, gd: (i // nb, 0, i % nb)),
                  pl.BlockSpec(memory_space=pl.ANY)],
        out_specs=pl.BlockSpec(memory_space=pl.ANY),
        scratch_shapes=[pltpu.VMEM((2, n_slots, d), BF16), pltpu.SemaphoreType.DMA((2,))],
    )
    return pl.pallas_call(
        functools.partial(_dispatch_kernel, n_slots=n_slots),
        grid_spec=grid_spec,
        out_shape=jax.ShapeDtypeStruct(xg_init.shape, BF16),
        input_output_aliases={3: 0},
        compiler_params=pltpu.CompilerParams(dimension_semantics=("arbitrary",), vmem_limit_bytes=VMEM_LIMIT),
        name="moe_dispatch",
    )(granule_dst, t, rows, xg_init)


def _experts_kernel(tile_expert_ref, n_tiles_ref, x_ref, wg_ref, wu_ref, wd_ref, y_ref):
    del tile_expert_ref

    @pl.when(pl.program_id(0) < n_tiles_ref[0])
    def _():
        x = x_ref[...]
        act = _silu(_dot(x, wg_ref[0].astype(BF16))) * _dot(x, wu_ref[0].astype(BF16))
        y_ref[...] = _dot(act.astype(BF16), wd_ref[0].astype(BF16)).astype(BF16)


def _experts(tile_expert, n_tiles, xg, w_gate, w_up, w_down, tile):
    rows, d = xg.shape
    used = lambda i, te, nt: jnp.minimum(i, nt[0] - 1)
    wspec = lambda shape: pl.BlockSpec((1,) + shape, lambda i, te, nt: (te[used(i, te, nt)], 0, 0))
    grid_spec = pltpu.PrefetchScalarGridSpec(
        num_scalar_prefetch=2,
        grid=(tile_expert.shape[0],),
        in_specs=[pl.BlockSpec((tile, d), lambda i, te, nt: (used(i, te, nt), 0)),
                  wspec((d, D_EXPERT)), wspec((d, D_EXPERT)), wspec((D_EXPERT, d))],
        out_specs=pl.BlockSpec((tile, d), lambda i, te, nt: (used(i, te, nt), 0)),
    )
    return pl.pallas_call(
        _experts_kernel,
        grid_spec=grid_spec,
        out_shape=jax.ShapeDtypeStruct((rows, d), BF16),
        input_output_aliases={2: 0},
        compiler_params=pltpu.CompilerParams(dimension_semantics=("arbitrary",), vmem_limit_bytes=VMEM_LIMIT),
        name="moe_experts",
    )(tile_expert, n_tiles, xg, w_gate, w_up, w_down)


def _combine_kernel(gsrc_ref, yg_ref, cols_ref, h_ref, gain_ref, o_ref, ys_scr, sem, *, n_slots):
    blk = pl.program_id(0)
    last = pl.num_programs(0) - 1
    buf = blk % 2
    tb = h_ref.shape[0]
    n_gran = n_slots // SLOT_GRAN

    @pl.when(blk == 0)
    def _():
        for copy in _granule_copies(gsrc_ref, 0, 0, ys_scr, yg_ref, sem, False, n_gran):
            copy.start()

    @pl.when(blk < last)
    def _():
        for copy in _granule_copies(gsrc_ref, blk + 1, 1 - buf, ys_scr, yg_ref, sem, False, n_gran):
            copy.start()

    info = cols_ref[...]
    slot = lax.broadcasted_iota(jnp.int32, (tb, n_slots), 1).astype(F32)
    weights = (jnp.where(slot == info[:, 0:1], info[:, 2:3], 0.0)
               + jnp.where(slot == info[:, 1:2], info[:, 3:4], 0.0)).astype(BF16)
    for copy in _granule_copies(gsrc_ref, blk, buf, ys_scr, yg_ref, sem, False, n_gran):
        copy.wait()
    y = _dot(weights, ys_scr[buf])
    o_ref[...] = _rms_scale(h_ref[...] + y) * gain_ref[...]


def _combine(granule_dst, yg, cols, h, gain, tb, n_slots):
    n, d = h.shape
    grid_spec = pltpu.PrefetchScalarGridSpec(
        num_scalar_prefetch=1,
        grid=(n // tb,),
        in_specs=[pl.BlockSpec(memory_space=pl.ANY),
                  pl.BlockSpec((tb, SLAB), lambda i, gd: (i, 0)),
                  pl.BlockSpec((tb, d), lambda i, gd: (i, 0)),
                  pl.BlockSpec((1, d), lambda i, gd: (0, 0))],
        out_specs=pl.BlockSpec((tb, d), lambda i, gd: (i, 0)),
        scratch_shapes=[pltpu.VMEM((2, n_slots, d), BF16), pltpu.SemaphoreType.DMA((2,))],
    )
    return pl.pallas_call(
        functools.partial(_combine_kernel, n_slots=n_slots),
        grid_spec=grid_spec,
        out_shape=jax.ShapeDtypeStruct((n, d), F32),
        compiler_params=pltpu.CompilerParams(dimension_semantics=("arbitrary",), vmem_limit_bytes=VMEM_LIMIT),
        name="moe_combine",
    )(granule_dst, yg, cols, h, gain)


def _regroup_plan(n_gran_be, n_slots, tile, n_tiles_max):
    cnt = n_gran_be.astype(jnp.int32)
    region = (jnp.sum(cnt, axis=0) + tile - 1) // tile * tile
    region_end = jnp.cumsum(region)
    first_row = (region_end - region)[None, :] + jnp.cumsum(cnt, axis=0) - cnt
    seg_end = jnp.cumsum(cnt, axis=1)
    seg_start = seg_end - cnt
    g_row = jnp.arange(n_slots // SLOT_GRAN, dtype=jnp.int32) * SLOT_GRAN
    owned = ((g_row[None, :, None] >= seg_start[:, None, :]) & (g_row[None, :, None] < seg_end[:, None, :]))
    in_use = g_row[None, :] < seg_end[:, -1:]
    row = g_row[None, :] + jnp.sum(jnp.where(owned, (first_row - seg_start)[:, None, :], 0), axis=-1)
    zero_row = n_tiles_max * tile
    spill_row = zero_row + SLOT_GRAN + (jnp.arange(cnt.shape[0], dtype=jnp.int32) % 2)[:, None] * n_slots + g_row
    dispatch_dst = jnp.where(in_use, row, spill_row).astype(jnp.int32)
    combine_src = jnp.where(in_use, row, zero_row).astype(jnp.int32)
    tile_row = jnp.arange(n_tiles_max, dtype=jnp.int32) * tile
    tile_expert = jnp.minimum(jnp.sum(tile_row[:, None] >= region_end[None, :], axis=-1), N_EXPERTS - 1)
    n_tiles = (region_end[-1] // tile).astype(jnp.int32)[None]
    return dispatch_dst, combine_src, tile_expert.astype(jnp.int32), n_tiles


def _rope_tables(length):
    pos = jnp.arange(length, dtype=F32)
    inv_freq = ROPE_THETA ** (-jnp.arange(0, HEAD_DIM, 2, dtype=F32) / HEAD_DIM)
    ang = pos[:, None] * inv_freq[None, :]
    cos, sin = jnp.cos(ang), jnp.sin(ang)
    zero = jnp.zeros_like(sin)
    cos_t = jnp.tile(cos, (1, 4))
    sina_t = jnp.tile(jnp.concatenate([-sin, zero], axis=1), (1, 2))
    sinb_t = jnp.tile(jnp.concatenate([zero, sin], axis=1), (1, 2))
    return cos_t, sina_t, sinb_t


def _block(total, want):
    blk = min(total, want)
    assert total % blk == 0, (total, blk)
    return blk


def kernel(x, meta, norm_mix, w_in, lambda_q1, lambda_k1, lambda_q2, lambda_k2, diff_norm, conv_w, a_log, dt_bias,
           gdn_norm, w_out, norm_ffn, w_group, b_group, w_router, b_router, w_gate, w_up, w_down, norm_final):
    b, s, d = x.shape
    assert d == D_MODEL and meta.shape == (N_META, D_MODEL) and s % CHUNK == 0
    assert norm_mix.shape[0] == 1, "single-layer block"
    l = 0

    w = w_in[l]
    w_a = w[:, :3 * MIX_HALF].astype(BF16)
    w_b = w[:, 3 * MIX_HALF:6 * MIX_HALF].astype(BF16)
    w_z = w[:, 6 * MIX_HALF:7 * MIX_HALF].astype(BF16)
    c_ab = 7 * MIX_HALF
    lane_pad = lambda a: jnp.pad(a, [(0, 0)] * (a.ndim - 1) + [(0, SLAB - a.shape[-1])])
    w_g = lane_pad(w[:, c_ab:c_ab + 2 * GDN_HEADS]).astype(BF16)
    alog_rep = lane_pad(a_log[l])[None]
    dtb_rep = lane_pad(dt_bias[l])[None]
    gn_rep = jnp.tile(gdn_norm[l], GDN_HEADS)[None]
    g512 = (jnp.arange(MIX_HALF)[:, None] // HEAD_DIM == jnp.arange(MIX_HALF)[None, :] // HEAD_DIM).astype(BF16)
    gain_mix = norm_mix[l][None]
    w_o = w_out[l].astype(BF16)
    route_pad = lambda g, e: jnp.concatenate(
        [g, jnp.zeros(g.shape[:-1] + (EXPERT_ROW0 - N_GROUPS,), F32), e,
         jnp.zeros(g.shape[:-1] + (ROUTE_LANES - EXPERT_ROW0 - N_EXPERTS,), F32)], axis=-1)
    w_route_f = route_pad(w_group[l], w_router[l])
    w_route_hi = w_route_f.astype(BF16)
    w_route = jnp.concatenate([w_route_hi, (w_route_f - w_route_hi.astype(F32)).astype(BF16)], axis=1)
    b_route = route_pad(b_group[l], b_router[l])[None]
    lam_vecs = [v[l][None] for v in (lambda_q1, lambda_k1, lambda_q2, lambda_k2)]
    cos_t, sina_t, sinb_t = _rope_tables(N_META + s)

    meta3 = meta[None]
    _, ka_m, va_m = _proj_attn(meta3, gain_mix, w_a, cos_t[:N_META], sina_t[:N_META], sinb_t[:N_META], N_META)
    zero_halo = jnp.zeros((1, 1, N_META, d), F32)
    gdn_m = _proj_gdn(meta3, zero_halo, gain_mix, w_b, w_z, w_g, conv_w[l], alog_rep, dtb_rep, g512, N_META)
    pad_m = lambda a: jnp.pad(a, ((0, 0), (0, CHUNK - N_META), (0, 0)))
    s_zero = jnp.zeros((GDN_HEADS // GDN_GROUP, GDN_GW, GDN_GW), F32)
    _, s_meta = _gated_delta(*[pad_m(a) for a in gdn_m], gn_rep, g512, s_zero, CHUNK)

    tb_a = _block(s, 1024)
    qa, ka, va = _proj_attn(x, gain_mix, w_a, cos_t[N_META:], sina_t[N_META:], sinb_t[N_META:], tb_a)
    tb_g = _block(s, 512)
    nb_g = s // tb_g
    tails = x.reshape(b, nb_g, tb_g, d)[:, :-1, tb_g - N_META:, :]
    halo = jnp.concatenate([jnp.broadcast_to(meta[None, None], (b, 1, N_META, d)), tails], axis=1)
    gdn_f = _proj_gdn(x, halo, gain_mix, w_b, w_z, w_g, conv_w[l], alog_rep, dtb_rep, g512, tb_g)

    oa = _diff_attention(qa, ka, va, ka_m, va_m, *lam_vecs, diff_norm[l][None], _block(s, 2048), 512)
    ob, _ = _gated_delta(*gdn_f, gn_rep, g512, s_meta[0], _block(s, 256))

    tb_r = _block(s, MOE_BLOCK)
    tok = jnp.arange(tb_r)
    before = (tok[:, None] < tok[None, :]).astype(BF16)
    h1, t, rows, cols, cnt = _out_router(x, oa, ob, w_o, norm_ffn[l][None], w_route, b_route, before, tb_r)

    n = b * s
    nblk = n // tb_r
    n_slots = 2 * tb_r + N_EXPERTS * SLOT_GRAN
    rows_max = 2 * n + (SLOT_GRAN - 1) * N_EXPERTS * nblk + (MOE_TILE - 1) * N_EXPERTS
    n_tiles_max = -(-rows_max // MOE_TILE)
    dispatch_dst, combine_src, tile_expert, n_tiles = _regroup_plan(cnt[:, :, 0], n_slots, MOE_TILE, n_tiles_max)
    buf_rows = n_tiles_max * MOE_TILE + SLOT_GRAN + 2 * n_slots
    xg = _dispatch(dispatch_dst, t.reshape(n, d), rows, jnp.zeros((buf_rows, d), BF16), tb_r, n_slots)
    yg = _experts(tile_expert, n_tiles, xg, w_gate[l], w_up[l], w_down[l], MOE_TILE)
    out = _combine(combine_src, yg, cols.reshape(n, SLAB), h1.reshape(n, d), norm_final[None], tb_r, n_slots)
    return out.reshape(b, s, d)
```

```python
import functools
import math

import jax
import jax.numpy as jnp
from jax import lax
from jax.experimental import pallas as pl
from jax.experimental.pallas import tpu as pltpu

F32 = jnp.float32
BF16 = jnp.bfloat16
HIGHEST = lax.Precision.HIGHEST

D_MODEL = 1024
N_META = 16
CHUNK = 64
EPS = 1e-6
ROPE_THETA = 10000.0
HEAD_DIM = 64
SLAB = 128
DA_HEADS = 4
GDN_HEADS = 8
MIX_HALF = 512
GDN_GROUP = 2
GDN_GW = GDN_GROUP * HEAD_DIM
N_GROUPS = 4
EXPERTS_PER_GROUP = 8
N_EXPERTS = 32
D_EXPERT = 256
CONV_K = 4
LAM_INIT = 0.8 - 0.6 * math.exp(-0.3 * 0)
ROUTE_LANES = 128
EXPERT_ROW0 = 8
SLOT_GRAN = 16
MOE_BLOCK = 512
MOE_TILE = 512
ROW_STRIP = 64
PROJ_ATTN_ROWS = 1024
PROJ_GDN_ROWS = 512
ATTN_QUERY_ROWS = 2048
ATTN_KEY_BLOCK = 512
GDN_ROWS = 256
NEG_BIG = -1e30
LOG2E = math.log2(math.e)
VMEM_LIMIT = 56 * 1024 * 1024


def _dot(a, b, precision=None):
    return jnp.dot(a, b, preferred_element_type=F32, precision=precision)


def _dot_nt(a, b):
    return lax.dot_general(a, b, (((1,), (1,)), ((), ())), preferred_element_type=F32)


def _dot_tn(a, b):
    return lax.dot_general(a, b, (((0,), (0,)), ((), ())), preferred_element_type=F32)


def _rms_scale(x):
    return x * lax.rsqrt(jnp.mean(x * x, axis=-1, keepdims=True) + EPS)


def _silu(x):
    return x * jax.nn.sigmoid(x)


def _group_sumsq(x, g_ref):
    return _dot((x * x).astype(BF16), g_ref[...])


def _proj_attn_kernel(x_ref, gain_ref, w_ref, cos_ref, sina_ref, sinb_ref, qa_ref, ka_ref, va_ref):
    u = (_rms_scale(x_ref[0]) * gain_ref[...]).astype(BF16)
    proj = _dot(u, w_ref[...])
    va_ref[0] = proj[:, 2 * MIX_HALF:].astype(BF16)
    cos, sina, sinb = cos_ref[...], sina_ref[...], sinb_ref[...]
    for s in range(2 * DA_HEADS):
        xs = proj[:, SLAB * s:SLAB * (s + 1)]
        r = xs * cos + pltpu.roll(xs, SLAB - 32, 1) * sina + pltpu.roll(xs, 32, 1) * sinb
        if s < DA_HEADS:
            qa_ref[0, :, SLAB * s:SLAB * (s + 1)] = (r * (HEAD_DIM ** -0.5 * LOG2E)).astype(BF16)
        else:
            t = s - DA_HEADS
            ka_ref[0, :, SLAB * t:SLAB * (t + 1)] = r.astype(BF16)


def _proj_attn(x, gain, w_a, cos, sina, sinb, tb):
    b, s, d = x.shape
    nb = s // tb
    slab_out = jax.ShapeDtypeStruct((b, s, MIX_HALF), BF16)
    full = lambda shape: pl.BlockSpec(shape, lambda i, j: (0,) * len(shape))
    tok = pl.BlockSpec((1, tb, MIX_HALF), lambda i, j: (i, j, 0))
    tab = pl.BlockSpec((tb, SLAB), lambda i, j: (j, 0))
    return pl.pallas_call(
        _proj_attn_kernel,
        grid=(b, nb),
        in_specs=[pl.BlockSpec((1, tb, d), lambda i, j: (i, j, 0)), full((1, d)), full(w_a.shape), tab, tab, tab],
        out_specs=[tok, tok, tok],
        out_shape=[slab_out, slab_out, slab_out],
        compiler_params=pltpu.CompilerParams(dimension_semantics=("parallel", "parallel"),
                                             vmem_limit_bytes=VMEM_LIMIT),
        name="proj_attn",
    )(x, gain, w_a, cos, sina, sinb)


def _proj_gdn_kernel(x_ref, halo_ref, gain_ref, wb_ref, wz_ref, wg_ref, convw_ref, alog_ref, dtb_ref, g_ref,
                     q_ref, k_ref, v_ref, z_ref, gb_ref, beta_ref, pb_scr):
    halo_rows = halo_ref.shape[2]
    xe = jnp.concatenate([halo_ref[0, 0], x_ref[0]], axis=0)
    ue = (_rms_scale(xe) * gain_ref[...]).astype(BF16)
    u = ue[halo_rows:]
    cw = convw_ref[...]
    outs = (q_ref, k_ref, v_ref)
    for part in range(3):
        cols = slice(MIX_HALF * part, MIX_HALF * (part + 1))
        pb_scr[...] = _dot(ue, wb_ref[:, cols])
        c = cw[:, cols]
        tb = pb_scr.shape[0] - halo_rows
        y = pb_scr[halo_rows:, :] * c[CONV_K - 1:CONV_K]
        for back in range(1, CONV_K):
            tap = CONV_K - 1 - back
            y = y + pb_scr[pl.ds(halo_rows - back, tb), :] * c[tap:tap + 1]
        y = _silu(y)
        if part < 2:
            y = y * lax.rsqrt(_group_sumsq(y, g_ref) + EPS)
            if part == 0:
                y = y * (HEAD_DIM ** -0.5)
        outs[part][0] = y.astype(BF16)
    z_ref[0] = _silu(_dot(u, wz_ref[...])).astype(BF16)
    gates = _dot(u, wg_ref[...])
    ab = gates + dtb_ref[...]
    softplus = jnp.maximum(ab, 0.0) + jnp.log1p(jnp.exp(-jnp.abs(ab)))
    g = -jnp.exp(alog_ref[...]) * softplus
    beta = jax.nn.sigmoid(gates)
    tb = g.shape[0]
    first_head = lax.broadcasted_iota(jnp.int32, (tb, SLAB), 1) < HEAD_DIM
    spread = lambda v, h: jnp.broadcast_to(v[:, h:h + 1], (tb, SLAB))
    for p in range(GDN_HEADS // 2):
        cols = slice(SLAB * p, SLAB * (p + 1))
        gb_ref[0, :, cols] = jnp.where(first_head, spread(g, 2 * p), spread(g, 2 * p + 1))
        beta_ref[0, :, cols] = jnp.where(first_head, spread(beta, GDN_HEADS + 2 * p),
                                         spread(beta, GDN_HEADS + 2 * p + 1)).astype(BF16)


def _proj_gdn(x, halo, gain, w_b, w_z, w_g, conv_w, alog_rep, dtb_rep, g512, tb):
    b, s, d = x.shape
    nb = s // tb
    full = lambda shape: pl.BlockSpec(shape, lambda i, j: (0,) * len(shape))
    tok = pl.BlockSpec((1, tb, MIX_HALF), lambda i, j: (i, j, 0))
    bf = jax.ShapeDtypeStruct((b, s, MIX_HALF), BF16)
    return pl.pallas_call(
        _proj_gdn_kernel,
        grid=(b, nb),
        in_specs=[pl.BlockSpec((1, tb, d), lambda i, j: (i, j, 0)),
                  pl.BlockSpec((1, 1) + halo.shape[2:], lambda i, j: (i, j, 0, 0)),
                  full((1, d)), full(w_b.shape), full(w_z.shape), full(w_g.shape), full(conv_w.shape),
                  full((1, SLAB)), full((1, SLAB)), full(g512.shape)],
        out_specs=[tok] * 6,
        out_shape=[bf, bf, bf, bf, jax.ShapeDtypeStruct((b, s, MIX_HALF), F32), bf],
        scratch_shapes=[pltpu.VMEM((halo.shape[2] + tb, MIX_HALF), F32)],
        compiler_params=pltpu.CompilerParams(dimension_semantics=("parallel", "parallel"),
                                             vmem_limit_bytes=VMEM_LIMIT),
        name="proj_gdn",
    )(x, halo, gain, w_b, w_z, w_g, conv_w, alog_rep, dtb_rep, g512)


def _attn_kernel(q_ref, k_ref, v_ref, km_ref, vm_ref, lq1_ref, lk1_ref, lq2_ref, lk2_ref, dn_ref, o_ref,
                 qq_scr, m_scr, acc_scr, st_scr, *, qb, cw):
    i = pl.program_id(2)
    n_col = 2 * qb // cw
    per_map = qb // cw
    q = q_ref[0]
    lane = lax.broadcasted_iota(jnp.int32, (qb, SLAB), 1)
    zero = jnp.zeros_like(q)
    qq_scr[...] = jnp.concatenate([jnp.where(lane < HEAD_DIM, q, zero), jnp.where(lane < HEAD_DIM, zero, q)], axis=0)

    def with_ones(v_blk):
        return jnp.concatenate([v_blk, jnp.ones_like(v_blk)], axis=1)

    v1_meta = with_ones(vm_ref[0])
    for c in range(n_col):
        rs = slice(cw * c, cw * (c + 1))
        s = _dot_nt(qq_scr[rs, :], km_ref[0])
        m_new = jnp.max(s, axis=1, keepdims=True)
        acc_scr[rs, :] = _dot(jnp.exp2((s - m_new).astype(BF16)), v1_meta)
        m_scr[rs, :] = jnp.broadcast_to(m_new, (cw, SLAB))

    def scores_into(slot, start, c):
        st_scr[slot] = _dot_nt(qq_scr[cw * c:cw * (c + 1), :], k_ref[0, pl.ds(start, cw), :])

    def softmax_pv(slot, v1, c, masked):
        alphas = []
        for t in range(cw // ROW_STRIP):
            ls = slice(ROW_STRIP * t, ROW_STRIP * (t + 1))
            gs = slice(cw * c + ROW_STRIP * t, cw * c + ROW_STRIP * (t + 1))
            s = st_scr[slot, ls, :]
            if masked:
                q_chunk = (lax.broadcasted_iota(jnp.int32, s.shape, 0) + ROW_STRIP * t) // CHUNK
                k_chunk = lax.broadcasted_iota(jnp.int32, s.shape, 1) // CHUNK
                s = jnp.where(k_chunk <= q_chunk, s, NEG_BIG)
            parts = [s[:, SLAB * k:SLAB * (k + 1)] for k in range(cw // SLAB)]
            lane_max = functools.reduce(jnp.maximum, parts)
            m_prev = m_scr[gs, :]
            m_new = jnp.maximum(m_prev, jnp.max(lane_max, axis=1, keepdims=True))
            for k, part in enumerate(parts):
                st_scr[slot, ls, SLAB * k:SLAB * (k + 1)] = jnp.exp2(part - m_new)
            alphas.append(jnp.exp2(m_prev - m_new))
            m_scr[gs, :] = m_new
        rs = slice(cw * c, cw * (c + 1))
        alpha = jnp.concatenate(alphas, axis=0)
        pv = _dot(st_scr[slot].astype(BF16), v1)
        acc_scr[rs, :] = acc_scr[rs, :] * jnp.concatenate([alpha, alpha], axis=1) + pv

    parity = [0]
    scores_into(0, 0, 0)

    def key_block(start, groups, masked_groups, following):
        v1 = with_ones(v_ref[0, pl.ds(start, cw), :])
        for idx, c in enumerate(groups):
            slot = parity[0]
            if idx + 1 < len(groups):
                scores_into(1 - slot, start, groups[idx + 1])
            elif following is not None:
                scores_into(1 - slot, *following)
            softmax_pv(slot, v1, c, c in masked_groups)
            parity[0] = 1 - slot

    all_groups = list(range(n_col))

    def full_blocks(j, carry):
        for d in range(per_map):
            start = pl.multiple_of((per_map * j + d) * cw, cw)
            key_block(start, all_groups, (), (start + cw, 0))
        return carry

    lax.fori_loop(0, i, full_blocks, 0)
    for d in range(per_map):
        start = pl.multiple_of((per_map * i + d) * cw, cw)
        groups = [c for c in all_groups if c % per_map >= d]
        following = (start + cw, d + 1) if d + 1 < per_map else None
        key_block(start, groups, [c for c in groups if c % per_map == d], following)

    acc = acc_scr[...]
    o1 = acc[:qb, :SLAB] / acc[:qb, SLAB:]
    o2 = acc[qb:, :SLAB] / acc[qb:, SLAB:]
    lam = (jnp.exp(jnp.sum(lq1_ref[...] * lk1_ref[...], axis=1, keepdims=True))
           - jnp.exp(jnp.sum(lq2_ref[...] * lk2_ref[...], axis=1, keepdims=True)) + LAM_INIT)
    o = o1 - lam * o2
    o_ref[0] = (_rms_scale(o) * dn_ref[...] * (1.0 - LAM_INIT)).astype(BF16)


def _diff_attention(qa, ka, va, ka_meta, va_meta, lq1, lk1, lq2, lk2, diff_norm, qb, cw):
    b, s, _ = qa.shape
    nq = s // qb
    vec = lambda n: pl.BlockSpec((1, n), lambda bi, h, i: (0, 0))
    seq = pl.BlockSpec((1, s, SLAB), lambda bi, h, i: (bi, 0, h))
    meta = pl.BlockSpec((1, N_META, SLAB), lambda bi, h, i: (0, 0, h))
    return pl.pallas_call(
        functools.partial(_attn_kernel, qb=qb, cw=cw),
        grid=(b, DA_HEADS, nq),
        in_specs=[pl.BlockSpec((1, qb, SLAB), lambda bi, h, i: (bi, i, h)), seq, seq, meta, meta,
                  vec(HEAD_DIM), vec(HEAD_DIM), vec(HEAD_DIM), vec(HEAD_DIM), vec(SLAB)],
        out_specs=pl.BlockSpec((1, qb, SLAB), lambda bi, h, i: (bi, i, h)),
        out_shape=jax.ShapeDtypeStruct((b, s, MIX_HALF), BF16),
        scratch_shapes=[pltpu.VMEM((2 * qb, SLAB), BF16), pltpu.VMEM((2 * qb, SLAB), F32),
                        pltpu.VMEM((2 * qb, 2 * SLAB), F32), pltpu.VMEM((2, cw, cw), F32)],
        compiler_params=pltpu.CompilerParams(dimension_semantics=("parallel", "parallel", "arbitrary"),
                                             vmem_limit_bytes=VMEM_LIMIT),
        name="diff_attention",
    )(qa, ka, va, ka_meta, va_meta, lq1, lk1, lq2, lk2, diff_norm)


def _gdn_kernel(q_ref, k_ref, v_ref, z_ref, g_ref, beta_ref, gn_ref, g512_ref, s0_ref, o_ref, sfin_ref,
                s_scr, o_scr, *, n_chunks):
    j = pl.program_id(0)
    n_pairs = GDN_HEADS // GDN_GROUP
    n_seq = q_ref.shape[0]

    @pl.when(j == 0)
    def _():
        for bi in range(n_seq):
            s_scr[bi] = s0_ref[...]

    row = lax.broadcasted_iota(jnp.int32, (CHUNK, GDN_GW), 0)
    lane = lax.broadcasted_iota(jnp.int32, (CHUNK, GDN_GW), 1)
    col = lane % CHUNK
    lane_head = lane // HEAD_DIM
    tri_incl = row >= col
    tri_strict = row > col
    eye = (row == col).astype(F32)
    bd_mask = ((lax.broadcasted_iota(jnp.int32, (GDN_GW, GDN_GW), 0) // HEAD_DIM)
               == (lax.broadcasted_iota(jnp.int32, (GDN_GW, GDN_GW), 1) // HEAD_DIM))

    def block_diag(x):
        xb = x.astype(BF16)
        zero = jnp.zeros_like(xb)
        return jnp.concatenate([jnp.where(lane_head == h, xb, zero) for h in range(GDN_GROUP)], axis=0)

    def pair_mm(x, y):
        return _dot(x.astype(BF16), block_diag(y))

    items = [(bi, c, p) for bi in range(n_seq) for c in range(n_chunks) for p in range(n_pairs)]
    rows_of = lambda c: slice(CHUNK * c, CHUNK * (c + 1))
    cols_of = lambda p: slice(GDN_GW * p, GDN_GW * (p + 1))
    load = lambda ref, it: ref[it[0], rows_of(it[1]), cols_of(it[2])]

    tbg = n_chunks * CHUNK
    ri = lax.broadcasted_iota(jnp.int32, (tbg, tbg), 0)
    ci = lax.broadcasted_iota(jnp.int32, (tbg, tbg), 1)
    chunk_tri = ((ri >= ci) & (ri // CHUNK == ci // CHUNK)).astype(BF16)
    g_cum = []
    for bi in range(n_seq):
        rest = g_ref[bi]
        terms = []
        for _ in range(3):
            term = rest.astype(BF16)
            terms.append(term)
            rest = rest - term.astype(F32)
        cum = _dot(chunk_tri, jnp.concatenate(terms, axis=1))
        g_cum.append(cum[:, :MIX_HALF] + cum[:, MIX_HALF:2 * MIX_HALF] + cum[:, 2 * MIX_HALF:])

    g_col, decay, k_f, k_beta, k_bd, q_f, vb = {}, {}, {}, {}, {}, {}, {}
    for it in items:
        g = load(g_ref, it)
        g_col[it] = g_cum[it[0]][rows_of(it[1]), cols_of(it[2])]
        g_row = jnp.sum(jnp.where(row <= col, g, 0.0), axis=0, keepdims=True)
        decay[it] = jnp.where(tri_incl, jnp.exp(jnp.where(tri_incl, g_col[it] - g_row, 0.0)), 0.0)
        beta = load(beta_ref, it).astype(F32)
        k_f[it] = load(k_ref, it).astype(F32)
        q_f[it] = load(q_ref, it).astype(F32)
        k_beta[it] = k_f[it] * beta
        vb[it] = load(v_ref, it).astype(F32) * beta
        k_bd[it] = block_diag(k_f[it])

    a, a_qk = {}, {}
    for it in items:
        lhs = jnp.concatenate([k_beta[it].astype(BF16), q_f[it].astype(BF16)], axis=0)
        kk = _dot_nt(lhs, k_bd[it])
        a[it] = jnp.where(tri_strict, kk[:CHUNK] * decay[it], 0.0)
        a_qk[it] = jnp.where(tri_incl, kk[CHUNK:] * decay[it], 0.0)

    t = {it: eye - a[it] for it in items}
    pw = dict(a)
    for _ in range(5):
        for it in items:
            pw[it] = pair_mm(pw[it], pw[it])
        for it in items:
            t[it] = t[it] + pair_mm(t[it], pw[it])

    u, w, k_g, q_g, e_last = {}, {}, {}, {}, {}
    for it in items:
        e_col = jnp.exp(g_col[it])
        rhs = jnp.concatenate([block_diag(vb[it]), block_diag(k_beta[it] * e_col)], axis=1)
        uw = _dot(t[it].astype(BF16), rhs)
        u[it], w[it] = uw[:, :GDN_GW], uw[:, GDN_GW:]
        g_last = g_col[it][CHUNK - 1:CHUNK, :]
        k_g[it] = (k_f[it] * jnp.exp(g_last - g_col[it])).astype(BF16)
        q_g[it] = q_f[it] * e_col
        e_last[it] = jnp.exp(g_last)

    state = {(bi, p): s_scr[bi, p] for bi in range(n_seq) for p in range(n_pairs)}
    sp = lambda it: (it[0], it[2])
    for c in range(n_chunks):
        its = [(bi, c, p) for bi in range(n_seq) for p in range(n_pairs)]
        ws_qs = [_dot(jnp.concatenate([w[it], q_g[it]], axis=0).astype(BF16), state[sp(it)].astype(BF16))
                 for it in its]
        v_new = [u[it] - sq[:CHUNK] for it, sq in zip(its, ws_qs)]
        for it, sq, vn in zip(its, ws_qs, v_new):
            o_scr[it[0], rows_of(c), cols_of(it[2])] = sq[CHUNK:] + pair_mm(a_qk[it], vn)
        for it, vn in zip(its, v_new):
            upd = _dot_tn(k_g[it], vn.astype(BF16))
            state[sp(it)] = state[sp(it)] * e_last[it] + jnp.where(bd_mask, upd, 0.0)
    for (bi, p), val in state.items():
        s_scr[bi, p] = val

    for bi in range(n_seq):
        o = o_scr[bi]
        ss = _group_sumsq(o, g512_ref)
        y = o * lax.rsqrt(ss * (1.0 / HEAD_DIM) + EPS) * gn_ref[...]
        o_ref[bi] = (y * z_ref[bi].astype(F32)).astype(BF16)

    @pl.when(j == pl.num_programs(0) - 1)
    def _():
        sfin_ref[...] = s_scr[...]


def _gated_delta(q, k, v, z, gb, beta, gn_rep, g512, s0, tbg):
    b, s, _ = q.shape
    nb = s // tbg
    state_shape = (b,) + s0.shape
    tok = pl.BlockSpec((b, tbg, MIX_HALF), lambda j: (0, j, 0))
    full = lambda shape: pl.BlockSpec(shape, lambda j: (0,) * len(shape))
    return pl.pallas_call(
        functools.partial(_gdn_kernel, n_chunks=tbg // CHUNK),
        grid=(nb,),
        in_specs=[tok] * 6 + [full((1, MIX_HALF)), full(g512.shape), full(s0.shape)],
        out_specs=[tok, pl.BlockSpec(state_shape, lambda j: (0, 0, 0, 0))],
        out_shape=[jax.ShapeDtypeStruct((b, s, MIX_HALF), BF16), jax.ShapeDtypeStruct(state_shape, F32)],
        scratch_shapes=[pltpu.VMEM(state_shape, F32), pltpu.VMEM((b, tbg, MIX_HALF), F32)],
        compiler_params=pltpu.CompilerParams(dimension_semantics=("arbitrary",),
                                             vmem_limit_bytes=VMEM_LIMIT),
        name="gated_delta",
    )(q, k, v, z, gb, beta, gn_rep, g512, s0)


def _out_router_kernel(x_ref, oa_ref, ob_ref, wo_ref, gain_ref, wr_ref, br_ref, before_ref,
                       h_ref, t_ref, rows_ref, cols_ref, cnt_ref):
    h = x_ref[0] + _dot(jnp.concatenate([oa_ref[0], ob_ref[0]], axis=1), wo_ref[...])
    h_ref[0] = h
    t = _rms_scale(h) * gain_ref[...]
    t_hi = t.astype(BF16)
    t_ref[0] = t_hi
    t_lo = (t - t_hi.astype(F32)).astype(BF16)
    wr = wr_ref[...]
    hi_part = _dot(t_hi, wr)
    logits = hi_part[:, :ROUTE_LANES] + hi_part[:, ROUTE_LANES:] + _dot(t_lo, wr)[:, :ROUTE_LANES] + br_ref[...]

    lt = logits.T
    tb = lt.shape[1]
    row8 = lax.broadcasted_iota(jnp.int32, (EXPERTS_PER_GROUP, tb), 0)
    col_max = lambda v: jnp.max(v, axis=0, keepdims=True)
    col_sum = lambda v: jnp.sum(v, axis=0, keepdims=True)
    first_argmax = lambda v, vmax: jnp.min(jnp.where(v == vmax, row8, EXPERTS_PER_GROUP), axis=0, keepdims=True)

    gl = jnp.where(row8 < N_GROUPS, lt[:EXPERTS_PER_GROUP], NEG_BIG)
    gmax = col_max(gl)
    gsel = first_argmax(gl, gmax)
    psel = 1.0 / col_sum(jnp.exp(gl - gmax))
    el = jnp.zeros((EXPERTS_PER_GROUP, tb), F32)
    for g in range(N_GROUPS):
        lo = EXPERT_ROW0 + EXPERTS_PER_GROUP * g
        el = jnp.where(gsel == g, lt[lo:lo + EXPERTS_PER_GROUP], el)
    m1 = col_max(el)
    i1 = first_argmax(el, m1)
    el2 = jnp.where(row8 == i1, NEG_BIG, el)
    m2 = col_max(el2)
    i2 = first_argmax(el2, m2)
    denom = col_sum(jnp.exp(el - m1))
    p1 = 1.0 / denom
    p2 = jnp.exp(m2 - m1) / denom
    w1 = p1 / (p1 + p2) * psel
    w2 = p2 / (p1 + p2) * psel
    e1 = gsel * EXPERTS_PER_GROUP + i1
    e2 = gsel * EXPERTS_PER_GROUP + i2

    row_e = lax.broadcasted_iota(jnp.int32, (N_EXPERTS, tb), 0)
    hot1 = (row_e == e1).astype(F32)
    hot2 = (row_e == e2).astype(F32)
    both = hot1 + hot2
    n_gran = jnp.ceil(jnp.sum(both, axis=1, keepdims=True) * (1.0 / SLOT_GRAN)) * SLOT_GRAN
    ei = lax.broadcasted_iota(jnp.int32, (N_EXPERTS, N_EXPERTS), 0)
    ej = lax.broadcasted_iota(jnp.int32, (N_EXPERTS, N_EXPERTS), 1)
    seg_start = _dot((ei > ej).astype(F32), jnp.broadcast_to(n_gran, (N_EXPERTS, SLAB)), precision=HIGHEST)[:, :1]
    earlier = _dot(both.astype(BF16), before_ref[...])
    where_to = earlier + seg_start
    pos1 = col_sum(hot1 * where_to)
    pos2 = col_sum(hot2 * where_to)

    info = jnp.concatenate([pos1, pos2, w1, w2, e1.astype(F32), e2.astype(F32),
                            jnp.zeros((SLAB - 6, tb), F32)], axis=0)
    rows_ref[0] = info[:8]
    cols_ref[0] = info.T
    cnt_ref[0] = jnp.broadcast_to(n_gran, (N_EXPERTS, SLAB))


def _out_router(x, oa, ob, w_o, gain, w_route, b_route, before, tb):
    b, s, d = x.shape
    nb = s // tb
    full = lambda shape: pl.BlockSpec(shape, lambda i, j: (0,) * len(shape))
    tokd = pl.BlockSpec((1, tb, d), lambda i, j: (i, j, 0))
    tokh = pl.BlockSpec((1, tb, MIX_HALF), lambda i, j: (i, j, 0))
    return pl.pallas_call(
        _out_router_kernel,
        grid=(b, nb),
        in_specs=[tokd, tokh, tokh, full(w_o.shape), full((1, d)), full(w_route.shape), full((1, ROUTE_LANES)),
                  full(before.shape)],
        out_specs=[tokd, tokd, pl.BlockSpec((1, 8, tb), lambda i, j: (i, 0, j)),
                   pl.BlockSpec((1, tb, SLAB), lambda i, j: (i, j, 0)),
                   pl.BlockSpec((1, N_EXPERTS, SLAB), lambda i, j: (i * nb + j, 0, 0))],
        out_shape=[jax.ShapeDtypeStruct((b, s, d), F32), jax.ShapeDtypeStruct((b, s, d), BF16),
                   jax.ShapeDtypeStruct((b, 8, s), F32), jax.ShapeDtypeStruct((b, s, SLAB), F32),
                   jax.ShapeDtypeStruct((b * nb, N_EXPERTS, SLAB), F32)],
        compiler_params=pltpu.CompilerParams(dimension_semantics=("parallel", "parallel"),
                                             vmem_limit_bytes=VMEM_LIMIT),
        name="out_router",
    )(x, oa, ob, w_o, gain, w_route, b_route, before)


def _granule_copies(table_ref, block, buf, local_ref, global_ref, sem, to_global, n_gran):
    copies = []
    for g in range(n_gran):
        loc = local_ref.at[buf, pl.ds(g * SLOT_GRAN, SLOT_GRAN)]
        glob = global_ref.at[pl.ds(pl.multiple_of(table_ref[block, g], SLOT_GRAN), SLOT_GRAN)]
        copies.append(pltpu.make_async_copy(loc, glob, sem.at[buf]) if to_global
                      else pltpu.make_async_copy(glob, loc, sem.at[buf]))
    return copies


def _dispatch_kernel(gdst_ref, t_ref, rows_ref, xg_init_ref, xg_ref, xs_scr, sem, *, n_slots):
    del xg_init_ref
    blk = pl.program_id(0)
    last = pl.num_programs(0) - 1
    buf = blk % 2
    tb = t_ref.shape[0]
    n_gran = n_slots // SLOT_GRAN
    info = rows_ref[0]
    slot = lax.broadcasted_iota(jnp.int32, (n_slots, tb), 0).astype(F32)
    onehot = jnp.where((slot == info[0:1]) | (slot == info[1:2]), 1.0, 0.0).astype(BF16)
    xs_scr[buf] = _dot(onehot, t_ref[...]).astype(BF16)
    for copy in _granule_copies(gdst_ref, blk, buf, xs_scr, xg_ref, sem, True, n_gran):
        copy.start()

    @pl.when(blk > 0)
    def _():
        for copy in _granule_copies(gdst_ref, blk - 1, 1 - buf, xs_scr, xg_ref, sem, True, n_gran):
            copy.wait()

    @pl.when(blk == last)
    def _():
        for copy in _granule_copies(gdst_ref, blk, buf, xs_scr, xg_ref, sem, True, n_gran):
            copy.wait()


def _dispatch(granule_dst, t, rows, xg_init, tb, n_slots):
    n, d = t.shape
    nblk = n // tb
    nb = rows.shape[2] // tb
    grid_spec = pltpu.PrefetchScalarGridSpec(
        num_scalar_prefetch=1,
        grid=(nblk,),
        in_specs=[pl.BlockSpec((tb, d), lambda i, gd: (i, 0)),
                  pl.BlockSpec((1, 8, tb), lambda i, gd: (i // nb, 0, i % nb)),
                  pl.BlockSpec(memory_space=pl.ANY)],
        out_specs=pl.BlockSpec(memory_space=pl.ANY),
        scratch_shapes=[pltpu.VMEM((2, n_slots, d), BF16), pltpu.SemaphoreType.DMA((2,))],
    )
    return pl.pallas_call(
        functools.partial(_dispatch_kernel, n_slots=n_slots),
        grid_spec=grid_spec,
        out_shape=jax.ShapeDtypeStruct(xg_init.shape, BF16),
        input_output_aliases={3: 0},
        compiler_params=pltpu.CompilerParams(dimension_semantics=("arbitrary",), vmem_limit_bytes=VMEM_LIMIT),
        name="moe_dispatch",
    )(granule_dst, t, rows, xg_init)


def _experts_kernel(tile_expert_ref, n_tiles_ref, x_ref, wg_ref, wu_ref, wd_ref, y_ref):
    del tile_expert_ref

    @pl.when(pl.program_id(0) < n_tiles_ref[0])
    def _():
        x = x_ref[...]
        act = _silu(_dot(x, wg_ref[0].astype(BF16))) * _dot(x, wu_ref[0].astype(BF16))
        y_ref[...] = _dot(act.astype(BF16), wd_ref[0].astype(BF16)).astype(BF16)


def _experts(tile_expert, n_tiles, xg, w_gate, w_up, w_down, tile):
    rows, d = xg.shape
    used = lambda i, te, nt: jnp.minimum(i, nt[0] - 1)
    wspec = lambda shape: pl.BlockSpec((1,) + shape, lambda i, te, nt: (te[used(i, te, nt)], 0, 0))
    grid_spec = pltpu.PrefetchScalarGridSpec(
        num_scalar_prefetch=2,
        grid=(tile_expert.shape[0],),
        in_specs=[pl.BlockSpec((tile, d), lambda i, te, nt: (used(i, te, nt), 0)),
                  wspec((d, D_EXPERT)), wspec((d, D_EXPERT)), wspec((D_EXPERT, d))],
        out_specs=pl.BlockSpec((tile, d), lambda i, te, nt: (used(i, te, nt), 0)),
    )
    return pl.pallas_call(
        _experts_kernel,
        grid_spec=grid_spec,
        out_shape=jax.ShapeDtypeStruct((rows, d), BF16),
        input_output_aliases={2: 0},
        compiler_params=pltpu.CompilerParams(dimension_semantics=("arbitrary",), vmem_limit_bytes=VMEM_LIMIT),
        name="moe_experts",
    )(tile_expert, n_tiles, xg, w_gate, w_up, w_down)


def _combine_kernel(gsrc_ref, yg_ref, cols_ref, h_ref, gain_ref, o_ref, ys_scr, sem, *, n_slots):
    blk = pl.program_id(0)
    last = pl.num_programs(0) - 1
    buf = blk % 2
    tb = h_ref.shape[0]
    n_gran = n_slots // SLOT_GRAN

    @pl.when(blk == 0)
    def _():
        for copy in _granule_copies(gsrc_ref, 0, 0, ys_scr, yg_ref, sem, False, n_gran):
            copy.start()

    @pl.when(blk < last)
    def _():
        for copy in _granule_copies(gsrc_ref, blk + 1, 1 - buf, ys_scr, yg_ref, sem, False, n_gran):
            copy.start()

    info = cols_ref[...]
    slot = lax.broadcasted_iota(jnp.int32, (tb, n_slots), 1).astype(F32)
    weights = (jnp.where(slot == info[:, 0:1], info[:, 2:3], 0.0)
               + jnp.where(slot == info[:, 1:2], info[:, 3:4], 0.0)).astype(BF16)
    for copy in _granule_copies(gsrc_ref, blk, buf, ys_scr, yg_ref, sem, False, n_gran):
        copy.wait()
    y = _dot(weights, ys_scr[buf])
    o_ref[...] = _rms_scale(h_ref[...] + y) * gain_ref[...]


def _combine(granule_dst, yg, cols, h, gain, tb, n_slots):
    n, d = h.shape
    grid_spec = pltpu.PrefetchScalarGridSpec(
        num_scalar_prefetch=1,
        grid=(n // tb,),
        in_specs=[pl.BlockSpec(memory_space=pl.ANY),
                  pl.BlockSpec((tb, SLAB), lambda i, gd: (i, 0)),
                  pl.BlockSpec((tb, d), lambda i, gd: (i, 0)),
                  pl.BlockSpec((1, d), lambda i, gd: (0, 0))],
        out_specs=pl.BlockSpec((tb, d), lambda i, gd: (i, 0)),
        scratch_shapes=[pltpu.VMEM((2, n_slots, d), BF16), pltpu.SemaphoreType.DMA((2,))],
    )
    return pl.pallas_call(
        functools.partial(_combine_kernel, n_slots=n_slots),
        grid_spec=grid_spec,
        out_shape=jax.ShapeDtypeStruct((n, d), F32),
        compiler_params=pltpu.CompilerParams(dimension_semantics=("arbitrary",), vmem_limit_bytes=VMEM_LIMIT),
        name="moe_combine",
    )(granule_dst, yg, cols, h, gain)


def _regroup_plan(n_gran_be, n_slots, tile, n_tiles_max):
    cnt = n_gran_be.astype(jnp.int32)
    region = (jnp.sum(cnt, axis=0) + tile - 1) // tile * tile
    region_end = jnp.cumsum(region)
    first_row = (region_end - region)[None, :] + jnp.cumsum(cnt, axis=0) - cnt
    seg_end = jnp.cumsum(cnt, axis=1)
    seg_start = seg_end - cnt
    g_row = jnp.arange(n_slots // SLOT_GRAN, dtype=jnp.int32) * SLOT_GRAN
    owned = ((g_row[None, :, None] >= seg_start[:, None, :]) & (g_row[None, :, None] < seg_end[:, None, :]))
    in_use = g_row[None, :] < seg_end[:, -1:]
    row = g_row[None, :] + jnp.sum(jnp.where(owned, (first_row - seg_start)[:, None, :], 0), axis=-1)
    zero_row = n_tiles_max * tile
    spill_row = zero_row + SLOT_GRAN + (jnp.arange(cnt.shape[0], dtype=jnp.int32) % 2)[:, None] * n_slots + g_row
    dispatch_dst = jnp.where(in_use, row, spill_row).astype(jnp.int32)
    combine_src = jnp.where(in_use, row, zero_row).astype(jnp.int32)
    tile_row = jnp.arange(n_tiles_max, dtype=jnp.int32) * tile
    tile_expert = jnp.minimum(jnp.sum(tile_row[:, None] >= region_end[None, :], axis=-1), N_EXPERTS - 1)
    n_tiles = (region_end[-1] // tile).astype(jnp.int32)[None]
    return dispatch_dst, combine_src, tile_expert.astype(jnp.int32), n_tiles


def _rope_tables(length):
    pos = jnp.arange(length, dtype=F32)
    inv_freq = ROPE_THETA ** (-jnp.arange(0, HEAD_DIM, 2, dtype=F32) / HEAD_DIM)
    ang = pos[:, None] * inv_freq[None, :]
    cos, sin = jnp.cos(ang), jnp.sin(ang)
    zero = jnp.zeros_like(sin)
    cos_t = jnp.tile(cos, (1, 4))
    sina_t = jnp.tile(jnp.concatenate([-sin, zero], axis=1), (1, 2))
    sinb_t = jnp.tile(jnp.concatenate([zero, sin], axis=1), (1, 2))
    return cos_t, sina_t, sinb_t


def _block(total, want):
    blk = min(total, want)
    assert total % blk == 0, (total, blk)
    return blk


def kernel(x, meta, norm_mix, w_in, lambda_q1, lambda_k1, lambda_q2, lambda_k2, diff_norm, conv_w, a_log, dt_bias,
           gdn_norm, w_out, norm_ffn, w_group, b_group, w_router, b_router, w_gate, w_up, w_down, norm_final):
    b, s, d = x.shape
    assert d == D_MODEL and meta.shape == (N_META, D_MODEL) and s % CHUNK == 0
    assert norm_mix.shape[0] == 1, "single-layer block"
    l = 0

    w = w_in[l]
    w_a = w[:, :3 * MIX_HALF].astype(BF16)
    w_b = w[:, 3 * MIX_HALF:6 * MIX_HALF].astype(BF16)
    w_z = w[:, 6 * MIX_HALF:7 * MIX_HALF].astype(BF16)
    c_ab = 7 * MIX_HALF
    lane_pad = lambda a: jnp.pad(a, [(0, 0)] * (a.ndim - 1) + [(0, SLAB - a.shape[-1])])
    w_g = lane_pad(w[:, c_ab:c_ab + 2 * GDN_HEADS]).astype(BF16)
    alog_rep = lane_pad(a_log[l])[None]
    dtb_rep = lane_pad(dt_bias[l])[None]
    gn_rep = jnp.tile(gdn_norm[l], GDN_HEADS)[None]
    g512 = (jnp.arange(MIX_HALF)[:, None] // HEAD_DIM == jnp.arange(MIX_HALF)[None, :] // HEAD_DIM).astype(BF16)
    gain_mix = norm_mix[l][None]
    w_o = w_out[l].astype(BF16)
    route_pad = lambda g, e: jnp.concatenate(
        [g, jnp.zeros(g.shape[:-1] + (EXPERT_ROW0 - N_GROUPS,), F32), e,
         jnp.zeros(g.shape[:-1] + (ROUTE_LANES - EXPERT_ROW0 - N_EXPERTS,), F32)], axis=-1)
    w_route_f = route_pad(w_group[l], w_router[l])
    w_route_hi = w_route_f.astype(BF16)
    w_route = jnp.concatenate([w_route_hi, (w_route_f - w_route_hi.astype(F32)).astype(BF16)], axis=1)
    b_route = route_pad(b_group[l], b_router[l])[None]
    lam_vecs = [v[l][None] for v in (lambda_q1, lambda_k1, lambda_q2, lambda_k2)]
    cos_t, sina_t, sinb_t = _rope_tables(N_META + s)

    meta3 = meta[None]
    _, ka_m, va_m = _proj_attn(meta3, gain_mix, w_a, cos_t[:N_META], sina_t[:N_META], sinb_t[:N_META], N_META)
    zero_halo = jnp.zeros((1, 1, N_META, d), F32)
    gdn_m = _proj_gdn(meta3, zero_halo, gain_mix, w_b, w_z, w_g, conv_w[l], alog_rep, dtb_rep, g512, N_META)
    pad_m = lambda a: jnp.pad(a, ((0, 0), (0, CHUNK - N_META), (0, 0)))
    s_zero = jnp.zeros((GDN_HEADS // GDN_GROUP, GDN_GW, GDN_GW), F32)
    _, s_meta = _gated_delta(*[pad_m(a) for a in gdn_m], gn_rep, g512, s_zero, CHUNK)

    tb_a = _block(s, PROJ_ATTN_ROWS)
    qa, ka, va = _proj_attn(x, gain_mix, w_a, cos_t[N_META:], sina_t[N_META:], sinb_t[N_META:], tb_a)
    tb_g = _block(s, PROJ_GDN_ROWS)
    nb_g = s // tb_g
    tails = x.reshape(b, nb_g, tb_g, d)[:, :-1, tb_g - N_META:, :]
    halo = jnp.concatenate([jnp.broadcast_to(meta[None, None], (b, 1, N_META, d)), tails], axis=1)
    gdn_f = _proj_gdn(x, halo, gain_mix, w_b, w_z, w_g, conv_w[l], alog_rep, dtb_rep, g512, tb_g)

    oa = _diff_attention(qa, ka, va, ka_m, va_m, *lam_vecs, diff_norm[l][None],
                         _block(s, ATTN_QUERY_ROWS), ATTN_KEY_BLOCK)
    ob, _ = _gated_delta(*gdn_f, gn_rep, g512, s_meta[0], _block(s, GDN_ROWS))

    tb_r = _block(s, MOE_BLOCK)
    tok = jnp.arange(tb_r)
    before = (tok[:, None] < tok[None, :]).astype(BF16)
    h1, t, rows, cols, cnt = _out_router(x, oa, ob, w_o, norm_ffn[l][None], w_route, b_route, before, tb_r)

    n = b * s
    nblk = n // tb_r
    n_slots = 2 * tb_r + N_EXPERTS * SLOT_GRAN
    rows_max = 2 * n + (SLOT_GRAN - 1) * N_EXPERTS * nblk + (MOE_TILE - 1) * N_EXPERTS
    n_tiles_max = -(-rows_max // MOE_TILE)
    dispatch_dst, combine_src, tile_expert, n_tiles = _regroup_plan(cnt[:, :, 0], n_slots, MOE_TILE, n_tiles_max)
    buf_rows = n_tiles_max * MOE_TILE + SLOT_GRAN + 2 * n_slots
    xg = _dispatch(dispatch_dst, t.reshape(n, d), rows, jnp.zeros((buf_rows, d), BF16), tb_r, n_slots)
    yg = _experts(tile_expert, n_tiles, xg, w_gate[l], w_up[l], w_down[l], MOE_TILE)
    out = _combine(combine_src, yg, cols.reshape(n, SLAB), h1.reshape(n, d), norm_final[None], tb_r, n_slots)
    return out.reshape(b, s, d)
```

```python
import functools
import math

import jax
import jax.numpy as jnp
from jax import lax
from jax.experimental import pallas as pl
from jax.experimental.pallas import tpu as pltpu

F32 = jnp.float32
BF16 = jnp.bfloat16
HIGHEST = lax.Precision.HIGHEST

D_MODEL = 1024
N_META = 16
CHUNK = 64
EPS = 1e-6
ROPE_THETA = 10000.0
HEAD_DIM = 64
SLAB = 128
DA_HEADS = 4
GDN_HEADS = 8
MIX_HALF = 512
GDN_GROUP = 2
GDN_GW = GDN_GROUP * HEAD_DIM
N_GROUPS = 4
EXPERTS_PER_GROUP = 8
N_EXPERTS = 32
D_EXPERT = 256
CONV_K = 4
LAM_INIT = 0.8 - 0.6 * math.exp(-0.3 * 0)
ROUTE_LANES = 128
EXPERT_ROW0 = 8
SLOT_GRAN = 16
MOE_BLOCK = 512
MOE_TILE = 512
ROW_STRIP = 64
PROJ_ATTN_ROWS = 1024
PROJ_GDN_ROWS = 512
ATTN_QUERY_ROWS = 2048
ATTN_KEY_BLOCK = 512
GDN_ROWS = 256
NEG_BIG = -1e30
LOG2E = math.log2(math.e)
VMEM_LIMIT = 56 * 1024 * 1024


def _dot(a, b, precision=None):
    return jnp.dot(a, b, preferred_element_type=F32, precision=precision)


def _dot_nt(a, b):
    return lax.dot_general(a, b, (((1,), (1,)), ((), ())), preferred_element_type=F32)


def _dot_tn(a, b):
    return lax.dot_general(a, b, (((0,), (0,)), ((), ())), preferred_element_type=F32)


def _rms_scale(x):
    return x * lax.rsqrt(jnp.mean(x * x, axis=-1, keepdims=True) + EPS)


def _silu(x):
    return x * jax.nn.sigmoid(x)


def _group_sumsq(x, g_ref):
    return _dot((x * x).astype(BF16), g_ref[...])


def _proj_attn_kernel(x_ref, gain_ref, w_ref, cos_ref, sina_ref, sinb_ref, qa_ref, ka_ref, va_ref):
    u = (_rms_scale(x_ref[0]) * gain_ref[...]).astype(BF16)
    proj = _dot(u, w_ref[...])
    va_ref[0] = proj[:, 2 * MIX_HALF:].astype(BF16)
    cos, sina, sinb = cos_ref[...], sina_ref[...], sinb_ref[...]
    for s in range(2 * DA_HEADS):
        xs = proj[:, SLAB * s:SLAB * (s + 1)]
        r = xs * cos + pltpu.roll(xs, SLAB - 32, 1) * sina + pltpu.roll(xs, 32, 1) * sinb
        if s < DA_HEADS:
            qa_ref[0, :, SLAB * s:SLAB * (s + 1)] = (r * (HEAD_DIM ** -0.5 * LOG2E)).astype(BF16)
        else:
            t = s - DA_HEADS
            ka_ref[0, :, SLAB * t:SLAB * (t + 1)] = r.astype(BF16)


def _proj_attn(x, gain, w_a, cos, sina, sinb, tb):
    b, s, d = x.shape
    nb = s // tb
    slab_out = jax.ShapeDtypeStruct((b, s, MIX_HALF), BF16)
    full = lambda shape: pl.BlockSpec(shape, lambda i, j: (0,) * len(shape))
    tok = pl.BlockSpec((1, tb, MIX_HALF), lambda i, j: (i, j, 0))
    tab = pl.BlockSpec((tb, SLAB), lambda i, j: (j, 0))
    return pl.pallas_call(
        _proj_attn_kernel,
        grid=(b, nb),
        in_specs=[pl.BlockSpec((1, tb, d), lambda i, j: (i, j, 0)), full((1, d)), full(w_a.shape), tab, tab, tab],
        out_specs=[tok, tok, tok],
        out_shape=[slab_out, slab_out, slab_out],
        compiler_params=pltpu.CompilerParams(dimension_semantics=("parallel", "parallel"),
                                             vmem_limit_bytes=VMEM_LIMIT),
        name="proj_attn",
    )(x, gain, w_a, cos, sina, sinb)


def _proj_gdn_kernel(x_ref, halo_ref, gain_ref, wb_ref, wz_ref, wg_ref, convw_ref, alog_ref, dtb_ref, g_ref,
                     q_ref, k_ref, v_ref, z_ref, gb_ref, beta_ref, pb_scr):
    halo_rows = halo_ref.shape[2]
    xe = jnp.concatenate([halo_ref[0, 0], x_ref[0]], axis=0)
    ue = (_rms_scale(xe) * gain_ref[...]).astype(BF16)
    u = ue[halo_rows:]
    cw = convw_ref[...]
    outs = (q_ref, k_ref, v_ref)
    for part in range(3):
        cols = slice(MIX_HALF * part, MIX_HALF * (part + 1))
        pb_scr[...] = _dot(ue, wb_ref[:, cols])
        c = cw[:, cols]
        tb = pb_scr.shape[0] - halo_rows
        y = pb_scr[halo_rows:, :] * c[CONV_K - 1:CONV_K]
        for back in range(1, CONV_K):
            tap = CONV_K - 1 - back
            y = y + pb_scr[pl.ds(halo_rows - back, tb), :] * c[tap:tap + 1]
        y = _silu(y)
        if part < 2:
            y = y * lax.rsqrt(_group_sumsq(y, g_ref) + EPS)
            if part == 0:
                y = y * (HEAD_DIM ** -0.5)
        outs[part][0] = y.astype(BF16)
    z_ref[0] = _silu(_dot(u, wz_ref[...])).astype(BF16)
    gates = _dot(u, wg_ref[...])
    ab = gates + dtb_ref[...]
    softplus = jnp.maximum(ab, 0.0) + jnp.log1p(jnp.exp(-jnp.abs(ab)))
    g = -jnp.exp(alog_ref[...]) * softplus
    beta = jax.nn.sigmoid(gates)
    tb = g.shape[0]
    first_head = lax.broadcasted_iota(jnp.int32, (tb, SLAB), 1) < HEAD_DIM
    spread = lambda v, h: jnp.broadcast_to(v[:, h:h + 1], (tb, SLAB))
    for p in range(GDN_HEADS // 2):
        cols = slice(SLAB * p, SLAB * (p + 1))
        gb_ref[0, :, cols] = jnp.where(first_head, spread(g, 2 * p), spread(g, 2 * p + 1))
        beta_ref[0, :, cols] = jnp.where(first_head, spread(beta, GDN_HEADS + 2 * p),
                                         spread(beta, GDN_HEADS + 2 * p + 1)).astype(BF16)


def _proj_gdn(x, halo, gain, w_b, w_z, w_g, conv_w, alog_rep, dtb_rep, g512, tb):
    b, s, d = x.shape
    nb = s // tb
    full = lambda shape: pl.BlockSpec(shape, lambda i, j: (0,) * len(shape))
    tok = pl.BlockSpec((1, tb, MIX_HALF), lambda i, j: (i, j, 0))
    bf = jax.ShapeDtypeStruct((b, s, MIX_HALF), BF16)
    return pl.pallas_call(
        _proj_gdn_kernel,
        grid=(b, nb),
        in_specs=[pl.BlockSpec((1, tb, d), lambda i, j: (i, j, 0)),
                  pl.BlockSpec((1, 1) + halo.shape[2:], lambda i, j: (i, j, 0, 0)),
                  full((1, d)), full(w_b.shape), full(w_z.shape), full(w_g.shape), full(conv_w.shape),
                  full((1, SLAB)), full((1, SLAB)), full(g512.shape)],
        out_specs=[tok] * 6,
        out_shape=[bf, bf, bf, bf, jax.ShapeDtypeStruct((b, s, MIX_HALF), F32), bf],
        scratch_shapes=[pltpu.VMEM((halo.shape[2] + tb, MIX_HALF), F32)],
        compiler_params=pltpu.CompilerParams(dimension_semantics=("parallel", "parallel"),
                                             vmem_limit_bytes=VMEM_LIMIT),
        name="proj_gdn",
    )(x, halo, gain, w_b, w_z, w_g, conv_w, alog_rep, dtb_rep, g512)


def _attn_kernel(q_ref, k_ref, v_ref, km_ref, vm_ref, lq1_ref, lk1_ref, lq2_ref, lk2_ref, dn_ref, o_ref,
                 qq_scr, m_scr, acc_scr, st_scr, *, qb, cw):
    i = pl.program_id(2)
    n_col = 2 * qb // cw
    per_map = qb // cw
    q = q_ref[0]
    lane = lax.broadcasted_iota(jnp.int32, (qb, SLAB), 1)
    zero = jnp.zeros_like(q)
    qq_scr[...] = jnp.concatenate([jnp.where(lane < HEAD_DIM, q, zero), jnp.where(lane < HEAD_DIM, zero, q)], axis=0)

    def with_ones(v_blk):
        return jnp.concatenate([v_blk, jnp.ones_like(v_blk)], axis=1)

    v1_meta = with_ones(vm_ref[0])
    for c in range(n_col):
        rs = slice(cw * c, cw * (c + 1))
        s = _dot_nt(qq_scr[rs, :], km_ref[0])
        m_new = jnp.max(s, axis=1, keepdims=True)
        acc_scr[rs, :] = _dot(jnp.exp2((s - m_new).astype(BF16)), v1_meta)
        m_scr[rs, :] = jnp.broadcast_to(m_new, (cw, SLAB))

    def scores_into(slot, start, c):
        st_scr[slot] = _dot_nt(qq_scr[cw * c:cw * (c + 1), :], k_ref[0, pl.ds(start, cw), :])

    def softmax_pv(slot, v1, c, masked):
        alphas = []
        for t in range(cw // ROW_STRIP):
            ls = slice(ROW_STRIP * t, ROW_STRIP * (t + 1))
            gs = slice(cw * c + ROW_STRIP * t, cw * c + ROW_STRIP * (t + 1))
            s = st_scr[slot, ls, :]
            if masked:
                q_chunk = (lax.broadcasted_iota(jnp.int32, s.shape, 0) + ROW_STRIP * t) // CHUNK
                k_chunk = lax.broadcasted_iota(jnp.int32, s.shape, 1) // CHUNK
                s = jnp.where(k_chunk <= q_chunk, s, NEG_BIG)
            parts = [s[:, SLAB * k:SLAB * (k + 1)] for k in range(cw // SLAB)]
            lane_max = functools.reduce(jnp.maximum, parts)
            m_prev = m_scr[gs, :]
            m_new = jnp.maximum(m_prev, jnp.max(lane_max, axis=1, keepdims=True))
            for k, part in enumerate(parts):
                st_scr[slot, ls, SLAB * k:SLAB * (k + 1)] = jnp.exp2(part - m_new)
            alphas.append(jnp.exp2(m_prev - m_new))
            m_scr[gs, :] = m_new
        rs = slice(cw * c, cw * (c + 1))
        alpha = jnp.concatenate(alphas, axis=0)
        pv = _dot(st_scr[slot].astype(BF16), v1)
        acc_scr[rs, :] = acc_scr[rs, :] * jnp.concatenate([alpha, alpha], axis=1) + pv

    parity = [0]
    scores_into(0, 0, 0)

    def key_block(start, groups, masked_groups, following):
        v1 = with_ones(v_ref[0, pl.ds(start, cw), :])
        for idx, c in enumerate(groups):
            slot = parity[0]
            if idx + 1 < len(groups):
                scores_into(1 - slot, start, groups[idx + 1])
            elif following is not None:
                scores_into(1 - slot, *following)
            softmax_pv(slot, v1, c, c in masked_groups)
            parity[0] = 1 - slot

    all_groups = list(range(n_col))

    def full_blocks(j, carry):
        for d in range(per_map):
            start = pl.multiple_of((per_map * j + d) * cw, cw)
            key_block(start, all_groups, (), (start + cw, 0))
        return carry

    lax.fori_loop(0, i, full_blocks, 0)
    for d in range(per_map):
        start = pl.multiple_of((per_map * i + d) * cw, cw)
        groups = [c for c in all_groups if c % per_map >= d]
        following = (start + cw, d + 1) if d + 1 < per_map else None
        key_block(start, groups, [c for c in groups if c % per_map == d], following)

    acc = acc_scr[...]
    o1 = acc[:qb, :SLAB] / acc[:qb, SLAB:]
    o2 = acc[qb:, :SLAB] / acc[qb:, SLAB:]
    lam = (jnp.exp(jnp.sum(lq1_ref[...] * lk1_ref[...], axis=1, keepdims=True))
           - jnp.exp(jnp.sum(lq2_ref[...] * lk2_ref[...], axis=1, keepdims=True)) + LAM_INIT)
    o = o1 - lam * o2
    o_ref[0] = (_rms_scale(o) * dn_ref[...] * (1.0 - LAM_INIT)).astype(BF16)


def _diff_attention(qa, ka, va, ka_meta, va_meta, lq1, lk1, lq2, lk2, diff_norm, qb, cw):
    b, s, _ = qa.shape
    nq = s // qb
    vec = lambda n: pl.BlockSpec((1, n), lambda bi, h, i: (0, 0))
    seq = pl.BlockSpec((1, s, SLAB), lambda bi, h, i: (bi, 0, h))
    meta = pl.BlockSpec((1, N_META, SLAB), lambda bi, h, i: (0, 0, h))
    return pl.pallas_call(
        functools.partial(_attn_kernel, qb=qb, cw=cw),
        grid=(b, DA_HEADS, nq),
        in_specs=[pl.BlockSpec((1, qb, SLAB), lambda bi, h, i: (bi, i, h)), seq, seq, meta, meta,
                  vec(HEAD_DIM), vec(HEAD_DIM), vec(HEAD_DIM), vec(HEAD_DIM), vec(SLAB)],
        out_specs=pl.BlockSpec((1, qb, SLAB), lambda bi, h, i: (bi, i, h)),
        out_shape=jax.ShapeDtypeStruct((b, s, MIX_HALF), BF16),
        scratch_shapes=[pltpu.VMEM((2 * qb, SLAB), BF16), pltpu.VMEM((2 * qb, SLAB), F32),
                        pltpu.VMEM((2 * qb, 2 * SLAB), F32), pltpu.VMEM((2, cw, cw), F32)],
        compiler_params=pltpu.CompilerParams(dimension_semantics=("parallel", "parallel", "arbitrary"),
                                             vmem_limit_bytes=VMEM_LIMIT),
        name="diff_attention",
    )(qa, ka, va, ka_meta, va_meta, lq1, lk1, lq2, lk2, diff_norm)


def _gdn_kernel(q_ref, k_ref, v_ref, z_ref, g_ref, beta_ref, gn_ref, g512_ref, s0_ref, o_ref, sfin_ref,
                s_scr, o_scr, *, n_chunks):
    j = pl.program_id(0)
    n_pairs = GDN_HEADS // GDN_GROUP
    n_seq = q_ref.shape[0]

    @pl.when(j == 0)
    def _():
        for bi in range(n_seq):
            s_scr[bi] = s0_ref[...]

    row = lax.broadcasted_iota(jnp.int32, (CHUNK, GDN_GW), 0)
    lane = lax.broadcasted_iota(jnp.int32, (CHUNK, GDN_GW), 1)
    col = lane % CHUNK
    lane_head = lane // HEAD_DIM
    tri_incl = row >= col
    tri_strict = row > col
    eye = (row == col).astype(F32)
    bd_mask = ((lax.broadcasted_iota(jnp.int32, (GDN_GW, GDN_GW), 0) // HEAD_DIM)
               == (lax.broadcasted_iota(jnp.int32, (GDN_GW, GDN_GW), 1) // HEAD_DIM))

    def block_diag(x):
        xb = x.astype(BF16)
        zero = jnp.zeros_like(xb)
        return jnp.concatenate([jnp.where(lane_head == h, xb, zero) for h in range(GDN_GROUP)], axis=0)

    def pair_mm(x, y):
        return _dot(x.astype(BF16), block_diag(y))

    items = [(bi, c, p) for bi in range(n_seq) for c in range(n_chunks) for p in range(n_pairs)]
    rows_of = lambda c: slice(CHUNK * c, CHUNK * (c + 1))
    cols_of = lambda p: slice(GDN_GW * p, GDN_GW * (p + 1))
    load = lambda ref, it: ref[it[0], rows_of(it[1]), cols_of(it[2])]

    tbg = n_chunks * CHUNK
    ri = lax.broadcasted_iota(jnp.int32, (tbg, tbg), 0)
    ci = lax.broadcasted_iota(jnp.int32, (tbg, tbg), 1)
    chunk_tri = ((ri >= ci) & (ri // CHUNK == ci // CHUNK)).astype(BF16)
    g_cum = []
    for bi in range(n_seq):
        rest = g_ref[bi]
        terms = []
        for _ in range(3):
            term = rest.astype(BF16)
            terms.append(term)
            rest = rest - term.astype(F32)
        cum = _dot(chunk_tri, jnp.concatenate(terms, axis=1))
        g_cum.append(cum[:, :MIX_HALF] + cum[:, MIX_HALF:2 * MIX_HALF] + cum[:, 2 * MIX_HALF:])

    g_col, decay, k_f, k_beta, k_bd, q_f, vb = {}, {}, {}, {}, {}, {}, {}
    for it in items:
        g = load(g_ref, it)
        g_col[it] = g_cum[it[0]][rows_of(it[1]), cols_of(it[2])]
        g_row = jnp.sum(jnp.where(row <= col, g, 0.0), axis=0, keepdims=True)
        decay[it] = jnp.where(tri_incl, jnp.exp(jnp.where(tri_incl, g_col[it] - g_row, 0.0)), 0.0)
        beta = load(beta_ref, it).astype(F32)
        k_f[it] = load(k_ref, it).astype(F32)
        q_f[it] = load(q_ref, it).astype(F32)
        k_beta[it] = k_f[it] * beta
        vb[it] = load(v_ref, it).astype(F32) * beta
        k_bd[it] = block_diag(k_f[it])

    a, a_qk = {}, {}
    for it in items:
        lhs = jnp.concatenate([k_beta[it].astype(BF16), q_f[it].astype(BF16)], axis=0)
        kk = _dot_nt(lhs, k_bd[it])
        a[it] = jnp.where(tri_strict, kk[:CHUNK] * decay[it], 0.0)
        a_qk[it] = jnp.where(tri_incl, kk[CHUNK:] * decay[it], 0.0)

    t = {it: eye - a[it] for it in items}
    pw = dict(a)
    for _ in range(5):
        for it in items:
            pw[it] = pair_mm(pw[it], pw[it])
        for it in items:
            t[it] = t[it] + pair_mm(t[it], pw[it])

    u, w, k_g, q_g, e_last = {}, {}, {}, {}, {}
    for it in items:
        e_col = jnp.exp(g_col[it])
        rhs = jnp.concatenate([block_diag(vb[it]), block_diag(k_beta[it] * e_col)], axis=1)
        uw = _dot(t[it].astype(BF16), rhs)
        u[it], w[it] = uw[:, :GDN_GW], uw[:, GDN_GW:]
        g_last = g_col[it][CHUNK - 1:CHUNK, :]
        k_g[it] = (k_f[it] * jnp.exp(g_last - g_col[it])).astype(BF16)
        q_g[it] = q_f[it] * e_col
        e_last[it] = jnp.exp(g_last)

    state = {(bi, p): s_scr[bi, p] for bi in range(n_seq) for p in range(n_pairs)}
    sp = lambda it: (it[0], it[2])
    for c in range(n_chunks):
        its = [(bi, c, p) for bi in range(n_seq) for p in range(n_pairs)]
        ws_qs = [_dot(jnp.concatenate([w[it], q_g[it]], axis=0).astype(BF16), state[sp(it)].astype(BF16))
                 for it in its]
        v_new = [u[it] - sq[:CHUNK] for it, sq in zip(its, ws_qs)]
        for it, sq, vn in zip(its, ws_qs, v_new):
            o_scr[it[0], rows_of(c), cols_of(it[2])] = sq[CHUNK:] + pair_mm(a_qk[it], vn)
        for it, vn in zip(its, v_new):
            upd = _dot_tn(k_g[it], vn.astype(BF16))
            state[sp(it)] = state[sp(it)] * e_last[it] + jnp.where(bd_mask, upd, 0.0)
    for (bi, p), val in state.items():
        s_scr[bi, p] = val

    for bi in range(n_seq):
        o = o_scr[bi]
        ss = _group_sumsq(o, g512_ref)
        y = o * lax.rsqrt(ss * (1.0 / HEAD_DIM) + EPS) * gn_ref[...]
        o_ref[bi] = (y * z_ref[bi].astype(F32)).astype(BF16)

    @pl.when(j == pl.num_programs(0) - 1)
    def _():
        sfin_ref[...] = s_scr[...]


def _gated_delta(q, k, v, z, gb, beta, gn_rep, g512, s0, tbg):
    b, s, _ = q.shape
    nb = s // tbg
    state_shape = (b,) + s0.shape
    tok = pl.BlockSpec((b, tbg, MIX_HALF), lambda j: (0, j, 0))
    full = lambda shape: pl.BlockSpec(shape, lambda j: (0,) * len(shape))
    return pl.pallas_call(
        functools.partial(_gdn_kernel, n_chunks=tbg // CHUNK),
        grid=(nb,),
        in_specs=[tok] * 6 + [full((1, MIX_HALF)), full(g512.shape), full(s0.shape)],
        out_specs=[tok, pl.BlockSpec(state_shape, lambda j: (0, 0, 0, 0))],
        out_shape=[jax.ShapeDtypeStruct((b, s, MIX_HALF), BF16), jax.ShapeDtypeStruct(state_shape, F32)],
        scratch_shapes=[pltpu.VMEM(state_shape, F32), pltpu.VMEM((b, tbg, MIX_HALF), F32)],
        compiler_params=pltpu.CompilerParams(dimension_semantics=("arbitrary",),
                                             vmem_limit_bytes=VMEM_LIMIT),
        name="gated_delta",
    )(q, k, v, z, gb, beta, gn_rep, g512, s0)


def _out_router_kernel(x_ref, oa_ref, ob_ref, wo_ref, gain_ref, wr_ref, br_ref, before_ref,
                       h_ref, t_ref, rows_ref, cols_ref, cnt_ref):
    h = x_ref[0] + _dot(jnp.concatenate([oa_ref[0], ob_ref[0]], axis=1), wo_ref[...])
    h_ref[0] = h
    t = _rms_scale(h) * gain_ref[...]
    t_hi = t.astype(BF16)
    t_ref[0] = t_hi
    t_lo = (t - t_hi.astype(F32)).astype(BF16)
    wr = wr_ref[...]
    hi_part = _dot(t_hi, wr)
    logits = hi_part[:, :ROUTE_LANES] + hi_part[:, ROUTE_LANES:] + _dot(t_lo, wr)[:, :ROUTE_LANES] + br_ref[...]

    lt = logits.T
    tb = lt.shape[1]
    row8 = lax.broadcasted_iota(jnp.int32, (EXPERTS_PER_GROUP, tb), 0)
    col_max = lambda v: jnp.max(v, axis=0, keepdims=True)
    col_sum = lambda v: jnp.sum(v, axis=0, keepdims=True)
    first_argmax = lambda v, vmax: jnp.min(jnp.where(v == vmax, row8, EXPERTS_PER_GROUP), axis=0, keepdims=True)

    gl = jnp.where(row8 < N_GROUPS, lt[:EXPERTS_PER_GROUP], NEG_BIG)
    gmax = col_max(gl)
    gsel = first_argmax(gl, gmax)
    psel = 1.0 / col_sum(jnp.exp(gl - gmax))
    el = jnp.zeros((EXPERTS_PER_GROUP, tb), F32)
    for g in range(N_GROUPS):
        lo = EXPERT_ROW0 + EXPERTS_PER_GROUP * g
        el = jnp.where(gsel == g, lt[lo:lo + EXPERTS_PER_GROUP], el)
    m1 = col_max(el)
    i1 = first_argmax(el, m1)
    el2 = jnp.where(row8 == i1, NEG_BIG, el)
    m2 = col_max(el2)
    i2 = first_argmax(el2, m2)
    denom = col_sum(jnp.exp(el - m1))
    p1 = 1.0 / denom
    p2 = jnp.exp(m2 - m1) / denom
    w1 = p1 / (p1 + p2) * psel
    w2 = p2 / (p1 + p2) * psel
    e1 = gsel * EXPERTS_PER_GROUP + i1
    e2 = gsel * EXPERTS_PER_GROUP + i2

    row_e = lax.broadcasted_iota(jnp.int32, (N_EXPERTS, tb), 0)
    hot1 = (row_e == e1).astype(F32)
    hot2 = (row_e == e2).astype(F32)
    both = hot1 + hot2
    n_gran = jnp.ceil(jnp.sum(both, axis=1, keepdims=True) * (1.0 / SLOT_GRAN)) * SLOT_GRAN
    ei = lax.broadcasted_iota(jnp.int32, (N_EXPERTS, N_EXPERTS), 0)
    ej = lax.broadcasted_iota(jnp.int32, (N_EXPERTS, N_EXPERTS), 1)
    seg_start = _dot((ei > ej).astype(F32), jnp.broadcast_to(n_gran, (N_EXPERTS, SLAB)), precision=HIGHEST)[:, :1]
    earlier = _dot(both.astype(BF16), before_ref[...])
    where_to = earlier + seg_start
    pos1 = col_sum(hot1 * where_to)
    pos2 = col_sum(hot2 * where_to)

    info = jnp.concatenate([pos1, pos2, w1, w2, e1.astype(F32), e2.astype(F32),
                            jnp.zeros((SLAB - 6, tb), F32)], axis=0)
    rows_ref[0] = info[:8]
    cols_ref[0] = info.T
    cnt_ref[0] = jnp.broadcast_to(n_gran, (N_EXPERTS, SLAB))


def _out_router(x, oa, ob, w_o, gain, w_route, b_route, before, tb):
    b, s, d = x.shape
    nb = s // tb
    full = lambda shape: pl.BlockSpec(shape, lambda i, j: (0,) * len(shape))
    tokd = pl.BlockSpec((1, tb, d), lambda i, j: (i, j, 0))
    tokh = pl.BlockSpec((1, tb, MIX_HALF), lambda i, j: (i, j, 0))
    return pl.pallas_call(
        _out_router_kernel,
        grid=(b, nb),
        in_specs=[tokd, tokh, tokh, full(w_o.shape), full((1, d)), full(w_route.shape), full((1, ROUTE_LANES)),
                  full(before.shape)],
        out_specs=[tokd, tokd, pl.BlockSpec((1, 8, tb), lambda i, j: (i, 0, j)),
                   pl.BlockSpec((1, tb, SLAB), lambda i, j: (i, j, 0)),
                   pl.BlockSpec((1, N_EXPERTS, SLAB), lambda i, j: (i * nb + j, 0, 0))],
        out_shape=[jax.ShapeDtypeStruct((b, s, d), F32), jax.ShapeDtypeStruct((b, s, d), BF16),
                   jax.ShapeDtypeStruct((b, 8, s), F32), jax.ShapeDtypeStruct((b, s, SLAB), F32),
                   jax.ShapeDtypeStruct((b * nb, N_EXPERTS, SLAB), F32)],
        compiler_params=pltpu.CompilerParams(dimension_semantics=("parallel", "parallel"),
                                             vmem_limit_bytes=VMEM_LIMIT),
        name="out_router",
    )(x, oa, ob, w_o, gain, w_route, b_route, before)


def _granule_copies(table_ref, block, buf, local_ref, global_ref, sem, to_global, n_gran):
    copies = []
    for g in range(n_gran):
        loc = local_ref.at[buf, pl.ds(g * SLOT_GRAN, SLOT_GRAN)]
        glob = global_ref.at[pl.ds(pl.multiple_of(table_ref[block, g], SLOT_GRAN), SLOT_GRAN)]
        copies.append(pltpu.make_async_copy(loc, glob, sem.at[buf]) if to_global
                      else pltpu.make_async_copy(glob, loc, sem.at[buf]))
    return copies


def _dispatch_kernel(gdst_ref, t_ref, rows_ref, xg_init_ref, xg_ref, xs_scr, sem, *, n_slots):
    del xg_init_ref
    blk = pl.program_id(0)
    last = pl.num_programs(0) - 1
    buf = blk % 2
    tb = t_ref.shape[0]
    n_gran = n_slots // SLOT_GRAN
    info = rows_ref[0]
    slot = lax.broadcasted_iota(jnp.int32, (n_slots, tb), 0).astype(F32)
    onehot = jnp.where((slot == info[0:1]) | (slot == info[1:2]), 1.0, 0.0).astype(BF16)
    xs_scr[buf] = _dot(onehot, t_ref[...]).astype(BF16)
    for copy in _granule_copies(gdst_ref, blk, buf, xs_scr, xg_ref, sem, True, n_gran):
        copy.start()

    @pl.when(blk > 0)
    def _():
        for copy in _granule_copies(gdst_ref, blk - 1, 1 - buf, xs_scr, xg_ref, sem, True, n_gran):
            copy.wait()

    @pl.when(blk == last)
    def _():
        for copy in _granule_copies(gdst_ref, blk, buf, xs_scr, xg_ref, sem, True, n_gran):
            copy.wait()


def _dispatch(granule_dst, t, rows, xg_init, tb, n_slots):
    n, d = t.shape
    nblk = n // tb
    nb = rows.shape[2] // tb
    grid_spec = pltpu.PrefetchScalarGridSpec(
        num_scalar_prefetch=1,
        grid=(nblk,),
        in_specs=[pl.BlockSpec((tb, d), lambda i, gd: (i, 0)),
                  pl.BlockSpec((1, 8, tb), lambda i, gd: (i // nb, 0, i % nb)),
                  pl.BlockSpec(memory_space=pl.ANY)],
        out_specs=pl.BlockSpec(memory_space=pl.ANY),
        scratch_shapes=[pltpu.VMEM((2, n_slots, d), BF16), pltpu.SemaphoreType.DMA((2,))],
    )
    return pl.pallas_call(
        functools.partial(_dispatch_kernel, n_slots=n_slots),
        grid_spec=grid_spec,
        out_shape=jax.ShapeDtypeStruct(xg_init.shape, BF16),
        input_output_aliases={3: 0},
        compiler_params=pltpu.CompilerParams(dimension_semantics=("arbitrary",), vmem_limit_bytes=VMEM_LIMIT),
        name="moe_dispatch",
    )(granule_dst, t, rows, xg_init)


def _experts_kernel(tile_expert_ref, n_tiles_ref, x_ref, wg_ref, wu_ref, wd_ref, y_ref):
    del tile_expert_ref

    @pl.when(pl.program_id(0) < n_tiles_ref[0])
    def _():
        x = x_ref[...]
        act = _silu(_dot(x, wg_ref[0].astype(BF16))) * _dot(x, wu_ref[0].astype(BF16))
        y_ref[...] = _dot(act.astype(BF16), wd_ref[0].astype(BF16)).astype(BF16)


def _experts(tile_expert, n_tiles, xg, w_gate, w_up, w_down, tile):
    rows, d = xg.shape
    used = lambda i, te, nt: jnp.minimum(i, nt[0] - 1)
    wspec = lambda shape: pl.BlockSpec((1,) + shape, lambda i, te, nt: (te[used(i, te, nt)], 0, 0))
    grid_spec = pltpu.PrefetchScalarGridSpec(
        num_scalar_prefetch=2,
        grid=(tile_expert.shape[0],),
        in_specs=[pl.BlockSpec((tile, d), lambda i, te, nt: (used(i, te, nt), 0)),
                  wspec((d, D_EXPERT)), wspec((d, D_EXPERT)), wspec((D_EXPERT, d))],
        out_specs=pl.BlockSpec((tile, d), lambda i, te, nt: (used(i, te, nt), 0)),
    )
    return pl.pallas_call(
        _experts_kernel,
        grid_spec=grid_spec,
        out_shape=jax.ShapeDtypeStruct((rows, d), BF16),
        input_output_aliases={2: 0},
        compiler_params=pltpu.CompilerParams(dimension_semantics=("arbitrary",), vmem_limit_bytes=VMEM_LIMIT),
        name="moe_experts",
    )(tile_expert, n_tiles, xg, w_gate, w_up, w_down)


def _combine_kernel(gsrc_ref, yg_ref, cols_ref, h_ref, gain_ref, o_ref, ys_scr, sem, *, n_slots):
    blk = pl.program_id(0)
    last = pl.num_programs(0) - 1
    buf = blk % 2
    tb = h_ref.shape[0]
    n_gran = n_slots // SLOT_GRAN

    @pl.when(blk == 0)
    def _():
        for copy in _granule_copies(gsrc_ref, 0, 0, ys_scr, yg_ref, sem, False, n_gran):
            copy.start()

    @pl.when(blk < last)
    def _():
        for copy in _granule_copies(gsrc_ref, blk + 1, 1 - buf, ys_scr, yg_ref, sem, False, n_gran):
            copy.start()

    info = cols_ref[...]
    slot = lax.broadcasted_iota(jnp.int32, (tb, n_slots), 1).astype(F32)
    weights = (jnp.where(slot == info[:, 0:1], info[:, 2:3], 0.0)
               + jnp.where(slot == info[:, 1:2], info[:, 3:4], 0.0)).astype(BF16)
    for copy in _granule_copies(gsrc_ref, blk, buf, ys_scr, yg_ref, sem, False, n_gran):
        copy.wait()
    y = _dot(weights, ys_scr[buf])
    o_ref[...] = _rms_scale(h_ref[...] + y) * gain_ref[...]


def _combine(granule_dst, yg, cols, h, gain, tb, n_slots):
    n, d = h.shape
    grid_spec = pltpu.PrefetchScalarGridSpec(
        num_scalar_prefetch=1,
        grid=(n // tb,),
        in_specs=[pl.BlockSpec(memory_space=pl.ANY),
                  pl.BlockSpec((tb, SLAB), lambda i, gd: (i, 0)),
                  pl.BlockSpec((tb, d), lambda i, gd: (i, 0)),
                  pl.BlockSpec((1, d), lambda i, gd: (0, 0))],
        out_specs=pl.BlockSpec((tb, d), lambda i, gd: (i, 0)),
        scratch_shapes=[pltpu.VMEM((2, n_slots, d), BF16), pltpu.SemaphoreType.DMA((2,))],
    )
    return pl.pallas_call(
        functools.partial(_combine_kernel, n_slots=n_slots),
        grid_spec=grid_spec,
        out_shape=jax.ShapeDtypeStruct((n, d), F32),
        compiler_params=pltpu.CompilerParams(dimension_semantics=("arbitrary",), vmem_limit_bytes=VMEM_LIMIT),
        name="moe_combine",
    )(granule_dst, yg, cols, h, gain)


def _regroup_plan(n_gran_be, n_slots, tile, n_tiles_max):
    cnt = n_gran_be.astype(jnp.int32)
    region = (jnp.sum(cnt, axis=0) + tile - 1) // tile * tile
    region_end = jnp.cumsum(region)
    first_row = (region_end - region)[None, :] + jnp.cumsum(cnt, axis=0) - cnt
    seg_end = jnp.cumsum(cnt, axis=1)
    seg_start = seg_end - cnt
    g_row = jnp.arange(n_slots // SLOT_GRAN, dtype=jnp.int32) * SLOT_GRAN
    owned = ((g_row[None, :, None] >= seg_start[:, None, :]) & (g_row[None, :, None] < seg_end[:, None, :]))
    in_use = g_row[None, :] < seg_end[:, -1:]
    row = g_row[None, :] + jnp.sum(jnp.where(owned, (first_row - seg_start)[:, None, :], 0), axis=-1)
    zero_row = n_tiles_max * tile
    spill_row = zero_row + SLOT_GRAN + (jnp.arange(cnt.shape[0], dtype=jnp.int32) % 2)[:, None] * n_slots + g_row
    dispatch_dst = jnp.where(in_use, row, spill_row).astype(jnp.int32)
    combine_src = jnp.where(in_use, row, zero_row).astype(jnp.int32)
    tile_row = jnp.arange(n_tiles_max, dtype=jnp.int32) * tile
    tile_expert = jnp.minimum(jnp.sum(tile_row[:, None] >= region_end[None, :], axis=-1), N_EXPERTS - 1)
    n_tiles = (region_end[-1] // tile).astype(jnp.int32)[None]
    return dispatch_dst, combine_src, tile_expert.astype(jnp.int32), n_tiles


def _rope_tables(first, length):
    pos = jnp.arange(first, first + length, dtype=F32)
    inv_freq = ROPE_THETA ** (-jnp.arange(0, HEAD_DIM, 2, dtype=F32) / HEAD_DIM)
    ang = pos[:, None] * inv_freq[None, :]
    cos, sin = jnp.cos(ang), jnp.sin(ang)
    zero = jnp.zeros_like(sin)
    cos_t = jnp.tile(cos, (1, 4))
    sina_t = jnp.tile(jnp.concatenate([-sin, zero], axis=1), (1, 2))
    sinb_t = jnp.tile(jnp.concatenate([zero, sin], axis=1), (1, 2))
    return cos_t, sina_t, sinb_t


def _block(total, want):
    blk = min(total, want)
    assert total % blk == 0, (total, blk)
    return blk


def kernel(x, meta, norm_mix, w_in, lambda_q1, lambda_k1, lambda_q2, lambda_k2, diff_norm, conv_w, a_log, dt_bias,
           gdn_norm, w_out, norm_ffn, w_group, b_group, w_router, b_router, w_gate, w_up, w_down, norm_final):
    b, s, d = x.shape
    assert d == D_MODEL and meta.shape == (N_META, D_MODEL) and s % CHUNK == 0
    assert norm_mix.shape[0] == 1, "single-layer block"
    l = 0

    w = w_in[l]
    w_a = w[:, :3 * MIX_HALF].astype(BF16)
    w_b = w[:, 3 * MIX_HALF:6 * MIX_HALF].astype(BF16)
    w_z = w[:, 6 * MIX_HALF:7 * MIX_HALF].astype(BF16)
    c_ab = 7 * MIX_HALF
    lane_pad = lambda a: jnp.pad(a, [(0, 0)] * (a.ndim - 1) + [(0, SLAB - a.shape[-1])])
    w_g = lane_pad(w[:, c_ab:c_ab + 2 * GDN_HEADS]).astype(BF16)
    alog_rep = lane_pad(a_log[l])[None]
    dtb_rep = lane_pad(dt_bias[l])[None]
    gn_rep = jnp.tile(gdn_norm[l], GDN_HEADS)[None]
    g512 = (jnp.arange(MIX_HALF)[:, None] // HEAD_DIM == jnp.arange(MIX_HALF)[None, :] // HEAD_DIM).astype(BF16)
    gain_mix = norm_mix[l][None]
    w_o = w_out[l].astype(BF16)
    route_pad = lambda g, e: jnp.concatenate(
        [g, jnp.zeros(g.shape[:-1] + (EXPERT_ROW0 - N_GROUPS,), F32), e,
         jnp.zeros(g.shape[:-1] + (ROUTE_LANES - EXPERT_ROW0 - N_EXPERTS,), F32)], axis=-1)
    w_route_f = route_pad(w_group[l], w_router[l])
    w_route_hi = w_route_f.astype(BF16)
    w_route = jnp.concatenate([w_route_hi, (w_route_f - w_route_hi.astype(F32)).astype(BF16)], axis=1)
    b_route = route_pad(b_group[l], b_router[l])[None]
    lam_vecs = [v[l][None] for v in (lambda_q1, lambda_k1, lambda_q2, lambda_k2)]
    rope_meta = _rope_tables(0, N_META)
    rope_frames = _rope_tables(N_META, s)

    meta3 = meta[None]
    _, ka_m, va_m = _proj_attn(meta3, gain_mix, w_a, *rope_meta, N_META)
    zero_halo = jnp.zeros((1, 1, N_META, d), F32)
    gdn_m = _proj_gdn(meta3, zero_halo, gain_mix, w_b, w_z, w_g, conv_w[l], alog_rep, dtb_rep, g512, N_META)
    pad_m = lambda a: jnp.pad(a, ((0, 0), (0, CHUNK - N_META), (0, 0)))
    s_zero = jnp.zeros((GDN_HEADS // GDN_GROUP, GDN_GW, GDN_GW), F32)
    _, s_meta = _gated_delta(*[pad_m(a) for a in gdn_m], gn_rep, g512, s_zero, CHUNK)

    tb_a = _block(s, PROJ_ATTN_ROWS)
    qa, ka, va = _proj_attn(x, gain_mix, w_a, *rope_frames, tb_a)
    tb_g = _block(s, PROJ_GDN_ROWS)
    nb_g = s // tb_g
    tails = x.reshape(b, nb_g, tb_g, d)[:, :-1, tb_g - N_META:, :]
    halo = jnp.concatenate([jnp.broadcast_to(meta[None, None], (b, 1, N_META, d)), tails], axis=1)
    gdn_f = _proj_gdn(x, halo, gain_mix, w_b, w_z, w_g, conv_w[l], alog_rep, dtb_rep, g512, tb_g)

    oa = _diff_attention(qa, ka, va, ka_m, va_m, *lam_vecs, diff_norm[l][None],
                         _block(s, ATTN_QUERY_ROWS), ATTN_KEY_BLOCK)
    ob, _ = _gated_delta(*gdn_f, gn_rep, g512, s_meta[0], _block(s, GDN_ROWS))

    tb_r = _block(s, MOE_BLOCK)
    tok = jnp.arange(tb_r)
    before = (tok[:, None] < tok[None, :]).astype(BF16)
    h1, t, rows, cols, cnt = _out_router(x, oa, ob, w_o, norm_ffn[l][None], w_route, b_route, before, tb_r)

    n = b * s
    nblk = n // tb_r
    n_slots = 2 * tb_r + N_EXPERTS * SLOT_GRAN
    rows_max = 2 * n + (SLOT_GRAN - 1) * N_EXPERTS * nblk + (MOE_TILE - 1) * N_EXPERTS
    n_tiles_max = -(-rows_max // MOE_TILE)
    dispatch_dst, combine_src, tile_expert, n_tiles = _regroup_plan(cnt[:, :, 0], n_slots, MOE_TILE, n_tiles_max)
    buf_rows = n_tiles_max * MOE_TILE + SLOT_GRAN + 2 * n_slots
    xg = _dispatch(dispatch_dst, t.reshape(n, d), rows, jnp.zeros((buf_rows, d), BF16), tb_r, n_slots)
    yg = _experts(tile_expert, n_tiles, xg, w_gate[l], w_up[l], w_down[l], MOE_TILE)
    out = _combine(combine_src, yg, cols.reshape(n, SLAB), h1.reshape(n, d), norm_final[None], tb_r, n_slots)
    return out.reshape(b, s, d)
```

```python
import functools
import math

import jax
import jax.numpy as jnp
from jax import lax
from jax.experimental import pallas as pl
from jax.experimental.pallas import tpu as pltpu

F32 = jnp.float32
BF16 = jnp.bfloat16
HIGHEST = lax.Precision.HIGHEST

D_MODEL = 1024
N_META = 16
CHUNK = 64
EPS = 1e-6
ROPE_THETA = 10000.0
HEAD_DIM = 64
SLAB = 128
DA_HEADS = 4
GDN_HEADS = 8
MIX_HALF = 512
GDN_GROUP = 2
GDN_GW = GDN_GROUP * HEAD_DIM
N_GROUPS = 4
EXPERTS_PER_GROUP = 8
N_EXPERTS = 32
D_EXPERT = 256
CONV_K = 4
LAM_INIT = 0.8 - 0.6 * math.exp(-0.3 * 0)
ROUTE_LANES = 128
EXPERT_ROW0 = 8
SLOT_GRAN = 16
MOE_BLOCK = 512
MOE_TILE = 512
ROW_STRIP = 64
PROJ_ATTN_ROWS = 1024
PROJ_GDN_ROWS = 512
ATTN_QUERY_ROWS = 2048
ATTN_KEY_BLOCK = 512
GDN_ROWS = 256
NEG_BIG = -1e30
LOG2E = math.log2(math.e)
VMEM_LIMIT = 56 * 1024 * 1024


def _dot(a, b, precision=None):
    return jnp.dot(a, b, preferred_element_type=F32, precision=precision)


def _dot_nt(a, b):
    return lax.dot_general(a, b, (((1,), (1,)), ((), ())), preferred_element_type=F32)


def _dot_tn(a, b):
    return lax.dot_general(a, b, (((0,), (0,)), ((), ())), preferred_element_type=F32)


def _rms_scale(x):
    return x * lax.rsqrt(jnp.mean(x * x, axis=-1, keepdims=True) + EPS)


def _silu(x):
    return x * jax.nn.sigmoid(x)


def _group_sumsq(x, g_ref):
    return _dot((x * x).astype(BF16), g_ref[...])


def _proj_attn_kernel(x_ref, gain_ref, w_ref, cos_ref, sina_ref, sinb_ref, qa_ref, ka_ref, va_ref):
    u = (_rms_scale(x_ref[0]) * gain_ref[...]).astype(BF16)
    proj = _dot(u, w_ref[...])
    va_ref[0] = proj[:, 2 * MIX_HALF:].astype(BF16)
    cos, sina, sinb = cos_ref[...], sina_ref[...], sinb_ref[...]
    for s in range(2 * DA_HEADS):
        xs = proj[:, SLAB * s:SLAB * (s + 1)]
        r = xs * cos + pltpu.roll(xs, SLAB - 32, 1) * sina + pltpu.roll(xs, 32, 1) * sinb
        if s < DA_HEADS:
            qa_ref[0, :, SLAB * s:SLAB * (s + 1)] = (r * (HEAD_DIM ** -0.5 * LOG2E)).astype(BF16)
        else:
            t = s - DA_HEADS
            ka_ref[0, :, SLAB * t:SLAB * (t + 1)] = r.astype(BF16)


def _proj_attn(x, gain, w_a, cos, sina, sinb, tb):
    b, s, d = x.shape
    nb = s // tb
    slab_out = jax.ShapeDtypeStruct((b, s, MIX_HALF), BF16)
    full = lambda shape: pl.BlockSpec(shape, lambda i, j: (0,) * len(shape))
    tok = pl.BlockSpec((1, tb, MIX_HALF), lambda i, j: (i, j, 0))
    tab = pl.BlockSpec((tb, SLAB), lambda i, j: (j, 0))
    return pl.pallas_call(
        _proj_attn_kernel,
        grid=(b, nb),
        in_specs=[pl.BlockSpec((1, tb, d), lambda i, j: (i, j, 0)), full((1, d)), full(w_a.shape), tab, tab, tab],
        out_specs=[tok, tok, tok],
        out_shape=[slab_out, slab_out, slab_out],
        compiler_params=pltpu.CompilerParams(dimension_semantics=("parallel", "parallel"),
                                             vmem_limit_bytes=VMEM_LIMIT),
        name="proj_attn",
    )(x, gain, w_a, cos, sina, sinb)


def _proj_gdn_kernel(x_ref, halo_ref, gain_ref, wb_ref, wz_ref, wg_ref, convw_ref, alog_ref, dtb_ref, g_ref,
                     q_ref, k_ref, v_ref, z_ref, gb_ref, beta_ref, pb_scr):
    halo_rows = halo_ref.shape[2]
    xe = jnp.concatenate([halo_ref[0, 0], x_ref[0]], axis=0)
    ue = (_rms_scale(xe) * gain_ref[...]).astype(BF16)
    u = ue[halo_rows:]
    cw = convw_ref[...]
    outs = (q_ref, k_ref, v_ref)
    for part in range(3):
        cols = slice(MIX_HALF * part, MIX_HALF * (part + 1))
        pb_scr[...] = _dot(ue, wb_ref[:, cols])
        c = cw[:, cols]
        tb = pb_scr.shape[0] - halo_rows
        y = pb_scr[halo_rows:, :] * c[CONV_K - 1:CONV_K]
        for back in range(1, CONV_K):
            tap = CONV_K - 1 - back
            y = y + pb_scr[pl.ds(halo_rows - back, tb), :] * c[tap:tap + 1]
        y = _silu(y)
        if part < 2:
            y = y * lax.rsqrt(_group_sumsq(y, g_ref) + EPS)
            if part == 0:
                y = y * (HEAD_DIM ** -0.5)
        outs[part][0] = y.astype(BF16)
    z_ref[0] = _silu(_dot(u, wz_ref[...])).astype(BF16)
    gates = _dot(u, wg_ref[...])
    ab = gates + dtb_ref[...]
    softplus = jnp.maximum(ab, 0.0) + jnp.log1p(jnp.exp(-jnp.abs(ab)))
    g = -jnp.exp(alog_ref[...]) * softplus
    beta = jax.nn.sigmoid(gates)
    tb = g.shape[0]
    first_head = lax.broadcasted_iota(jnp.int32, (tb, SLAB), 1) < HEAD_DIM
    spread = lambda v, h: jnp.broadcast_to(v[:, h:h + 1], (tb, SLAB))
    for p in range(GDN_HEADS // 2):
        cols = slice(SLAB * p, SLAB * (p + 1))
        gb_ref[0, :, cols] = jnp.where(first_head, spread(g, 2 * p), spread(g, 2 * p + 1))
        beta_ref[0, :, cols] = jnp.where(first_head, spread(beta, GDN_HEADS + 2 * p),
                                         spread(beta, GDN_HEADS + 2 * p + 1)).astype(BF16)


def _proj_gdn(x, halo, gain, w_b, w_z, w_g, conv_w, alog_rep, dtb_rep, g512, tb):
    b, s, d = x.shape
    nb = s // tb
    full = lambda shape: pl.BlockSpec(shape, lambda i, j: (0,) * len(shape))
    tok = pl.BlockSpec((1, tb, MIX_HALF), lambda i, j: (i, j, 0))
    bf = jax.ShapeDtypeStruct((b, s, MIX_HALF), BF16)
    return pl.pallas_call(
        _proj_gdn_kernel,
        grid=(b, nb),
        in_specs=[pl.BlockSpec((1, tb, d), lambda i, j: (i, j, 0)),
                  pl.BlockSpec((1, 1) + halo.shape[2:], lambda i, j: (i, j, 0, 0)),
                  full((1, d)), full(w_b.shape), full(w_z.shape), full(w_g.shape), full(conv_w.shape),
                  full((1, SLAB)), full((1, SLAB)), full(g512.shape)],
        out_specs=[tok] * 6,
        out_shape=[bf, bf, bf, bf, jax.ShapeDtypeStruct((b, s, MIX_HALF), F32), bf],
        scratch_shapes=[pltpu.VMEM((halo.shape[2] + tb, MIX_HALF), F32)],
        compiler_params=pltpu.CompilerParams(dimension_semantics=("parallel", "parallel"),
                                             vmem_limit_bytes=VMEM_LIMIT),
        name="proj_gdn",
    )(x, halo, gain, w_b, w_z, w_g, conv_w, alog_rep, dtb_rep, g512)


def _attn_kernel(q_ref, k_ref, v_ref, km_ref, vm_ref, lq1_ref, lk1_ref, lq2_ref, lk2_ref, dn_ref, o_ref,
                 qq_scr, m_scr, acc_scr, st_scr, *, qb, cw):
    i = pl.program_id(2)
    n_col = 2 * qb // cw
    per_map = qb // cw
    q = q_ref[0]
    lane = lax.broadcasted_iota(jnp.int32, (qb, SLAB), 1)
    zero = jnp.zeros_like(q)
    qq_scr[...] = jnp.concatenate([jnp.where(lane < HEAD_DIM, q, zero), jnp.where(lane < HEAD_DIM, zero, q)], axis=0)

    def with_ones(v_blk):
        return jnp.concatenate([v_blk, jnp.ones_like(v_blk)], axis=1)

    v1_meta = with_ones(vm_ref[0])
    for c in range(n_col):
        rs = slice(cw * c, cw * (c + 1))
        s = _dot_nt(qq_scr[rs, :], km_ref[0])
        m_new = jnp.max(s, axis=1, keepdims=True)
        acc_scr[rs, :] = _dot(jnp.exp2((s - m_new).astype(BF16)), v1_meta)
        m_scr[rs, :] = jnp.broadcast_to(m_new, (cw, SLAB))

    def scores_into(slot, start, c):
        st_scr[slot] = _dot_nt(qq_scr[cw * c:cw * (c + 1), :], k_ref[0, pl.ds(start, cw), :])

    def softmax_pv(slot, v1, c, masked):
        alphas = []
        for t in range(cw // ROW_STRIP):
            ls = slice(ROW_STRIP * t, ROW_STRIP * (t + 1))
            gs = slice(cw * c + ROW_STRIP * t, cw * c + ROW_STRIP * (t + 1))
            s = st_scr[slot, ls, :]
            if masked:
                q_chunk = (lax.broadcasted_iota(jnp.int32, s.shape, 0) + ROW_STRIP * t) // CHUNK
                k_chunk = lax.broadcasted_iota(jnp.int32, s.shape, 1) // CHUNK
                s = jnp.where(k_chunk <= q_chunk, s, NEG_BIG)
            parts = [s[:, SLAB * k:SLAB * (k + 1)] for k in range(cw // SLAB)]
            lane_max = functools.reduce(jnp.maximum, parts)
            m_prev = m_scr[gs, :]
            m_new = jnp.maximum(m_prev, jnp.max(lane_max, axis=1, keepdims=True))
            for k, part in enumerate(parts):
                st_scr[slot, ls, SLAB * k:SLAB * (k + 1)] = jnp.exp2(part - m_new)
            alphas.append(jnp.exp2(m_prev - m_new))
            m_scr[gs, :] = m_new
        rs = slice(cw * c, cw * (c + 1))
        alpha = jnp.concatenate(alphas, axis=0)
        pv = _dot(st_scr[slot].astype(BF16), v1)
        acc_scr[rs, :] = acc_scr[rs, :] * jnp.concatenate([alpha, alpha], axis=1) + pv

    parity = [0]
    scores_into(0, 0, 0)

    def key_block(start, groups, masked_groups, following):
        v1 = with_ones(v_ref[0, pl.ds(start, cw), :])
        for idx, c in enumerate(groups):
            slot = parity[0]
            if idx + 1 < len(groups):
                scores_into(1 - slot, start, groups[idx + 1])
            elif following is not None:
                scores_into(1 - slot, *following)
            softmax_pv(slot, v1, c, c in masked_groups)
            parity[0] = 1 - slot

    all_groups = list(range(n_col))

    def full_blocks(j, carry):
        for d in range(per_map):
            start = pl.multiple_of((per_map * j + d) * cw, cw)
            key_block(start, all_groups, (), (start + cw, 0))
        return carry

    lax.fori_loop(0, i, full_blocks, 0)
    for d in range(per_map):
        start = pl.multiple_of((per_map * i + d) * cw, cw)
        groups = [c for c in all_groups if c % per_map >= d]
        following = (start + cw, d + 1) if d + 1 < per_map else None
        key_block(start, groups, [c for c in groups if c % per_map == d], following)

    acc = acc_scr[...]
    o1 = acc[:qb, :SLAB] / acc[:qb, SLAB:]
    o2 = acc[qb:, :SLAB] / acc[qb:, SLAB:]
    lam = (jnp.exp(jnp.sum(lq1_ref[...] * lk1_ref[...], axis=1, keepdims=True))
           - jnp.exp(jnp.sum(lq2_ref[...] * lk2_ref[...], axis=1, keepdims=True)) + LAM_INIT)
    o = o1 - lam * o2
    o_ref[0] = (_rms_scale(o) * dn_ref[...] * (1.0 - LAM_INIT)).astype(BF16)


def _diff_attention(qa, ka, va, ka_meta, va_meta, lq1, lk1, lq2, lk2, diff_norm, qb, cw):
    b, s, _ = qa.shape
    nq = s // qb
    vec = lambda n: pl.BlockSpec((1, n), lambda bi, h, i: (0, 0))
    seq = pl.BlockSpec((1, s, SLAB), lambda bi, h, i: (bi, 0, h))
    meta = pl.BlockSpec((1, N_META, SLAB), lambda bi, h, i: (0, 0, h))
    return pl.pallas_call(
        functools.partial(_attn_kernel, qb=qb, cw=cw),
        grid=(b, DA_HEADS, nq),
        in_specs=[pl.BlockSpec((1, qb, SLAB), lambda bi, h, i: (bi, i, h)), seq, seq, meta, meta,
                  vec(HEAD_DIM), vec(HEAD_DIM), vec(HEAD_DIM), vec(HEAD_DIM), vec(SLAB)],
        out_specs=pl.BlockSpec((1, qb, SLAB), lambda bi, h, i: (bi, i, h)),
        out_shape=jax.ShapeDtypeStruct((b, s, MIX_HALF), BF16),
        scratch_shapes=[pltpu.VMEM((2 * qb, SLAB), BF16), pltpu.VMEM((2 * qb, SLAB), F32),
                        pltpu.VMEM((2 * qb, 2 * SLAB), F32), pltpu.VMEM((2, cw, cw), F32)],
        compiler_params=pltpu.CompilerParams(dimension_semantics=("parallel", "parallel", "arbitrary"),
                                             vmem_limit_bytes=VMEM_LIMIT),
        name="diff_attention",
    )(qa, ka, va, ka_meta, va_meta, lq1, lk1, lq2, lk2, diff_norm)


def _gdn_kernel(q_ref, k_ref, v_ref, z_ref, g_ref, beta_ref, gn_ref, g512_ref, s0_ref, o_ref, sfin_ref,
                s_scr, o_scr, *, n_chunks):
    j = pl.program_id(0)
    n_pairs = GDN_HEADS // GDN_GROUP
    n_seq = q_ref.shape[0]

    @pl.when(j == 0)
    def _():
        for bi in range(n_seq):
            s_scr[bi] = s0_ref[...]

    row = lax.broadcasted_iota(jnp.int32, (CHUNK, GDN_GW), 0)
    lane = lax.broadcasted_iota(jnp.int32, (CHUNK, GDN_GW), 1)
    col = lane % CHUNK
    lane_head = lane // HEAD_DIM
    tri_incl = row >= col
    tri_strict = row > col
    eye = (row == col).astype(F32)
    bd_mask = ((lax.broadcasted_iota(jnp.int32, (GDN_GW, GDN_GW), 0) // HEAD_DIM)
               == (lax.broadcasted_iota(jnp.int32, (GDN_GW, GDN_GW), 1) // HEAD_DIM))

    def block_diag(x):
        xb = x.astype(BF16)
        zero = jnp.zeros_like(xb)
        return jnp.concatenate([jnp.where(lane_head == h, xb, zero) for h in range(GDN_GROUP)], axis=0)

    def pair_mm(x, y):
        return _dot(x.astype(BF16), block_diag(y))

    items = [(bi, c, p) for bi in range(n_seq) for c in range(n_chunks) for p in range(n_pairs)]
    rows_of = lambda c: slice(CHUNK * c, CHUNK * (c + 1))
    cols_of = lambda p: slice(GDN_GW * p, GDN_GW * (p + 1))
    load = lambda ref, it: ref[it[0], rows_of(it[1]), cols_of(it[2])]

    tbg = n_chunks * CHUNK
    ri = lax.broadcasted_iota(jnp.int32, (tbg, tbg), 0)
    ci = lax.broadcasted_iota(jnp.int32, (tbg, tbg), 1)
    chunk_tri = ((ri >= ci) & (ri // CHUNK == ci // CHUNK)).astype(BF16)
    g_cum = []
    for bi in range(n_seq):
        rest = g_ref[bi]
        terms = []
        for _ in range(3):
            term = rest.astype(BF16)
            terms.append(term)
            rest = rest - term.astype(F32)
        cum = _dot(chunk_tri, jnp.concatenate(terms, axis=1))
        g_cum.append(cum[:, :MIX_HALF] + cum[:, MIX_HALF:2 * MIX_HALF] + cum[:, 2 * MIX_HALF:])

    g_col, decay, k_f, k_beta, k_bd, q_f, vb = {}, {}, {}, {}, {}, {}, {}
    for it in items:
        g = load(g_ref, it)
        g_col[it] = g_cum[it[0]][rows_of(it[1]), cols_of(it[2])]
        g_row = jnp.sum(jnp.where(row <= col, g, 0.0), axis=0, keepdims=True)
        decay[it] = jnp.where(tri_incl, jnp.exp(jnp.where(tri_incl, g_col[it] - g_row, 0.0)), 0.0)
        beta = load(beta_ref, it).astype(F32)
        k_f[it] = load(k_ref, it).astype(F32)
        q_f[it] = load(q_ref, it).astype(F32)
        k_beta[it] = k_f[it] * beta
        vb[it] = load(v_ref, it).astype(F32) * beta
        k_bd[it] = block_diag(k_f[it])

    a, a_qk = {}, {}
    for it in items:
        lhs = jnp.concatenate([k_beta[it].astype(BF16), q_f[it].astype(BF16)], axis=0)
        kk = _dot_nt(lhs, k_bd[it])
        a[it] = jnp.where(tri_strict, kk[:CHUNK] * decay[it], 0.0)
        a_qk[it] = jnp.where(tri_incl, kk[CHUNK:] * decay[it], 0.0)

    t = {it: eye - a[it] for it in items}
    pw = {it: pair_mm(a[it], a[it]) for it in items}
    for _ in range(4):
        for it in items:
            both = pair_mm(jnp.concatenate([t[it], pw[it]], axis=0), pw[it])
            t[it] = t[it] + both[:CHUNK]
            pw[it] = both[CHUNK:]
    for it in items:
        t[it] = t[it] + pair_mm(t[it], pw[it])

    u, w, k_g, q_g, e_last = {}, {}, {}, {}, {}
    for it in items:
        e_col = jnp.exp(g_col[it])
        rhs = jnp.concatenate([block_diag(vb[it]), block_diag(k_beta[it] * e_col)], axis=1)
        uw = _dot(t[it].astype(BF16), rhs)
        u[it], w[it] = uw[:, :GDN_GW], uw[:, GDN_GW:]
        g_last = g_col[it][CHUNK - 1:CHUNK, :]
        k_g[it] = (k_f[it] * jnp.exp(g_last - g_col[it])).astype(BF16)
        q_g[it] = q_f[it] * e_col
        e_last[it] = jnp.exp(g_last)

    state = {(bi, p): s_scr[bi, p] for bi in range(n_seq) for p in range(n_pairs)}
    sp = lambda it: (it[0], it[2])
    for c in range(n_chunks):
        its = [(bi, c, p) for bi in range(n_seq) for p in range(n_pairs)]
        ws_qs = [_dot(jnp.concatenate([w[it], q_g[it]], axis=0).astype(BF16), state[sp(it)].astype(BF16))
                 for it in its]
        v_new = [u[it] - sq[:CHUNK] for it, sq in zip(its, ws_qs)]
        for it, sq, vn in zip(its, ws_qs, v_new):
            o_scr[it[0], rows_of(c), cols_of(it[2])] = sq[CHUNK:] + pair_mm(a_qk[it], vn)
        for it, vn in zip(its, v_new):
            upd = _dot_tn(k_g[it], vn.astype(BF16))
            state[sp(it)] = state[sp(it)] * e_last[it] + jnp.where(bd_mask, upd, 0.0)
    for (bi, p), val in state.items():
        s_scr[bi, p] = val

    for bi in range(n_seq):
        o = o_scr[bi]
        ss = _group_sumsq(o, g512_ref)
        y = o * lax.rsqrt(ss * (1.0 / HEAD_DIM) + EPS) * gn_ref[...]
        o_ref[bi] = (y * z_ref[bi].astype(F32)).astype(BF16)

    @pl.when(j == pl.num_programs(0) - 1)
    def _():
        sfin_ref[...] = s_scr[...]


def _gated_delta(q, k, v, z, gb, beta, gn_rep, g512, s0, tbg):
    b, s, _ = q.shape
    nb = s // tbg
    state_shape = (b,) + s0.shape
    tok = pl.BlockSpec((b, tbg, MIX_HALF), lambda j: (0, j, 0))
    full = lambda shape: pl.BlockSpec(shape, lambda j: (0,) * len(shape))
    return pl.pallas_call(
        functools.partial(_gdn_kernel, n_chunks=tbg // CHUNK),
        grid=(nb,),
        in_specs=[tok] * 6 + [full((1, MIX_HALF)), full(g512.shape), full(s0.shape)],
        out_specs=[tok, pl.BlockSpec(state_shape, lambda j: (0, 0, 0, 0))],
        out_shape=[jax.ShapeDtypeStruct((b, s, MIX_HALF), BF16), jax.ShapeDtypeStruct(state_shape, F32)],
        scratch_shapes=[pltpu.VMEM(state_shape, F32), pltpu.VMEM((b, tbg, MIX_HALF), F32)],
        compiler_params=pltpu.CompilerParams(dimension_semantics=("arbitrary",),
                                             vmem_limit_bytes=VMEM_LIMIT),
        name="gated_delta",
    )(q, k, v, z, gb, beta, gn_rep, g512, s0)


def _out_router_kernel(x_ref, oa_ref, ob_ref, wo_ref, gain_ref, wr_ref, br_ref, before_ref,
                       h_ref, t_ref, rows_ref, cols_ref, cnt_ref):
    h = x_ref[0] + _dot(jnp.concatenate([oa_ref[0], ob_ref[0]], axis=1), wo_ref[...])
    h_ref[0] = h
    t = _rms_scale(h) * gain_ref[...]
    t_hi = t.astype(BF16)
    t_ref[0] = t_hi
    t_lo = (t - t_hi.astype(F32)).astype(BF16)
    wr = wr_ref[...]
    hi_part = _dot(t_hi, wr)
    logits = hi_part[:, :ROUTE_LANES] + hi_part[:, ROUTE_LANES:] + _dot(t_lo, wr)[:, :ROUTE_LANES] + br_ref[...]

    lt = logits.T
    tb = lt.shape[1]
    row8 = lax.broadcasted_iota(jnp.int32, (EXPERTS_PER_GROUP, tb), 0)
    col_max = lambda v: jnp.max(v, axis=0, keepdims=True)
    col_sum = lambda v: jnp.sum(v, axis=0, keepdims=True)
    first_argmax = lambda v, vmax: jnp.min(jnp.where(v == vmax, row8, EXPERTS_PER_GROUP), axis=0, keepdims=True)

    gl = jnp.where(row8 < N_GROUPS, lt[:EXPERTS_PER_GROUP], NEG_BIG)
    gmax = col_max(gl)
    gsel = first_argmax(gl, gmax)
    psel = 1.0 / col_sum(jnp.exp(gl - gmax))
    el = jnp.zeros((EXPERTS_PER_GROUP, tb), F32)
    for g in range(N_GROUPS):
        lo = EXPERT_ROW0 + EXPERTS_PER_GROUP * g
        el = jnp.where(gsel == g, lt[lo:lo + EXPERTS_PER_GROUP], el)
    m1 = col_max(el)
    i1 = first_argmax(el, m1)
    el2 = jnp.where(row8 == i1, NEG_BIG, el)
    m2 = col_max(el2)
    i2 = first_argmax(el2, m2)
    denom = col_sum(jnp.exp(el - m1))
    p1 = 1.0 / denom
    p2 = jnp.exp(m2 - m1) / denom
    w1 = p1 / (p1 + p2) * psel
    w2 = p2 / (p1 + p2) * psel
    e1 = gsel * EXPERTS_PER_GROUP + i1
    e2 = gsel * EXPERTS_PER_GROUP + i2

    row_e = lax.broadcasted_iota(jnp.int32, (N_EXPERTS, tb), 0)
    hot1 = (row_e == e1).astype(F32)
    hot2 = (row_e == e2).astype(F32)
    both = hot1 + hot2
    n_gran = jnp.ceil(jnp.sum(both, axis=1, keepdims=True) * (1.0 / SLOT_GRAN)) * SLOT_GRAN
    ei = lax.broadcasted_iota(jnp.int32, (N_EXPERTS, N_EXPERTS), 0)
    ej = lax.broadcasted_iota(jnp.int32, (N_EXPERTS, N_EXPERTS), 1)
    seg_start = _dot((ei > ej).astype(F32), jnp.broadcast_to(n_gran, (N_EXPERTS, SLAB)), precision=HIGHEST)[:, :1]
    earlier = _dot(both.astype(BF16), before_ref[...])
    where_to = earlier + seg_start
    pos1 = col_sum(hot1 * where_to)
    pos2 = col_sum(hot2 * where_to)

    info = jnp.concatenate([pos1, pos2, w1, w2, e1.astype(F32), e2.astype(F32),
                            jnp.zeros((SLAB - 6, tb), F32)], axis=0)
    rows_ref[0] = info[:8]
    cols_ref[0] = info.T
    cnt_ref[0] = jnp.broadcast_to(n_gran, (N_EXPERTS, SLAB))


def _out_router(x, oa, ob, w_o, gain, w_route, b_route, before, tb):
    b, s, d = x.shape
    nb = s // tb
    full = lambda shape: pl.BlockSpec(shape, lambda i, j: (0,) * len(shape))
    tokd = pl.BlockSpec((1, tb, d), lambda i, j: (i, j, 0))
    tokh = pl.BlockSpec((1, tb, MIX_HALF), lambda i, j: (i, j, 0))
    return pl.pallas_call(
        _out_router_kernel,
        grid=(b, nb),
        in_specs=[tokd, tokh, tokh, full(w_o.shape), full((1, d)), full(w_route.shape), full((1, ROUTE_LANES)),
                  full(before.shape)],
        out_specs=[tokd, tokd, pl.BlockSpec((1, 8, tb), lambda i, j: (i, 0, j)),
                   pl.BlockSpec((1, tb, SLAB), lambda i, j: (i, j, 0)),
                   pl.BlockSpec((1, N_EXPERTS, SLAB), lambda i, j: (i * nb + j, 0, 0))],
        out_shape=[jax.ShapeDtypeStruct((b, s, d), F32), jax.ShapeDtypeStruct((b, s, d), BF16),
                   jax.ShapeDtypeStruct((b, 8, s), F32), jax.ShapeDtypeStruct((b, s, SLAB), F32),
                   jax.ShapeDtypeStruct((b * nb, N_EXPERTS, SLAB), F32)],
        compiler_params=pltpu.CompilerParams(dimension_semantics=("parallel", "parallel"),
                                             vmem_limit_bytes=VMEM_LIMIT),
        name="out_router",
    )(x, oa, ob, w_o, gain, w_route, b_route, before)


def _granule_copies(table_ref, block, buf, local_ref, global_ref, sem, to_global, n_gran):
    copies = []
    for g in range(n_gran):
        loc = local_ref.at[buf, pl.ds(g * SLOT_GRAN, SLOT_GRAN)]
        glob = global_ref.at[pl.ds(pl.multiple_of(table_ref[block, g], SLOT_GRAN), SLOT_GRAN)]
        copies.append(pltpu.make_async_copy(loc, glob, sem.at[buf]) if to_global
                      else pltpu.make_async_copy(glob, loc, sem.at[buf]))
    return copies


def _dispatch_kernel(gdst_ref, t_ref, rows_ref, xg_init_ref, xg_ref, xs_scr, sem, *, n_slots):
    del xg_init_ref
    blk = pl.program_id(0)
    last = pl.num_programs(0) - 1
    buf = blk % 2
    tb = t_ref.shape[0]
    n_gran = n_slots // SLOT_GRAN
    info = rows_ref[0]
    slot = lax.broadcasted_iota(jnp.int32, (n_slots, tb), 0).astype(F32)
    onehot = jnp.where((slot == info[0:1]) | (slot == info[1:2]), 1.0, 0.0).astype(BF16)
    xs_scr[buf] = _dot(onehot, t_ref[...]).astype(BF16)
    for copy in _granule_copies(gdst_ref, blk, buf, xs_scr, xg_ref, sem, True, n_gran):
        copy.start()

    @pl.when(blk > 0)
    def _():
        for copy in _granule_copies(gdst_ref, blk - 1, 1 - buf, xs_scr, xg_ref, sem, True, n_gran):
            copy.wait()

    @pl.when(blk == last)
    def _():
        for copy in _granule_copies(gdst_ref, blk, buf, xs_scr, xg_ref, sem, True, n_gran):
            copy.wait()


def _dispatch(granule_dst, t, rows, xg_init, tb, n_slots):
    n, d = t.shape
    nblk = n // tb
    nb = rows.shape[2] // tb
    grid_spec = pltpu.PrefetchScalarGridSpec(
        num_scalar_prefetch=1,
        grid=(nblk,),
        in_specs=[pl.BlockSpec((tb, d), lambda i, gd: (i, 0)),
                  pl.BlockSpec((1, 8, tb), lambda i, gd: (i // nb, 0, i % nb)),
                  pl.BlockSpec(memory_space=pl.ANY)],
        out_specs=pl.BlockSpec(memory_space=pl.ANY),
        scratch_shapes=[pltpu.VMEM((2, n_slots, d), BF16), pltpu.SemaphoreType.DMA((2,))],
    )
    return pl.pallas_call(
        functools.partial(_dispatch_kernel, n_slots=n_slots),
        grid_spec=grid_spec,
        out_shape=jax.ShapeDtypeStruct(xg_init.shape, BF16),
        input_output_aliases={3: 0},
        compiler_params=pltpu.CompilerParams(dimension_semantics=("arbitrary",), vmem_limit_bytes=VMEM_LIMIT),
        name="moe_dispatch",
    )(granule_dst, t, rows, xg_init)


def _experts_kernel(tile_expert_ref, n_tiles_ref, x_ref, wg_ref, wu_ref, wd_ref, y_ref):
    del tile_expert_ref

    @pl.when(pl.program_id(0) < n_tiles_ref[0])
    def _():
        x = x_ref[...]
        act = _silu(_dot(x, wg_ref[0].astype(BF16))) * _dot(x, wu_ref[0].astype(BF16))
        y_ref[...] = _dot(act.astype(BF16), wd_ref[0].astype(BF16)).astype(BF16)


def _experts(tile_expert, n_tiles, xg, w_gate, w_up, w_down, tile):
    rows, d = xg.shape
    used = lambda i, te, nt: jnp.minimum(i, nt[0] - 1)
    wspec = lambda shape: pl.BlockSpec((1,) + shape, lambda i, te, nt: (te[used(i, te, nt)], 0, 0))
    grid_spec = pltpu.PrefetchScalarGridSpec(
        num_scalar_prefetch=2,
        grid=(tile_expert.shape[0],),
        in_specs=[pl.BlockSpec((tile, d), lambda i, te, nt: (used(i, te, nt), 0)),
                  wspec((d, D_EXPERT)), wspec((d, D_EXPERT)), wspec((D_EXPERT, d))],
        out_specs=pl.BlockSpec((tile, d), lambda i, te, nt: (used(i, te, nt), 0)),
    )
    return pl.pallas_call(
        _experts_kernel,
        grid_spec=grid_spec,
        out_shape=jax.ShapeDtypeStruct((rows, d), BF16),
        input_output_aliases={2: 0},
        compiler_params=pltpu.CompilerParams(dimension_semantics=("arbitrary",), vmem_limit_bytes=VMEM_LIMIT),
        name="moe_experts",
    )(tile_expert, n_tiles, xg, w_gate, w_up, w_down)


def _combine_kernel(gsrc_ref, yg_ref, cols_ref, h_ref, gain_ref, o_ref, ys_scr, sem, *, n_slots):
    blk = pl.program_id(0)
    last = pl.num_programs(0) - 1
    buf = blk % 2
    tb = h_ref.shape[0]
    n_gran = n_slots // SLOT_GRAN

    @pl.when(blk == 0)
    def _():
        for copy in _granule_copies(gsrc_ref, 0, 0, ys_scr, yg_ref, sem, False, n_gran):
            copy.start()

    @pl.when(blk < last)
    def _():
        for copy in _granule_copies(gsrc_ref, blk + 1, 1 - buf, ys_scr, yg_ref, sem, False, n_gran):
            copy.start()

    info = cols_ref[...]
    slot = lax.broadcasted_iota(jnp.int32, (tb, n_slots), 1).astype(F32)
    weights = (jnp.where(slot == info[:, 0:1], info[:, 2:3], 0.0)
               + jnp.where(slot == info[:, 1:2], info[:, 3:4], 0.0)).astype(BF16)
    for copy in _granule_copies(gsrc_ref, blk, buf, ys_scr, yg_ref, sem, False, n_gran):
        copy.wait()
    y = _dot(weights, ys_scr[buf])
    o_ref[...] = _rms_scale(h_ref[...] + y) * gain_ref[...]


def _combine(granule_dst, yg, cols, h, gain, tb, n_slots):
    n, d = h.shape
    grid_spec = pltpu.PrefetchScalarGridSpec(
        num_scalar_prefetch=1,
        grid=(n // tb,),
        in_specs=[pl.BlockSpec(memory_space=pl.ANY),
                  pl.BlockSpec((tb, SLAB), lambda i, gd: (i, 0)),
                  pl.BlockSpec((tb, d), lambda i, gd: (i, 0)),
                  pl.BlockSpec((1, d), lambda i, gd: (0, 0))],
        out_specs=pl.BlockSpec((tb, d), lambda i, gd: (i, 0)),
        scratch_shapes=[pltpu.VMEM((2, n_slots, d), BF16), pltpu.SemaphoreType.DMA((2,))],
    )
    return pl.pallas_call(
        functools.partial(_combine_kernel, n_slots=n_slots),
        grid_spec=grid_spec,
        out_shape=jax.ShapeDtypeStruct((n, d), F32),
        compiler_params=pltpu.CompilerParams(dimension_semantics=("arbitrary",), vmem_limit_bytes=VMEM_LIMIT),
        name="moe_combine",
    )(granule_dst, yg, cols, h, gain)


def _regroup_plan(n_gran_be, n_slots, tile, n_tiles_max):
    cnt = n_gran_be.astype(jnp.int32)
    region = (jnp.sum(cnt, axis=0) + tile - 1) // tile * tile
    region_end = jnp.cumsum(region)
    first_row = (region_end - region)[None, :] + jnp.cumsum(cnt, axis=0) - cnt
    seg_end = jnp.cumsum(cnt, axis=1)
    seg_start = seg_end - cnt
    g_row = jnp.arange(n_slots // SLOT_GRAN, dtype=jnp.int32) * SLOT_GRAN
    owned = ((g_row[None, :, None] >= seg_start[:, None, :]) & (g_row[None, :, None] < seg_end[:, None, :]))
    in_use = g_row[None, :] < seg_end[:, -1:]
    row = g_row[None, :] + jnp.sum(jnp.where(owned, (first_row - seg_start)[:, None, :], 0), axis=-1)
    zero_row = n_tiles_max * tile
    spill_row = zero_row + SLOT_GRAN + (jnp.arange(cnt.shape[0], dtype=jnp.int32) % 2)[:, None] * n_slots + g_row
    dispatch_dst = jnp.where(in_use, row, spill_row).astype(jnp.int32)
    combine_src = jnp.where(in_use, row, zero_row).astype(jnp.int32)
    tile_row = jnp.arange(n_tiles_max, dtype=jnp.int32) * tile
    tile_expert = jnp.minimum(jnp.sum(tile_row[:, None] >= region_end[None, :], axis=-1), N_EXPERTS - 1)
    n_tiles = (region_end[-1] // tile).astype(jnp.int32)[None]
    return dispatch_dst, combine_src, tile_expert.astype(jnp.int32), n_tiles


def _rope_tables(first, length):
    pos = jnp.arange(first, first + length, dtype=F32)
    inv_freq = ROPE_THETA ** (-jnp.arange(0, HEAD_DIM, 2, dtype=F32) / HEAD_DIM)
    ang = pos[:, None] * inv_freq[None, :]
    cos, sin = jnp.cos(ang), jnp.sin(ang)
    zero = jnp.zeros_like(sin)
    cos_t = jnp.tile(cos, (1, 4))
    sina_t = jnp.tile(jnp.concatenate([-sin, zero], axis=1), (1, 2))
    sinb_t = jnp.tile(jnp.concatenate([zero, sin], axis=1), (1, 2))
    return cos_t, sina_t, sinb_t


def _block(total, want):
    blk = min(total, want)
    assert total % blk == 0, (total, blk)
    return blk


def kernel(x, meta, norm_mix, w_in, lambda_q1, lambda_k1, lambda_q2, lambda_k2, diff_norm, conv_w, a_log, dt_bias,
           gdn_norm, w_out, norm_ffn, w_group, b_group, w_router, b_router, w_gate, w_up, w_down, norm_final):
    b, s, d = x.shape
    assert d == D_MODEL and meta.shape == (N_META, D_MODEL) and s % CHUNK == 0
    assert norm_mix.shape[0] == 1, "single-layer block"
    l = 0

    w = w_in[l]
    w_a = w[:, :3 * MIX_HALF].astype(BF16)
    w_b = w[:, 3 * MIX_HALF:6 * MIX_HALF].astype(BF16)
    w_z = w[:, 6 * MIX_HALF:7 * MIX_HALF].astype(BF16)
    c_ab = 7 * MIX_HALF
    lane_pad = lambda a: jnp.pad(a, [(0, 0)] * (a.ndim - 1) + [(0, SLAB - a.shape[-1])])
    w_g = lane_pad(w[:, c_ab:c_ab + 2 * GDN_HEADS]).astype(BF16)
    alog_rep = lane_pad(a_log[l])[None]
    dtb_rep = lane_pad(dt_bias[l])[None]
    gn_rep = jnp.tile(gdn_norm[l], GDN_HEADS)[None]
    g512 = (jnp.arange(MIX_HALF)[:, None] // HEAD_DIM == jnp.arange(MIX_HALF)[None, :] // HEAD_DIM).astype(BF16)
    gain_mix = norm_mix[l][None]
    w_o = w_out[l].astype(BF16)
    route_pad = lambda g, e: jnp.concatenate(
        [g, jnp.zeros(g.shape[:-1] + (EXPERT_ROW0 - N_GROUPS,), F32), e,
         jnp.zeros(g.shape[:-1] + (ROUTE_LANES - EXPERT_ROW0 - N_EXPERTS,), F32)], axis=-1)
    w_route_f = route_pad(w_group[l], w_router[l])
    w_route_hi = w_route_f.astype(BF16)
    w_route = jnp.concatenate([w_route_hi, (w_route_f - w_route_hi.astype(F32)).astype(BF16)], axis=1)
    b_route = route_pad(b_group[l], b_router[l])[None]
    lam_vecs = [v[l][None] for v in (lambda_q1, lambda_k1, lambda_q2, lambda_k2)]
    rope_meta = _rope_tables(0, N_META)
    rope_frames = _rope_tables(N_META, s)

    meta3 = meta[None]
    _, ka_m, va_m = _proj_attn(meta3, gain_mix, w_a, *rope_meta, N_META)
    zero_halo = jnp.zeros((1, 1, N_META, d), F32)
    gdn_m = _proj_gdn(meta3, zero_halo, gain_mix, w_b, w_z, w_g, conv_w[l], alog_rep, dtb_rep, g512, N_META)
    pad_m = lambda a: jnp.pad(a, ((0, 0), (0, CHUNK - N_META), (0, 0)))
    s_zero = jnp.zeros((GDN_HEADS // GDN_GROUP, GDN_GW, GDN_GW), F32)
    _, s_meta = _gated_delta(*[pad_m(a) for a in gdn_m], gn_rep, g512, s_zero, CHUNK)

    tb_a = _block(s, PROJ_ATTN_ROWS)
    qa, ka, va = _proj_attn(x, gain_mix, w_a, *rope_frames, tb_a)
    tb_g = _block(s, PROJ_GDN_ROWS)
    nb_g = s // tb_g
    tails = x.reshape(b, nb_g, tb_g, d)[:, :-1, tb_g - N_META:, :]
    halo = jnp.concatenate([jnp.broadcast_to(meta[None, None], (b, 1, N_META, d)), tails], axis=1)
    gdn_f = _proj_gdn(x, halo, gain_mix, w_b, w_z, w_g, conv_w[l], alog_rep, dtb_rep, g512, tb_g)

    oa = _diff_attention(qa, ka, va, ka_m, va_m, *lam_vecs, diff_norm[l][None],
                         _block(s, ATTN_QUERY_ROWS), ATTN_KEY_BLOCK)
    ob, _ = _gated_delta(*gdn_f, gn_rep, g512, s_meta[0], _block(s, GDN_ROWS))

    tb_r = _block(s, MOE_BLOCK)
    tok = jnp.arange(tb_r)
    before = (tok[:, None] < tok[None, :]).astype(BF16)
    h1, t, rows, cols, cnt = _out_router(x, oa, ob, w_o, norm_ffn[l][None], w_route, b_route, before, tb_r)

    n = b * s
    nblk = n // tb_r
    n_slots = 2 * tb_r + N_EXPERTS * SLOT_GRAN
    rows_max = 2 * n + (SLOT_GRAN - 1) * N_EXPERTS * nblk + (MOE_TILE - 1) * N_EXPERTS
    n_tiles_max = -(-rows_max // MOE_TILE)
    dispatch_dst, combine_src, tile_expert, n_tiles = _regroup_plan(cnt[:, :, 0], n_slots, MOE_TILE, n_tiles_max)
    buf_rows = n_tiles_max * MOE_TILE + SLOT_GRAN + 2 * n_slots
    xg = _dispatch(dispatch_dst, t.reshape(n, d), rows, jnp.zeros((buf_rows, d), BF16), tb_r, n_slots)
    yg = _experts(tile_expert, n_tiles, xg, w_gate[l], w_up[l], w_down[l], MOE_TILE)
    out = _combine(combine_src, yg, cols.reshape(n, SLAB), h1.reshape(n, d), norm_final[None], tb_r, n_slots)
    return out.reshape(b, s, d)
```

```python
import functools
import math

import jax
import jax.numpy as jnp
from jax import lax
from jax.experimental import pallas as pl
from jax.experimental.pallas import tpu as pltpu

F32 = jnp.float32
BF16 = jnp.bfloat16
HIGHEST = lax.Precision.HIGHEST

D_MODEL = 1024
N_META = 16
CHUNK = 64
EPS = 1e-6
ROPE_THETA = 10000.0
HEAD_DIM = 64
SLAB = 128
DA_HEADS = 4
GDN_HEADS = 8
MIX_HALF = 512
GDN_GROUP = 2
GDN_GW = GDN_GROUP * HEAD_DIM
N_GROUPS = 4
EXPERTS_PER_GROUP = 8
N_EXPERTS = 32
D_EXPERT = 256
CONV_K = 4
LAM_INIT = 0.8 - 0.6 * math.exp(-0.3 * 0)
ROUTE_LANES = 128
EXPERT_ROW0 = 8
SLOT_GRAN = 16
MOE_BLOCK = 512
MOE_TILE = 512
ROW_STRIP = 64
PROJ_ATTN_ROWS = 1024
PROJ_GDN_ROWS = 512
ATTN_QUERY_ROWS = 2048
ATTN_KEY_BLOCK = 512
GDN_ROWS = 256
NEG_BIG = -1e30
LOG2E = math.log2(math.e)
VMEM_LIMIT = 56 * 1024 * 1024


def _dot(a, b, precision=None):
    return jnp.dot(a, b, preferred_element_type=F32, precision=precision)


def _dot_nt(a, b):
    return lax.dot_general(a, b, (((1,), (1,)), ((), ())), preferred_element_type=F32)


def _rms_scale(x):
    return x * lax.rsqrt(jnp.mean(x * x, axis=-1, keepdims=True) + EPS)


def _silu(x):
    return x * jax.nn.sigmoid(x)


def _group_sumsq(x, g_ref):
    return _dot((x * x).astype(BF16), g_ref[...])


def _proj_attn_kernel(x_ref, gain_ref, w_ref, cos_ref, sina_ref, sinb_ref, qa_ref, ka_ref, va_ref):
    u = (_rms_scale(x_ref[0]) * gain_ref[...]).astype(BF16)
    proj = _dot(u, w_ref[...])
    va_ref[0] = proj[:, 2 * MIX_HALF:].astype(BF16)
    cos, sina, sinb = cos_ref[...], sina_ref[...], sinb_ref[...]
    for s in range(2 * DA_HEADS):
        xs = proj[:, SLAB * s:SLAB * (s + 1)]
        r = xs * cos + pltpu.roll(xs, SLAB - 32, 1) * sina + pltpu.roll(xs, 32, 1) * sinb
        if s < DA_HEADS:
            qa_ref[0, :, SLAB * s:SLAB * (s + 1)] = (r * (HEAD_DIM ** -0.5 * LOG2E)).astype(BF16)
        else:
            t = s - DA_HEADS
            ka_ref[0, :, SLAB * t:SLAB * (t + 1)] = r.astype(BF16)


def _proj_attn(x, gain, w_a, cos, sina, sinb, tb):
    b, s, d = x.shape
    nb = s // tb
    slab_out = jax.ShapeDtypeStruct((b, s, MIX_HALF), BF16)
    full = lambda shape: pl.BlockSpec(shape, lambda i, j: (0,) * len(shape))
    tok = pl.BlockSpec((1, tb, MIX_HALF), lambda i, j: (i, j, 0))
    tab = pl.BlockSpec((tb, SLAB), lambda i, j: (j, 0))
    return pl.pallas_call(
        _proj_attn_kernel,
        grid=(b, nb),
        in_specs=[pl.BlockSpec((1, tb, d), lambda i, j: (i, j, 0)), full((1, d)), full(w_a.shape), tab, tab, tab],
        out_specs=[tok, tok, tok],
        out_shape=[slab_out, slab_out, slab_out],
        compiler_params=pltpu.CompilerParams(dimension_semantics=("parallel", "parallel"),
                                             vmem_limit_bytes=VMEM_LIMIT),
        name="proj_attn",
    )(x, gain, w_a, cos, sina, sinb)


def _proj_gdn_kernel(x_ref, halo_ref, gain_ref, wb_ref, wz_ref, wg_ref, convw_ref, alog_ref, dtb_ref, g_ref,
                     q_ref, k_ref, v_ref, z_ref, gb_ref, beta_ref, pb_scr):
    halo_rows = halo_ref.shape[2]
    xe = jnp.concatenate([halo_ref[0, 0], x_ref[0]], axis=0)
    ue = (_rms_scale(xe) * gain_ref[...]).astype(BF16)
    u = ue[halo_rows:]
    cw = convw_ref[...]
    outs = (q_ref, k_ref, v_ref)
    for part in range(3):
        cols = slice(MIX_HALF * part, MIX_HALF * (part + 1))
        pb_scr[...] = _dot(ue, wb_ref[:, cols])
        c = cw[:, cols]
        tb = pb_scr.shape[0] - halo_rows
        y = pb_scr[halo_rows:, :] * c[CONV_K - 1:CONV_K]
        for back in range(1, CONV_K):
            tap = CONV_K - 1 - back
            y = y + pb_scr[pl.ds(halo_rows - back, tb), :] * c[tap:tap + 1]
        y = _silu(y)
        if part < 2:
            y = y * lax.rsqrt(_group_sumsq(y, g_ref) + EPS)
            if part == 0:
                y = y * (HEAD_DIM ** -0.5)
        outs[part][0] = y.astype(BF16)
    z_ref[0] = _silu(_dot(u, wz_ref[...])).astype(BF16)
    gates = _dot(u, wg_ref[...])
    ab = gates + dtb_ref[...]
    softplus = jnp.maximum(ab, 0.0) + jnp.log1p(jnp.exp(-jnp.abs(ab)))
    g = -jnp.exp(alog_ref[...]) * softplus
    beta = jax.nn.sigmoid(gates)
    tb = g.shape[0]
    first_head = lax.broadcasted_iota(jnp.int32, (tb, SLAB), 1) < HEAD_DIM
    spread = lambda v, h: jnp.broadcast_to(v[:, h:h + 1], (tb, SLAB))
    for p in range(GDN_HEADS // 2):
        cols = slice(SLAB * p, SLAB * (p + 1))
        gb_ref[0, :, cols] = jnp.where(first_head, spread(g, 2 * p), spread(g, 2 * p + 1))
        beta_ref[0, :, cols] = jnp.where(first_head, spread(beta, GDN_HEADS + 2 * p),
                                         spread(beta, GDN_HEADS + 2 * p + 1)).astype(BF16)


def _proj_gdn(x, halo, gain, w_b, w_z, w_g, conv_w, alog_rep, dtb_rep, g512, tb):
    b, s, d = x.shape
    nb = s // tb
    full = lambda shape: pl.BlockSpec(shape, lambda i, j: (0,) * len(shape))
    tok = pl.BlockSpec((1, tb, MIX_HALF), lambda i, j: (i, j, 0))
    bf = jax.ShapeDtypeStruct((b, s, MIX_HALF), BF16)
    return pl.pallas_call(
        _proj_gdn_kernel,
        grid=(b, nb),
        in_specs=[pl.BlockSpec((1, tb, d), lambda i, j: (i, j, 0)),
                  pl.BlockSpec((1, 1) + halo.shape[2:], lambda i, j: (i, j, 0, 0)),
                  full((1, d)), full(w_b.shape), full(w_z.shape), full(w_g.shape), full(conv_w.shape),
                  full((1, SLAB)), full((1, SLAB)), full(g512.shape)],
        out_specs=[tok] * 6,
        out_shape=[bf, bf, bf, bf, jax.ShapeDtypeStruct((b, s, MIX_HALF), F32), bf],
        scratch_shapes=[pltpu.VMEM((halo.shape[2] + tb, MIX_HALF), F32)],
        compiler_params=pltpu.CompilerParams(dimension_semantics=("parallel", "parallel"),
                                             vmem_limit_bytes=VMEM_LIMIT),
        name="proj_gdn",
    )(x, halo, gain, w_b, w_z, w_g, conv_w, alog_rep, dtb_rep, g512)


def _attn_kernel(q_ref, k_ref, v_ref, km_ref, vm_ref, lq1_ref, lk1_ref, lq2_ref, lk2_ref, dn_ref, o_ref,
                 qq_scr, m_scr, acc_scr, st_scr, *, qb, cw):
    i = pl.program_id(2)
    n_col = 2 * qb // cw
    per_map = qb // cw
    q = q_ref[0]
    lane = lax.broadcasted_iota(jnp.int32, (qb, SLAB), 1)
    zero = jnp.zeros_like(q)
    qq_scr[...] = jnp.concatenate([jnp.where(lane < HEAD_DIM, q, zero), jnp.where(lane < HEAD_DIM, zero, q)], axis=0)

    def with_ones(v_blk):
        return jnp.concatenate([v_blk, jnp.ones_like(v_blk)], axis=1)

    v1_meta = with_ones(vm_ref[0])
    for c in range(n_col):
        rs = slice(cw * c, cw * (c + 1))
        s = _dot_nt(qq_scr[rs, :], km_ref[0])
        m_new = jnp.max(s, axis=1, keepdims=True)
        acc_scr[rs, :] = _dot(jnp.exp2((s - m_new).astype(BF16)), v1_meta)
        m_scr[rs, :] = jnp.broadcast_to(m_new, (cw, SLAB))

    def scores_into(slot, start, c):
        st_scr[slot] = _dot_nt(qq_scr[cw * c:cw * (c + 1), :], k_ref[0, pl.ds(start, cw), :])

    def softmax_pv(slot, v1, c, masked):
        alphas = []
        for t in range(cw // ROW_STRIP):
            ls = slice(ROW_STRIP * t, ROW_STRIP * (t + 1))
            gs = slice(cw * c + ROW_STRIP * t, cw * c + ROW_STRIP * (t + 1))
            s = st_scr[slot, ls, :]
            if masked:
                q_chunk = (lax.broadcasted_iota(jnp.int32, s.shape, 0) + ROW_STRIP * t) // CHUNK
                k_chunk = lax.broadcasted_iota(jnp.int32, s.shape, 1) // CHUNK
                s = jnp.where(k_chunk <= q_chunk, s, NEG_BIG)
            parts = [s[:, SLAB * k:SLAB * (k + 1)] for k in range(cw // SLAB)]
            lane_max = functools.reduce(jnp.maximum, parts)
            m_prev = m_scr[gs, :]
            m_new = jnp.maximum(m_prev, jnp.max(lane_max, axis=1, keepdims=True))
            for k, part in enumerate(parts):
                st_scr[slot, ls, SLAB * k:SLAB * (k + 1)] = jnp.exp2(part - m_new)
            alphas.append(jnp.exp2(m_prev - m_new))
            m_scr[gs, :] = m_new
        rs = slice(cw * c, cw * (c + 1))
        alpha = jnp.concatenate(alphas, axis=0)
        pv = _dot(st_scr[slot].astype(BF16), v1)
        acc_scr[rs, :] = acc_scr[rs, :] * jnp.concatenate([alpha, alpha], axis=1) + pv

    parity = [0]
    scores_into(0, 0, 0)

    def key_block(start, groups, masked_groups, following):
        v1 = with_ones(v_ref[0, pl.ds(start, cw), :])
        for idx, c in enumerate(groups):
            slot = parity[0]
            if idx + 1 < len(groups):
                scores_into(1 - slot, start, groups[idx + 1])
            elif following is not None:
                scores_into(1 - slot, *following)
            softmax_pv(slot, v1, c, c in masked_groups)
            parity[0] = 1 - slot

    all_groups = list(range(n_col))

    def full_blocks(j, carry):
        for d in range(per_map):
            start = pl.multiple_of((per_map * j + d) * cw, cw)
            key_block(start, all_groups, (), (start + cw, 0))
        return carry

    lax.fori_loop(0, i, full_blocks, 0)
    for d in range(per_map):
        start = pl.multiple_of((per_map * i + d) * cw, cw)
        groups = [c for c in all_groups if c % per_map >= d]
        following = (start + cw, d + 1) if d + 1 < per_map else None
        key_block(start, groups, [c for c in groups if c % per_map == d], following)

    acc = acc_scr[...]
    o1 = acc[:qb, :SLAB] / acc[:qb, SLAB:]
    o2 = acc[qb:, :SLAB] / acc[qb:, SLAB:]
    lam = (jnp.exp(jnp.sum(lq1_ref[...] * lk1_ref[...], axis=1, keepdims=True))
           - jnp.exp(jnp.sum(lq2_ref[...] * lk2_ref[...], axis=1, keepdims=True)) + LAM_INIT)
    o = o1 - lam * o2
    o_ref[0] = (_rms_scale(o) * dn_ref[...] * (1.0 - LAM_INIT)).astype(BF16)


def _diff_attention(qa, ka, va, ka_meta, va_meta, lq1, lk1, lq2, lk2, diff_norm, qb, cw):
    b, s, _ = qa.shape
    nq = s // qb
    vec = lambda n: pl.BlockSpec((1, n), lambda bi, h, i: (0, 0))
    seq = pl.BlockSpec((1, s, SLAB), lambda bi, h, i: (bi, 0, h))
    meta = pl.BlockSpec((1, N_META, SLAB), lambda bi, h, i: (0, 0, h))
    return pl.pallas_call(
        functools.partial(_attn_kernel, qb=qb, cw=cw),
        grid=(b, DA_HEADS, nq),
        in_specs=[pl.BlockSpec((1, qb, SLAB), lambda bi, h, i: (bi, i, h)), seq, seq, meta, meta,
                  vec(HEAD_DIM), vec(HEAD_DIM), vec(HEAD_DIM), vec(HEAD_DIM), vec(SLAB)],
        out_specs=pl.BlockSpec((1, qb, SLAB), lambda bi, h, i: (bi, i, h)),
        out_shape=jax.ShapeDtypeStruct((b, s, MIX_HALF), BF16),
        scratch_shapes=[pltpu.VMEM((2 * qb, SLAB), BF16), pltpu.VMEM((2 * qb, SLAB), F32),
                        pltpu.VMEM((2 * qb, 2 * SLAB), F32), pltpu.VMEM((2, cw, cw), F32)],
        compiler_params=pltpu.CompilerParams(dimension_semantics=("parallel", "parallel", "arbitrary"),
                                             vmem_limit_bytes=VMEM_LIMIT),
        name="diff_attention",
    )(qa, ka, va, ka_meta, va_meta, lq1, lk1, lq2, lk2, diff_norm)


def _gdn_kernel(q_ref, k_ref, v_ref, z_ref, g_ref, beta_ref, gn_ref, g512_ref, s0_ref, o_ref, sfin_ref,
                s_scr, o_scr, *, n_chunks):
    j = pl.program_id(0)
    n_pairs = GDN_HEADS // GDN_GROUP
    n_seq = q_ref.shape[0]

    @pl.when(j == 0)
    def _():
        for bi in range(n_seq):
            s_scr[bi] = s0_ref[...]

    row = lax.broadcasted_iota(jnp.int32, (CHUNK, GDN_GW), 0)
    lane = lax.broadcasted_iota(jnp.int32, (CHUNK, GDN_GW), 1)
    col = lane % CHUNK
    lane_head = lane // HEAD_DIM
    tri_incl = row >= col
    tri_strict = row > col
    eye = (row == col).astype(F32)

    def block_diag(x):
        xb = x.astype(BF16)
        zero = jnp.zeros_like(xb)
        return jnp.concatenate([jnp.where(lane_head == h, xb, zero) for h in range(GDN_GROUP)], axis=0)

    def pair_mm(x, y):
        return _dot(x.astype(BF16), block_diag(y))

    items = [(bi, c, p) for bi in range(n_seq) for c in range(n_chunks) for p in range(n_pairs)]
    rows_of = lambda c: slice(CHUNK * c, CHUNK * (c + 1))
    cols_of = lambda p: slice(GDN_GW * p, GDN_GW * (p + 1))
    load = lambda ref, it: ref[it[0], rows_of(it[1]), cols_of(it[2])]

    tbg = n_chunks * CHUNK
    ri = lax.broadcasted_iota(jnp.int32, (tbg, tbg), 0)
    ci = lax.broadcasted_iota(jnp.int32, (tbg, tbg), 1)
    chunk_tri = ((ri >= ci) & (ri // CHUNK == ci // CHUNK)).astype(BF16)
    g_cum = []
    for bi in range(n_seq):
        rest = g_ref[bi]
        terms = []
        for _ in range(3):
            term = rest.astype(BF16)
            terms.append(term)
            rest = rest - term.astype(F32)
        cum = _dot(chunk_tri, jnp.concatenate(terms, axis=1))
        g_cum.append(cum[:, :MIX_HALF] + cum[:, MIX_HALF:2 * MIX_HALF] + cum[:, 2 * MIX_HALF:])

    g_col, decay, k_f, k_beta, k_bd, q_f, vb = {}, {}, {}, {}, {}, {}, {}
    for it in items:
        g = load(g_ref, it)
        g_col[it] = g_cum[it[0]][rows_of(it[1]), cols_of(it[2])]
        g_row = jnp.sum(jnp.where(row <= col, g, 0.0), axis=0, keepdims=True)
        decay[it] = jnp.where(tri_incl, jnp.exp(jnp.where(tri_incl, g_col[it] - g_row, 0.0)), 0.0)
        beta = load(beta_ref, it).astype(F32)
        k_f[it] = load(k_ref, it).astype(F32)
        q_f[it] = load(q_ref, it).astype(F32)
        k_beta[it] = k_f[it] * beta
        vb[it] = load(v_ref, it).astype(F32) * beta
        k_bd[it] = block_diag(k_f[it])

    a, a_qk = {}, {}
    for it in items:
        lhs = jnp.concatenate([k_beta[it].astype(BF16), q_f[it].astype(BF16)], axis=0)
        kk = _dot_nt(lhs, k_bd[it])
        a[it] = jnp.where(tri_strict, kk[:CHUNK] * decay[it], 0.0)
        a_qk[it] = jnp.where(tri_incl, kk[CHUNK:] * decay[it], 0.0)

    t = {it: eye - a[it] for it in items}
    pw = {it: pair_mm(a[it], a[it]) for it in items}
    for _ in range(4):
        for it in items:
            both = pair_mm(jnp.concatenate([t[it], pw[it]], axis=0), pw[it])
            t[it] = t[it] + both[:CHUNK]
            pw[it] = both[CHUNK:]
    for it in items:
        t[it] = t[it] + pair_mm(t[it], pw[it])

    u, w, k_g, q_g, e_last = {}, {}, {}, {}, {}
    for it in items:
        e_col = jnp.exp(g_col[it])
        rhs = jnp.concatenate([block_diag(vb[it]), block_diag(k_beta[it] * e_col)], axis=1)
        uw = _dot(t[it].astype(BF16), rhs)
        u[it], w[it] = uw[:, :GDN_GW], uw[:, GDN_GW:]
        g_last = g_col[it][CHUNK - 1:CHUNK, :]
        k_end = k_f[it] * jnp.exp(g_last - g_col[it])
        k_g[it] = jnp.concatenate([jnp.where(lane_head == h, k_end, 0.0) for h in range(GDN_GROUP)], axis=0).T
        q_g[it] = q_f[it] * e_col
        e_last[it] = jnp.exp(g_last)

    state = {(bi, p): s_scr[bi, p] for bi in range(n_seq) for p in range(n_pairs)}
    sp = lambda it: (it[0], it[2])
    for c in range(n_chunks):
        its = [(bi, c, p) for bi in range(n_seq) for p in range(n_pairs)]
        ws_qs = [_dot(jnp.concatenate([w[it], q_g[it]], axis=0).astype(BF16), state[sp(it)].astype(BF16))
                 for it in its]
        v_new = [u[it] - sq[:CHUNK] for it, sq in zip(its, ws_qs)]
        for it, sq, vn in zip(its, ws_qs, v_new):
            both = pair_mm(jnp.concatenate([a_qk[it], k_g[it]], axis=0), vn)
            o_scr[it[0], rows_of(c), cols_of(it[2])] = sq[CHUNK:] + both[:CHUNK]
            state[sp(it)] = state[sp(it)] * e_last[it] + both[CHUNK:]
    for (bi, p), val in state.items():
        s_scr[bi, p] = val

    for bi in range(n_seq):
        o = o_scr[bi]
        ss = _group_sumsq(o, g512_ref)
        y = o * lax.rsqrt(ss * (1.0 / HEAD_DIM) + EPS) * gn_ref[...]
        o_ref[bi] = (y * z_ref[bi].astype(F32)).astype(BF16)

    @pl.when(j == pl.num_programs(0) - 1)
    def _():
        sfin_ref[...] = s_scr[...]


def _gated_delta(q, k, v, z, gb, beta, gn_rep, g512, s0, tbg):
    b, s, _ = q.shape
    nb = s // tbg
    state_shape = (b,) + s0.shape
    tok = pl.BlockSpec((b, tbg, MIX_HALF), lambda j: (0, j, 0))
    full = lambda shape: pl.BlockSpec(shape, lambda j: (0,) * len(shape))
    return pl.pallas_call(
        functools.partial(_gdn_kernel, n_chunks=tbg // CHUNK),
        grid=(nb,),
        in_specs=[tok] * 6 + [full((1, MIX_HALF)), full(g512.shape), full(s0.shape)],
        out_specs=[tok, pl.BlockSpec(state_shape, lambda j: (0, 0, 0, 0))],
        out_shape=[jax.ShapeDtypeStruct((b, s, MIX_HALF), BF16), jax.ShapeDtypeStruct(state_shape, F32)],
        scratch_shapes=[pltpu.VMEM(state_shape, F32), pltpu.VMEM((b, tbg, MIX_HALF), F32)],
        compiler_params=pltpu.CompilerParams(dimension_semantics=("arbitrary",),
                                             vmem_limit_bytes=VMEM_LIMIT),
        name="gated_delta",
    )(q, k, v, z, gb, beta, gn_rep, g512, s0)


def _out_router_kernel(x_ref, oa_ref, ob_ref, wo_ref, gain_ref, wr_ref, br_ref, before_ref,
                       h_ref, t_ref, rows_ref, cols_ref, cnt_ref):
    h = x_ref[0] + _dot(jnp.concatenate([oa_ref[0], ob_ref[0]], axis=1), wo_ref[...])
    h_ref[0] = h
    t = _rms_scale(h) * gain_ref[...]
    t_hi = t.astype(BF16)
    t_ref[0] = t_hi
    t_lo = (t - t_hi.astype(F32)).astype(BF16)
    wr = wr_ref[...]
    hi_part = _dot(t_hi, wr)
    logits = hi_part[:, :ROUTE_LANES] + hi_part[:, ROUTE_LANES:] + _dot(t_lo, wr)[:, :ROUTE_LANES] + br_ref[...]

    lt = logits.T
    tb = lt.shape[1]
    row8 = lax.broadcasted_iota(jnp.int32, (EXPERTS_PER_GROUP, tb), 0)
    col_max = lambda v: jnp.max(v, axis=0, keepdims=True)
    col_sum = lambda v: jnp.sum(v, axis=0, keepdims=True)
    first_argmax = lambda v, vmax: jnp.min(jnp.where(v == vmax, row8, EXPERTS_PER_GROUP), axis=0, keepdims=True)

    gl = jnp.where(row8 < N_GROUPS, lt[:EXPERTS_PER_GROUP], NEG_BIG)
    gmax = col_max(gl)
    gsel = first_argmax(gl, gmax)
    psel = 1.0 / col_sum(jnp.exp(gl - gmax))
    el = jnp.zeros((EXPERTS_PER_GROUP, tb), F32)
    for g in range(N_GROUPS):
        lo = EXPERT_ROW0 + EXPERTS_PER_GROUP * g
        el = jnp.where(gsel == g, lt[lo:lo + EXPERTS_PER_GROUP], el)
    m1 = col_max(el)
    i1 = first_argmax(el, m1)
    el2 = jnp.where(row8 == i1, NEG_BIG, el)
    m2 = col_max(el2)
    i2 = first_argmax(el2, m2)
    denom = col_sum(jnp.exp(el - m1))
    p1 = 1.0 / denom
    p2 = jnp.exp(m2 - m1) / denom
    w1 = p1 / (p1 + p2) * psel
    w2 = p2 / (p1 + p2) * psel
    e1 = gsel * EXPERTS_PER_GROUP + i1
    e2 = gsel * EXPERTS_PER_GROUP + i2

    row_e = lax.broadcasted_iota(jnp.int32, (N_EXPERTS, tb), 0)
    hot1 = (row_e == e1).astype(F32)
    hot2 = (row_e == e2).astype(F32)
    both = hot1 + hot2
    n_gran = jnp.ceil(jnp.sum(both, axis=1, keepdims=True) * (1.0 / SLOT_GRAN)) * SLOT_GRAN
    ei = lax.broadcasted_iota(jnp.int32, (N_EXPERTS, N_EXPERTS), 0)
    ej = lax.broadcasted_iota(jnp.int32, (N_EXPERTS, N_EXPERTS), 1)
    seg_start = _dot((ei > ej).astype(F32), jnp.broadcast_to(n_gran, (N_EXPERTS, SLAB)), precision=HIGHEST)[:, :1]
    earlier = _dot(both.astype(BF16), before_ref[...])
    where_to = earlier + seg_start
    pos1 = col_sum(hot1 * where_to)
    pos2 = col_sum(hot2 * where_to)

    info = jnp.concatenate([pos1, pos2, w1, w2, e1.astype(F32), e2.astype(F32),
                            jnp.zeros((SLAB - 6, tb), F32)], axis=0)
    rows_ref[0] = info[:8]
    cols_ref[0] = info.T
    cnt_ref[0] = jnp.broadcast_to(n_gran, (N_EXPERTS, SLAB))


def _out_router(x, oa, ob, w_o, gain, w_route, b_route, before, tb):
    b, s, d = x.shape
    nb = s // tb
    full = lambda shape: pl.BlockSpec(shape, lambda i, j: (0,) * len(shape))
    tokd = pl.BlockSpec((1, tb, d), lambda i, j: (i, j, 0))
    tokh = pl.BlockSpec((1, tb, MIX_HALF), lambda i, j: (i, j, 0))
    return pl.pallas_call(
        _out_router_kernel,
        grid=(b, nb),
        in_specs=[tokd, tokh, tokh, full(w_o.shape), full((1, d)), full(w_route.shape), full((1, ROUTE_LANES)),
                  full(before.shape)],
        out_specs=[tokd, tokd, pl.BlockSpec((1, 8, tb), lambda i, j: (i, 0, j)),
                   pl.BlockSpec((1, tb, SLAB), lambda i, j: (i, j, 0)),
                   pl.BlockSpec((1, N_EXPERTS, SLAB), lambda i, j: (i * nb + j, 0, 0))],
        out_shape=[jax.ShapeDtypeStruct((b, s, d), F32), jax.ShapeDtypeStruct((b, s, d), BF16),
                   jax.ShapeDtypeStruct((b, 8, s), F32), jax.ShapeDtypeStruct((b, s, SLAB), F32),
                   jax.ShapeDtypeStruct((b * nb, N_EXPERTS, SLAB), F32)],
        compiler_params=pltpu.CompilerParams(dimension_semantics=("parallel", "parallel"),
                                             vmem_limit_bytes=VMEM_LIMIT),
        name="out_router",
    )(x, oa, ob, w_o, gain, w_route, b_route, before)


def _granule_copies(table_ref, block, buf, local_ref, global_ref, sem, to_global, n_gran):
    copies = []
    for g in range(n_gran):
        loc = local_ref.at[buf, pl.ds(g * SLOT_GRAN, SLOT_GRAN)]
        glob = global_ref.at[pl.ds(pl.multiple_of(table_ref[block, g], SLOT_GRAN), SLOT_GRAN)]
        copies.append(pltpu.make_async_copy(loc, glob, sem.at[buf]) if to_global
                      else pltpu.make_async_copy(glob, loc, sem.at[buf]))
    return copies


def _dispatch_kernel(gdst_ref, t_ref, rows_ref, xg_init_ref, xg_ref, xs_scr, sem, *, n_slots):
    del xg_init_ref
    blk = pl.program_id(0)
    last = pl.num_programs(0) - 1
    buf = blk % 2
    tb = t_ref.shape[0]
    n_gran = n_slots // SLOT_GRAN
    info = rows_ref[0]
    slot = lax.broadcasted_iota(jnp.int32, (n_slots, tb), 0).astype(F32)
    onehot = jnp.where((slot == info[0:1]) | (slot == info[1:2]), 1.0, 0.0).astype(BF16)
    xs_scr[buf] = _dot(onehot, t_ref[...]).astype(BF16)
    for copy in _granule_copies(gdst_ref, blk, buf, xs_scr, xg_ref, sem, True, n_gran):
        copy.start()

    @pl.when(blk > 0)
    def _():
        for copy in _granule_copies(gdst_ref, blk - 1, 1 - buf, xs_scr, xg_ref, sem, True, n_gran):
            copy.wait()

    @pl.when(blk == last)
    def _():
        for copy in _granule_copies(gdst_ref, blk, buf, xs_scr, xg_ref, sem, True, n_gran):
            copy.wait()


def _dispatch(granule_dst, t, rows, xg_init, tb, n_slots):
    n, d = t.shape
    nblk = n // tb
    nb = rows.shape[2] // tb
    grid_spec = pltpu.PrefetchScalarGridSpec(
        num_scalar_prefetch=1,
        grid=(nblk,),
        in_specs=[pl.BlockSpec((tb, d), lambda i, gd: (i, 0)),
                  pl.BlockSpec((1, 8, tb), lambda i, gd: (i // nb, 0, i % nb)),
                  pl.BlockSpec(memory_space=pl.ANY)],
        out_specs=pl.BlockSpec(memory_space=pl.ANY),
        scratch_shapes=[pltpu.VMEM((2, n_slots, d), BF16), pltpu.SemaphoreType.DMA((2,))],
    )
    return pl.pallas_call(
        functools.partial(_dispatch_kernel, n_slots=n_slots),
        grid_spec=grid_spec,
        out_shape=jax.ShapeDtypeStruct(xg_init.shape, BF16),
        input_output_aliases={3: 0},
        compiler_params=pltpu.CompilerParams(dimension_semantics=("arbitrary",), vmem_limit_bytes=VMEM_LIMIT),
        name="moe_dispatch",
    )(granule_dst, t, rows, xg_init)


def _experts_kernel(tile_expert_ref, n_tiles_ref, x_ref, wg_ref, wu_ref, wd_ref, y_ref):
    del tile_expert_ref

    @pl.when(pl.program_id(0) < n_tiles_ref[0])
    def _():
        x = x_ref[...]
        act = _silu(_dot(x, wg_ref[0].astype(BF16))) * _dot(x, wu_ref[0].astype(BF16))
        y_ref[...] = _dot(act.astype(BF16), wd_ref[0].astype(BF16)).astype(BF16)


def _experts(tile_expert, n_tiles, xg, w_gate, w_up, w_down, tile):
    rows, d = xg.shape
    used = lambda i, te, nt: jnp.minimum(i, nt[0] - 1)
    wspec = lambda shape: pl.BlockSpec((1,) + shape, lambda i, te, nt: (te[used(i, te, nt)], 0, 0))
    grid_spec = pltpu.PrefetchScalarGridSpec(
        num_scalar_prefetch=2,
        grid=(tile_expert.shape[0],),
        in_specs=[pl.BlockSpec((tile, d), lambda i, te, nt: (used(i, te, nt), 0)),
                  wspec((d, D_EXPERT)), wspec((d, D_EXPERT)), wspec((D_EXPERT, d))],
        out_specs=pl.BlockSpec((tile, d), lambda i, te, nt: (used(i, te, nt), 0)),
    )
    return pl.pallas_call(
        _experts_kernel,
        grid_spec=grid_spec,
        out_shape=jax.ShapeDtypeStruct((rows, d), BF16),
        input_output_aliases={2: 0},
        compiler_params=pltpu.CompilerParams(dimension_semantics=("arbitrary",), vmem_limit_bytes=VMEM_LIMIT),
        name="moe_experts",
    )(tile_expert, n_tiles, xg, w_gate, w_up, w_down)


def _combine_kernel(gsrc_ref, yg_ref, cols_ref, h_ref, gain_ref, o_ref, ys_scr, sem, *, n_slots):
    blk = pl.program_id(0)
    last = pl.num_programs(0) - 1
    buf = blk % 2
    tb = h_ref.shape[0]
    n_gran = n_slots // SLOT_GRAN

    @pl.when(blk == 0)
    def _():
        for copy in _granule_copies(gsrc_ref, 0, 0, ys_scr, yg_ref, sem, False, n_gran):
            copy.start()

    @pl.when(blk < last)
    def _():
        for copy in _granule_copies(gsrc_ref, blk + 1, 1 - buf, ys_scr, yg_ref, sem, False, n_gran):
            copy.start()

    info = cols_ref[...]
    slot = lax.broadcasted_iota(jnp.int32, (tb, n_slots), 1).astype(F32)
    weights = (jnp.where(slot == info[:, 0:1], info[:, 2:3], 0.0)
               + jnp.where(slot == info[:, 1:2], info[:, 3:4], 0.0)).astype(BF16)
    for copy in _granule_copies(gsrc_ref, blk, buf, ys_scr, yg_ref, sem, False, n_gran):
        copy.wait()
    y = _dot(weights, ys_scr[buf])
    o_ref[...] = _rms_scale(h_ref[...] + y) * gain_ref[...]


def _combine(granule_dst, yg, cols, h, gain, tb, n_slots):
    n, d = h.shape
    grid_spec = pltpu.PrefetchScalarGridSpec(
        num_scalar_prefetch=1,
        grid=(n // tb,),
        in_specs=[pl.BlockSpec(memory_space=pl.ANY),
                  pl.BlockSpec((tb, SLAB), lambda i, gd: (i, 0)),
                  pl.BlockSpec((tb, d), lambda i, gd: (i, 0)),
                  pl.BlockSpec((1, d), lambda i, gd: (0, 0))],
        out_specs=pl.BlockSpec((tb, d), lambda i, gd: (i, 0)),
        scratch_shapes=[pltpu.VMEM((2, n_slots, d), BF16), pltpu.SemaphoreType.DMA((2,))],
    )
    return pl.pallas_call(
        functools.partial(_combine_kernel, n_slots=n_slots),
        grid_spec=grid_spec,
        out_shape=jax.ShapeDtypeStruct((n, d), F32),
        compiler_params=pltpu.CompilerParams(dimension_semantics=("arbitrary",), vmem_limit_bytes=VMEM_LIMIT),
        name="moe_combine",
    )(granule_dst, yg, cols, h, gain)


def _regroup_plan(n_gran_be, n_slots, tile, n_tiles_max):
    cnt = n_gran_be.astype(jnp.int32)
    region = (jnp.sum(cnt, axis=0) + tile - 1) // tile * tile
    region_end = jnp.cumsum(region)
    first_row = (region_end - region)[None, :] + jnp.cumsum(cnt, axis=0) - cnt
    seg_end = jnp.cumsum(cnt, axis=1)
    seg_start = seg_end - cnt
    g_row = jnp.arange(n_slots // SLOT_GRAN, dtype=jnp.int32) * SLOT_GRAN
    owned = ((g_row[None, :, None] >= seg_start[:, None, :]) & (g_row[None, :, None] < seg_end[:, None, :]))
    in_use = g_row[None, :] < seg_end[:, -1:]
    row = g_row[None, :] + jnp.sum(jnp.where(owned, (first_row - seg_start)[:, None, :], 0), axis=-1)
    zero_row = n_tiles_max * tile
    spill_row = zero_row + SLOT_GRAN + (jnp.arange(cnt.shape[0], dtype=jnp.int32) % 2)[:, None] * n_slots + g_row
    dispatch_dst = jnp.where(in_use, row, spill_row).astype(jnp.int32)
    combine_src = jnp.where(in_use, row, zero_row).astype(jnp.int32)
    tile_row = jnp.arange(n_tiles_max, dtype=jnp.int32) * tile
    tile_expert = jnp.minimum(jnp.sum(tile_row[:, None] >= region_end[None, :], axis=-1), N_EXPERTS - 1)
    n_tiles = (region_end[-1] // tile).astype(jnp.int32)[None]
    return dispatch_dst, combine_src, tile_expert.astype(jnp.int32), n_tiles


def _rope_tables(first, length):
    pos = jnp.arange(first, first + length, dtype=F32)
    inv_freq = ROPE_THETA ** (-jnp.arange(0, HEAD_DIM, 2, dtype=F32) / HEAD_DIM)
    ang = pos[:, None] * inv_freq[None, :]
    cos, sin = jnp.cos(ang), jnp.sin(ang)
    zero = jnp.zeros_like(sin)
    cos_t = jnp.tile(cos, (1, 4))
    sina_t = jnp.tile(jnp.concatenate([-sin, zero], axis=1), (1, 2))
    sinb_t = jnp.tile(jnp.concatenate([zero, sin], axis=1), (1, 2))
    return cos_t, sina_t, sinb_t


def _block(total, want):
    blk = min(total, want)
    assert total % blk == 0, (total, blk)
    return blk


def kernel(x, meta, norm_mix, w_in, lambda_q1, lambda_k1, lambda_q2, lambda_k2, diff_norm, conv_w, a_log, dt_bias,
           gdn_norm, w_out, norm_ffn, w_group, b_group, w_router, b_router, w_gate, w_up, w_down, norm_final):
    b, s, d = x.shape
    assert d == D_MODEL and meta.shape == (N_META, D_MODEL) and s % CHUNK == 0
    assert norm_mix.shape[0] == 1, "single-layer block"
    l = 0

    w = w_in[l]
    w_a = w[:, :3 * MIX_HALF].astype(BF16)
    w_b = w[:, 3 * MIX_HALF:6 * MIX_HALF].astype(BF16)
    w_z = w[:, 6 * MIX_HALF:7 * MIX_HALF].astype(BF16)
    c_ab = 7 * MIX_HALF
    lane_pad = lambda a: jnp.pad(a, [(0, 0)] * (a.ndim - 1) + [(0, SLAB - a.shape[-1])])
    w_g = lane_pad(w[:, c_ab:c_ab + 2 * GDN_HEADS]).astype(BF16)
    alog_rep = lane_pad(a_log[l])[None]
    dtb_rep = lane_pad(dt_bias[l])[None]
    gn_rep = jnp.tile(gdn_norm[l], GDN_HEADS)[None]
    g512 = (jnp.arange(MIX_HALF)[:, None] // HEAD_DIM == jnp.arange(MIX_HALF)[None, :] // HEAD_DIM).astype(BF16)
    gain_mix = norm_mix[l][None]
    w_o = w_out[l].astype(BF16)
    route_pad = lambda g, e: jnp.concatenate(
        [g, jnp.zeros(g.shape[:-1] + (EXPERT_ROW0 - N_GROUPS,), F32), e,
         jnp.zeros(g.shape[:-1] + (ROUTE_LANES - EXPERT_ROW0 - N_EXPERTS,), F32)], axis=-1)
    w_route_f = route_pad(w_group[l], w_router[l])
    w_route_hi = w_route_f.astype(BF16)
    w_route = jnp.concatenate([w_route_hi, (w_route_f - w_route_hi.astype(F32)).astype(BF16)], axis=1)
    b_route = route_pad(b_group[l], b_router[l])[None]
    lam_vecs = [v[l][None] for v in (lambda_q1, lambda_k1, lambda_q2, lambda_k2)]
    rope_meta = _rope_tables(0, N_META)
    rope_frames = _rope_tables(N_META, s)

    meta3 = meta[None]
    _, ka_m, va_m = _proj_attn(meta3, gain_mix, w_a, *rope_meta, N_META)
    zero_halo = jnp.zeros((1, 1, N_META, d), F32)
    gdn_m = _proj_gdn(meta3, zero_halo, gain_mix, w_b, w_z, w_g, conv_w[l], alog_rep, dtb_rep, g512, N_META)
    pad_m = lambda a: jnp.pad(a, ((0, 0), (0, CHUNK - N_META), (0, 0)))
    s_zero = jnp.zeros((GDN_HEADS // GDN_GROUP, GDN_GW, GDN_GW), F32)
    _, s_meta = _gated_delta(*[pad_m(a) for a in gdn_m], gn_rep, g512, s_zero, CHUNK)

    tb_a = _block(s, PROJ_ATTN_ROWS)
    qa, ka, va = _proj_attn(x, gain_mix, w_a, *rope_frames, tb_a)
    tb_g = _block(s, PROJ_GDN_ROWS)
    nb_g = s // tb_g
    tails = x.reshape(b, nb_g, tb_g, d)[:, :-1, tb_g - N_META:, :]
    halo = jnp.concatenate([jnp.broadcast_to(meta[None, None], (b, 1, N_META, d)), tails], axis=1)
    gdn_f = _proj_gdn(x, halo, gain_mix, w_b, w_z, w_g, conv_w[l], alog_rep, dtb_rep, g512, tb_g)

    oa = _diff_attention(qa, ka, va, ka_m, va_m, *lam_vecs, diff_norm[l][None],
                         _block(s, ATTN_QUERY_ROWS), ATTN_KEY_BLOCK)
    ob, _ = _gated_delta(*gdn_f, gn_rep, g512, s_meta[0], _block(s, GDN_ROWS))

    tb_r = _block(s, MOE_BLOCK)
    tok = jnp.arange(tb_r)
    before = (tok[:, None] < tok[None, :]).astype(BF16)
    h1, t, rows, cols, cnt = _out_router(x, oa, ob, w_o, norm_ffn[l][None], w_route, b_route, before, tb_r)

    n = b * s
    nblk = n // tb_r
    n_slots = 2 * tb_r + N_EXPERTS * SLOT_GRAN
    rows_max = 2 * n + (SLOT_GRAN - 1) * N_EXPERTS * nblk + (MOE_TILE - 1) * N_EXPERTS
    n_tiles_max = -(-rows_max // MOE_TILE)
    dispatch_dst, combine_src, tile_expert, n_tiles = _regroup_plan(cnt[:, :, 0], n_slots, MOE_TILE, n_tiles_max)
    buf_rows = n_tiles_max * MOE_TILE + SLOT_GRAN + 2 * n_slots
    xg = _dispatch(dispatch_dst, t.reshape(n, d), rows, jnp.zeros((buf_rows, d), BF16), tb_r, n_slots)
    yg = _experts(tile_expert, n_tiles, xg, w_gate[l], w_up[l], w_down[l], MOE_TILE)
    out = _combine(combine_src, yg, cols.reshape(n, SLAB), h1.reshape(n, d), norm_final[None], tb_r, n_slots)
    return out.reshape(b, s, d)
```

```python
import functools
import math

import jax
import jax.numpy as jnp
from jax import lax
from jax.experimental import pallas as pl
from jax.experimental.pallas import tpu as pltpu

F32 = jnp.float32
BF16 = jnp.bfloat16
HIGHEST = lax.Precision.HIGHEST

D_MODEL = 1024
N_META = 16
CHUNK = 64
EPS = 1e-6
ROPE_THETA = 10000.0
HEAD_DIM = 64
SLAB = 128
DA_HEADS = 4
GDN_HEADS = 8
MIX_HALF = 512
GDN_GROUP = 2
GDN_GW = GDN_GROUP * HEAD_DIM
N_GROUPS = 4
EXPERTS_PER_GROUP = 8
N_EXPERTS = 32
D_EXPERT = 256
CONV_K = 4
LAM_INIT = 0.8 - 0.6 * math.exp(-0.3 * 0)
ROUTE_LANES = 128
EXPERT_ROW0 = 8
SLOT_GRAN = 16
MOE_BLOCK = 512
MOE_TILE = 512
ROW_STRIP = 64
PROJ_ATTN_ROWS = 1024
PROJ_GDN_ROWS = 1024
ATTN_QUERY_ROWS = 2048
ATTN_KEY_BLOCK = 512
GDN_ROWS = 256
NEG_BIG = -1e30
LOG2E = math.log2(math.e)
VMEM_LIMIT = 56 * 1024 * 1024


def _dot(a, b, precision=None):
    return jnp.dot(a, b, preferred_element_type=F32, precision=precision)


def _dot_nt(a, b):
    return lax.dot_general(a, b, (((1,), (1,)), ((), ())), preferred_element_type=F32)


def _rms_scale(x):
    return x * lax.rsqrt(jnp.mean(x * x, axis=-1, keepdims=True) + EPS)


def _silu(x):
    return x * jax.nn.sigmoid(x)


def _group_sumsq(x, g_ref):
    return _dot((x * x).astype(BF16), g_ref[...])


def _proj_attn_kernel(x_ref, gain_ref, w_ref, cos_ref, sina_ref, sinb_ref, qa_ref, ka_ref, va_ref):
    u = (_rms_scale(x_ref[0]) * gain_ref[...]).astype(BF16)
    proj = _dot(u, w_ref[...])
    va_ref[0] = proj[:, 2 * MIX_HALF:].astype(BF16)
    cos, sina, sinb = cos_ref[...], sina_ref[...], sinb_ref[...]
    for s in range(2 * DA_HEADS):
        xs = proj[:, SLAB * s:SLAB * (s + 1)]
        r = xs * cos + pltpu.roll(xs, SLAB - 32, 1) * sina + pltpu.roll(xs, 32, 1) * sinb
        if s < DA_HEADS:
            qa_ref[0, :, SLAB * s:SLAB * (s + 1)] = (r * (HEAD_DIM ** -0.5 * LOG2E)).astype(BF16)
        else:
            t = s - DA_HEADS
            ka_ref[0, :, SLAB * t:SLAB * (t + 1)] = r.astype(BF16)


def _proj_attn(x, gain, w_a, cos, sina, sinb, tb):
    b, s, d = x.shape
    nb = s // tb
    slab_out = jax.ShapeDtypeStruct((b, s, MIX_HALF), BF16)
    full = lambda shape: pl.BlockSpec(shape, lambda i, j: (0,) * len(shape))
    tok = pl.BlockSpec((1, tb, MIX_HALF), lambda i, j: (i, j, 0))
    tab = pl.BlockSpec((tb, SLAB), lambda i, j: (j, 0))
    return pl.pallas_call(
        _proj_attn_kernel,
        grid=(b, nb),
        in_specs=[pl.BlockSpec((1, tb, d), lambda i, j: (i, j, 0)), full((1, d)), full(w_a.shape), tab, tab, tab],
        out_specs=[tok, tok, tok],
        out_shape=[slab_out, slab_out, slab_out],
        compiler_params=pltpu.CompilerParams(dimension_semantics=("parallel", "parallel"),
                                             vmem_limit_bytes=VMEM_LIMIT),
        name="proj_attn",
    )(x, gain, w_a, cos, sina, sinb)


def _proj_gdn_kernel(x_ref, halo_ref, gain_ref, wb_ref, wz_ref, wg_ref, convw_ref, alog_ref, dtb_ref, g_ref,
                     q_ref, k_ref, v_ref, z_ref, gb_ref, beta_ref, pb_scr):
    halo_rows = halo_ref.shape[2]
    xe = jnp.concatenate([halo_ref[0, 0], x_ref[0]], axis=0)
    ue = (_rms_scale(xe) * gain_ref[...]).astype(BF16)
    u = ue[halo_rows:]
    cw = convw_ref[...]
    outs = (q_ref, k_ref, v_ref)
    for part in range(3):
        cols = slice(MIX_HALF * part, MIX_HALF * (part + 1))
        pb_scr[...] = _dot(ue, wb_ref[:, cols])
        c = cw[:, cols]
        tb = pb_scr.shape[0] - halo_rows
        y = pb_scr[halo_rows:, :] * c[CONV_K - 1:CONV_K]
        for back in range(1, CONV_K):
            tap = CONV_K - 1 - back
            y = y + pb_scr[pl.ds(halo_rows - back, tb), :] * c[tap:tap + 1]
        y = _silu(y)
        if part < 2:
            y = y * lax.rsqrt(_group_sumsq(y, g_ref) + EPS)
            if part == 0:
                y = y * (HEAD_DIM ** -0.5)
        outs[part][0] = y.astype(BF16)
    z_ref[0] = _silu(_dot(u, wz_ref[...])).astype(BF16)
    gates = _dot(u, wg_ref[...])
    ab = gates + dtb_ref[...]
    softplus = jnp.maximum(ab, 0.0) + jnp.log1p(jnp.exp(-jnp.abs(ab)))
    g = -jnp.exp(alog_ref[...]) * softplus
    beta = jax.nn.sigmoid(gates)
    tb = g.shape[0]
    first_head = lax.broadcasted_iota(jnp.int32, (tb, SLAB), 1) < HEAD_DIM
    spread = lambda v, h: jnp.broadcast_to(v[:, h:h + 1], (tb, SLAB))
    for p in range(GDN_HEADS // 2):
        cols = slice(SLAB * p, SLAB * (p + 1))
        gb_ref[0, :, cols] = jnp.where(first_head, spread(g, 2 * p), spread(g, 2 * p + 1))
        beta_ref[0, :, cols] = jnp.where(first_head, spread(beta, GDN_HEADS + 2 * p),
                                         spread(beta, GDN_HEADS + 2 * p + 1)).astype(BF16)


def _proj_gdn(x, halo, gain, w_b, w_z, w_g, conv_w, alog_rep, dtb_rep, g512, tb):
    b, s, d = x.shape
    nb = s // tb
    full = lambda shape: pl.BlockSpec(shape, lambda i, j: (0,) * len(shape))
    tok = pl.BlockSpec((1, tb, MIX_HALF), lambda i, j: (i, j, 0))
    bf = jax.ShapeDtypeStruct((b, s, MIX_HALF), BF16)
    return pl.pallas_call(
        _proj_gdn_kernel,
        grid=(b, nb),
        in_specs=[pl.BlockSpec((1, tb, d), lambda i, j: (i, j, 0)),
                  pl.BlockSpec((1, 1) + halo.shape[2:], lambda i, j: (i, j, 0, 0)),
                  full((1, d)), full(w_b.shape), full(w_z.shape), full(w_g.shape), full(conv_w.shape),
                  full((1, SLAB)), full((1, SLAB)), full(g512.shape)],
        out_specs=[tok] * 6,
        out_shape=[bf, bf, bf, bf, jax.ShapeDtypeStruct((b, s, MIX_HALF), F32), bf],
        scratch_shapes=[pltpu.VMEM((halo.shape[2] + tb, MIX_HALF), F32)],
        compiler_params=pltpu.CompilerParams(dimension_semantics=("parallel", "parallel"),
                                             vmem_limit_bytes=VMEM_LIMIT),
        name="proj_gdn",
    )(x, halo, gain, w_b, w_z, w_g, conv_w, alog_rep, dtb_rep, g512)


def _attn_kernel(q_ref, k_ref, v_ref, km_ref, vm_ref, lq1_ref, lk1_ref, lq2_ref, lk2_ref, dn_ref, o_ref,
                 qq_scr, m_scr, acc_scr, st_scr, *, qb, cw):
    i = pl.program_id(2)
    n_col = 2 * qb // cw
    per_map = qb // cw
    q = q_ref[0]
    lane = lax.broadcasted_iota(jnp.int32, (qb, SLAB), 1)
    zero = jnp.zeros_like(q)
    qq_scr[...] = jnp.concatenate([jnp.where(lane < HEAD_DIM, q, zero), jnp.where(lane < HEAD_DIM, zero, q)], axis=0)

    def with_ones(v_blk):
        return jnp.concatenate([v_blk, jnp.ones_like(v_blk)], axis=1)

    v1_meta = with_ones(vm_ref[0])
    for c in range(n_col):
        rs = slice(cw * c, cw * (c + 1))
        s = _dot_nt(qq_scr[rs, :], km_ref[0])
        m_new = jnp.max(s, axis=1, keepdims=True)
        acc_scr[rs, :] = _dot(jnp.exp2((s - m_new).astype(BF16)), v1_meta)
        m_scr[rs, :] = jnp.broadcast_to(m_new, (cw, SLAB))

    def scores_into(slot, start, c):
        st_scr[slot] = _dot_nt(qq_scr[cw * c:cw * (c + 1), :], k_ref[0, pl.ds(start, cw), :])

    def softmax_pv(slot, v1, c, masked):
        alphas = []
        for t in range(cw // ROW_STRIP):
            ls = slice(ROW_STRIP * t, ROW_STRIP * (t + 1))
            gs = slice(cw * c + ROW_STRIP * t, cw * c + ROW_STRIP * (t + 1))
            s = st_scr[slot, ls, :]
            if masked:
                q_chunk = (lax.broadcasted_iota(jnp.int32, s.shape, 0) + ROW_STRIP * t) // CHUNK
                k_chunk = lax.broadcasted_iota(jnp.int32, s.shape, 1) // CHUNK
                s = jnp.where(k_chunk <= q_chunk, s, NEG_BIG)
            parts = [s[:, SLAB * k:SLAB * (k + 1)] for k in range(cw // SLAB)]
            lane_max = functools.reduce(jnp.maximum, parts)
            m_prev = m_scr[gs, :]
            m_new = jnp.maximum(m_prev, jnp.max(lane_max, axis=1, keepdims=True))
            for k, part in enumerate(parts):
                st_scr[slot, ls, SLAB * k:SLAB * (k + 1)] = jnp.exp2(part - m_new)
            alphas.append(jnp.exp2(m_prev - m_new))
            m_scr[gs, :] = m_new
        rs = slice(cw * c, cw * (c + 1))
        alpha = jnp.concatenate(alphas, axis=0)
        pv = _dot(st_scr[slot].astype(BF16), v1)
        acc_scr[rs, :] = acc_scr[rs, :] * jnp.concatenate([alpha, alpha], axis=1) + pv

    parity = [0]
    scores_into(0, 0, 0)

    def key_block(start, groups, masked_groups, following):
        v1 = with_ones(v_ref[0, pl.ds(start, cw), :])
        for idx, c in enumerate(groups):
            slot = parity[0]
            if idx + 1 < len(groups):
                scores_into(1 - slot, start, groups[idx + 1])
            elif following is not None:
                scores_into(1 - slot, *following)
            softmax_pv(slot, v1, c, c in masked_groups)
            parity[0] = 1 - slot

    all_groups = list(range(n_col))

    def full_blocks(j, carry):
        for d in range(per_map):
            start = pl.multiple_of((per_map * j + d) * cw, cw)
            key_block(start, all_groups, (), (start + cw, 0))
        return carry

    lax.fori_loop(0, i, full_blocks, 0)
    for d in range(per_map):
        start = pl.multiple_of((per_map * i + d) * cw, cw)
        groups = [c for c in all_groups if c % per_map >= d]
        following = (start + cw, d + 1) if d + 1 < per_map else None
        key_block(start, groups, [c for c in groups if c % per_map == d], following)

    acc = acc_scr[...]
    o1 = acc[:qb, :SLAB] / acc[:qb, SLAB:]
    o2 = acc[qb:, :SLAB] / acc[qb:, SLAB:]
    lam = (jnp.exp(jnp.sum(lq1_ref[...] * lk1_ref[...], axis=1, keepdims=True))
           - jnp.exp(jnp.sum(lq2_ref[...] * lk2_ref[...], axis=1, keepdims=True)) + LAM_INIT)
    o = o1 - lam * o2
    o_ref[0] = (_rms_scale(o) * dn_ref[...] * (1.0 - LAM_INIT)).astype(BF16)


def _diff_attention(qa, ka, va, ka_meta, va_meta, lq1, lk1, lq2, lk2, diff_norm, qb, cw):
    b, s, _ = qa.shape
    nq = s // qb
    vec = lambda n: pl.BlockSpec((1, n), lambda bi, h, i: (0, 0))
    seq = pl.BlockSpec((1, s, SLAB), lambda bi, h, i: (bi, 0, h))
    meta = pl.BlockSpec((1, N_META, SLAB), lambda bi, h, i: (0, 0, h))
    return pl.pallas_call(
        functools.partial(_attn_kernel, qb=qb, cw=cw),
        grid=(b, DA_HEADS, nq),
        in_specs=[pl.BlockSpec((1, qb, SLAB), lambda bi, h, i: (bi, i, h)), seq, seq, meta, meta,
                  vec(HEAD_DIM), vec(HEAD_DIM), vec(HEAD_DIM), vec(HEAD_DIM), vec(SLAB)],
        out_specs=pl.BlockSpec((1, qb, SLAB), lambda bi, h, i: (bi, i, h)),
        out_shape=jax.ShapeDtypeStruct((b, s, MIX_HALF), BF16),
        scratch_shapes=[pltpu.VMEM((2 * qb, SLAB), BF16), pltpu.VMEM((2 * qb, SLAB), F32),
                        pltpu.VMEM((2 * qb, 2 * SLAB), F32), pltpu.VMEM((2, cw, cw), F32)],
        compiler_params=pltpu.CompilerParams(dimension_semantics=("parallel", "parallel", "arbitrary"),
                                             vmem_limit_bytes=VMEM_LIMIT),
        name="diff_attention",
    )(qa, ka, va, ka_meta, va_meta, lq1, lk1, lq2, lk2, diff_norm)


def _gdn_kernel(q_ref, k_ref, v_ref, z_ref, g_ref, beta_ref, gn_ref, g512_ref, s0_ref, o_ref, sfin_ref,
                s_scr, o_scr, *, n_chunks):
    j = pl.program_id(0)
    n_pairs = GDN_HEADS // GDN_GROUP
    n_seq = q_ref.shape[0]

    @pl.when(j == 0)
    def _():
        for bi in range(n_seq):
            s_scr[bi] = s0_ref[...]

    row = lax.broadcasted_iota(jnp.int32, (CHUNK, GDN_GW), 0)
    lane = lax.broadcasted_iota(jnp.int32, (CHUNK, GDN_GW), 1)
    col = lane % CHUNK
    lane_head = lane // HEAD_DIM
    tri_incl = row >= col
    tri_strict = row > col
    eye = (row == col).astype(F32)

    def block_diag(x):
        xb = x.astype(BF16)
        zero = jnp.zeros_like(xb)
        return jnp.concatenate([jnp.where(lane_head == h, xb, zero) for h in range(GDN_GROUP)], axis=0)

    def pair_mm(x, y):
        return _dot(x.astype(BF16), block_diag(y))

    items = [(bi, c, p) for bi in range(n_seq) for c in range(n_chunks) for p in range(n_pairs)]
    rows_of = lambda c: slice(CHUNK * c, CHUNK * (c + 1))
    cols_of = lambda p: slice(GDN_GW * p, GDN_GW * (p + 1))
    load = lambda ref, it: ref[it[0], rows_of(it[1]), cols_of(it[2])]

    tbg = n_chunks * CHUNK
    ri = lax.broadcasted_iota(jnp.int32, (tbg, tbg), 0)
    ci = lax.broadcasted_iota(jnp.int32, (tbg, tbg), 1)
    chunk_tri = ((ri >= ci) & (ri // CHUNK == ci // CHUNK)).astype(BF16)
    g_cum = []
    for bi in range(n_seq):
        rest = g_ref[bi]
        terms = []
        for _ in range(3):
            term = rest.astype(BF16)
            terms.append(term)
            rest = rest - term.astype(F32)
        cum = _dot(chunk_tri, jnp.concatenate(terms, axis=1))
        g_cum.append(cum[:, :MIX_HALF] + cum[:, MIX_HALF:2 * MIX_HALF] + cum[:, 2 * MIX_HALF:])

    g_col, decay, k_f, k_beta, k_bd, q_f, vb = {}, {}, {}, {}, {}, {}, {}
    for it in items:
        g = load(g_ref, it)
        g_col[it] = g_cum[it[0]][rows_of(it[1]), cols_of(it[2])]
        g_row = jnp.sum(jnp.where(row <= col, g, 0.0), axis=0, keepdims=True)
        decay[it] = jnp.where(tri_incl, jnp.exp(jnp.where(tri_incl, g_col[it] - g_row, 0.0)), 0.0)
        beta = load(beta_ref, it).astype(F32)
        k_f[it] = load(k_ref, it).astype(F32)
        q_f[it] = load(q_ref, it).astype(F32)
        k_beta[it] = k_f[it] * beta
        vb[it] = load(v_ref, it).astype(F32) * beta
        k_bd[it] = block_diag(k_f[it])

    a, a_qk = {}, {}
    for it in items:
        lhs = jnp.concatenate([k_beta[it].astype(BF16), q_f[it].astype(BF16)], axis=0)
        kk = _dot_nt(lhs, k_bd[it])
        a[it] = jnp.where(tri_strict, kk[:CHUNK] * decay[it], 0.0)
        a_qk[it] = jnp.where(tri_incl, kk[CHUNK:] * decay[it], 0.0)

    t = {it: eye - a[it] for it in items}
    pw = {it: pair_mm(a[it], a[it]) for it in items}
    for _ in range(4):
        for it in items:
            both = pair_mm(jnp.concatenate([t[it], pw[it]], axis=0), pw[it])
            t[it] = t[it] + both[:CHUNK]
            pw[it] = both[CHUNK:]
    for it in items:
        t[it] = t[it] + pair_mm(t[it], pw[it])

    u, w, k_g, q_g, e_last = {}, {}, {}, {}, {}
    for it in items:
        e_col = jnp.exp(g_col[it])
        rhs = jnp.concatenate([block_diag(vb[it]), block_diag(k_beta[it] * e_col)], axis=1)
        uw = _dot(t[it].astype(BF16), rhs)
        u[it], w[it] = uw[:, :GDN_GW], uw[:, GDN_GW:]
        g_last = g_col[it][CHUNK - 1:CHUNK, :]
        k_end = k_f[it] * jnp.exp(g_last - g_col[it])
        k_g[it] = jnp.concatenate([jnp.where(lane_head == h, k_end, 0.0) for h in range(GDN_GROUP)], axis=0).T
        q_g[it] = q_f[it] * e_col
        e_last[it] = jnp.exp(g_last)

    state = {(bi, p): s_scr[bi, p] for bi in range(n_seq) for p in range(n_pairs)}
    sp = lambda it: (it[0], it[2])
    for c in range(n_chunks):
        its = [(bi, c, p) for bi in range(n_seq) for p in range(n_pairs)]
        ws_qs = [_dot(jnp.concatenate([w[it], q_g[it]], axis=0).astype(BF16), state[sp(it)].astype(BF16))
                 for it in its]
        v_new = [u[it] - sq[:CHUNK] for it, sq in zip(its, ws_qs)]
        for it, sq, vn in zip(its, ws_qs, v_new):
            both = pair_mm(jnp.concatenate([a_qk[it], k_g[it]], axis=0), vn)
            o_scr[it[0], rows_of(c), cols_of(it[2])] = sq[CHUNK:] + both[:CHUNK]
            state[sp(it)] = state[sp(it)] * e_last[it] + both[CHUNK:]
    for (bi, p), val in state.items():
        s_scr[bi, p] = val

    for bi in range(n_seq):
        o = o_scr[bi]
        ss = _group_sumsq(o, g512_ref)
        y = o * lax.rsqrt(ss * (1.0 / HEAD_DIM) + EPS) * gn_ref[...]
        o_ref[bi] = (y * z_ref[bi].astype(F32)).astype(BF16)

    @pl.when(j == pl.num_programs(0) - 1)
    def _():
        sfin_ref[...] = s_scr[...]


def _gated_delta(q, k, v, z, gb, beta, gn_rep, g512, s0, tbg):
    b, s, _ = q.shape
    nb = s // tbg
    state_shape = (b,) + s0.shape
    tok = pl.BlockSpec((b, tbg, MIX_HALF), lambda j: (0, j, 0))
    full = lambda shape: pl.BlockSpec(shape, lambda j: (0,) * len(shape))
    return pl.pallas_call(
        functools.partial(_gdn_kernel, n_chunks=tbg // CHUNK),
        grid=(nb,),
        in_specs=[tok] * 6 + [full((1, MIX_HALF)), full(g512.shape), full(s0.shape)],
        out_specs=[tok, pl.BlockSpec(state_shape, lambda j: (0, 0, 0, 0))],
        out_shape=[jax.ShapeDtypeStruct((b, s, MIX_HALF), BF16), jax.ShapeDtypeStruct(state_shape, F32)],
        scratch_shapes=[pltpu.VMEM(state_shape, F32), pltpu.VMEM((b, tbg, MIX_HALF), F32)],
        compiler_params=pltpu.CompilerParams(dimension_semantics=("arbitrary",),
                                             vmem_limit_bytes=VMEM_LIMIT),
        name="gated_delta",
    )(q, k, v, z, gb, beta, gn_rep, g512, s0)


def _out_router_kernel(x_ref, oa_ref, ob_ref, wo_ref, gain_ref, wr_ref, br_ref, before_ref,
                       h_ref, t_ref, rows_ref, cols_ref, cnt_ref):
    h = x_ref[0] + _dot(jnp.concatenate([oa_ref[0], ob_ref[0]], axis=1), wo_ref[...])
    h_ref[0] = h
    t = _rms_scale(h) * gain_ref[...]
    t_hi = t.astype(BF16)
    t_ref[0] = t_hi
    t_lo = (t - t_hi.astype(F32)).astype(BF16)
    wr = wr_ref[...]
    hi_part = _dot(t_hi, wr)
    logits = hi_part[:, :ROUTE_LANES] + hi_part[:, ROUTE_LANES:] + _dot(t_lo, wr)[:, :ROUTE_LANES] + br_ref[...]

    lt = logits.T
    tb = lt.shape[1]
    row8 = lax.broadcasted_iota(jnp.int32, (EXPERTS_PER_GROUP, tb), 0)
    col_max = lambda v: jnp.max(v, axis=0, keepdims=True)
    col_sum = lambda v: jnp.sum(v, axis=0, keepdims=True)
    first_argmax = lambda v, vmax: jnp.min(jnp.where(v == vmax, row8, EXPERTS_PER_GROUP), axis=0, keepdims=True)

    gl = jnp.where(row8 < N_GROUPS, lt[:EXPERTS_PER_GROUP], NEG_BIG)
    gmax = col_max(gl)
    gsel = first_argmax(gl, gmax)
    psel = 1.0 / col_sum(jnp.exp(gl - gmax))
    el = jnp.zeros((EXPERTS_PER_GROUP, tb), F32)
    for g in range(N_GROUPS):
        lo = EXPERT_ROW0 + EXPERTS_PER_GROUP * g
        el = jnp.where(gsel == g, lt[lo:lo + EXPERTS_PER_GROUP], el)
    m1 = col_max(el)
    i1 = first_argmax(el, m1)
    el2 = jnp.where(row8 == i1, NEG_BIG, el)
    m2 = col_max(el2)
    i2 = first_argmax(el2, m2)
    denom = col_sum(jnp.exp(el - m1))
    p1 = 1.0 / denom
    p2 = jnp.exp(m2 - m1) / denom
    w1 = p1 / (p1 + p2) * psel
    w2 = p2 / (p1 + p2) * psel
    e1 = gsel * EXPERTS_PER_GROUP + i1
    e2 = gsel * EXPERTS_PER_GROUP + i2

    row_e = lax.broadcasted_iota(jnp.int32, (N_EXPERTS, tb), 0)
    hot1 = (row_e == e1).astype(F32)
    hot2 = (row_e == e2).astype(F32)
    both = hot1 + hot2
    n_gran = jnp.ceil(jnp.sum(both, axis=1, keepdims=True) * (1.0 / SLOT_GRAN)) * SLOT_GRAN
    ei = lax.broadcasted_iota(jnp.int32, (N_EXPERTS, N_EXPERTS), 0)
    ej = lax.broadcasted_iota(jnp.int32, (N_EXPERTS, N_EXPERTS), 1)
    seg_start = _dot((ei > ej).astype(F32), jnp.broadcast_to(n_gran, (N_EXPERTS, SLAB)), precision=HIGHEST)[:, :1]
    earlier = _dot(both.astype(BF16), before_ref[...])
    where_to = earlier + seg_start
    pos1 = col_sum(hot1 * where_to)
    pos2 = col_sum(hot2 * where_to)

    info = jnp.concatenate([pos1, pos2, w1, w2, e1.astype(F32), e2.astype(F32),
                            jnp.zeros((SLAB - 6, tb), F32)], axis=0)
    rows_ref[0] = info[:8]
    cols_ref[0] = info.T
    cnt_ref[0] = jnp.broadcast_to(n_gran, (N_EXPERTS, SLAB))


def _out_router(x, oa, ob, w_o, gain, w_route, b_route, before, tb):
    b, s, d = x.shape
    nb = s // tb
    full = lambda shape: pl.BlockSpec(shape, lambda i, j: (0,) * len(shape))
    tokd = pl.BlockSpec((1, tb, d), lambda i, j: (i, j, 0))
    tokh = pl.BlockSpec((1, tb, MIX_HALF), lambda i, j: (i, j, 0))
    return pl.pallas_call(
        _out_router_kernel,
        grid=(b, nb),
        in_specs=[tokd, tokh, tokh, full(w_o.shape), full((1, d)), full(w_route.shape), full((1, ROUTE_LANES)),
                  full(before.shape)],
        out_specs=[tokd, tokd, pl.BlockSpec((1, 8, tb), lambda i, j: (i, 0, j)),
                   pl.BlockSpec((1, tb, SLAB), lambda i, j: (i, j, 0)),
                   pl.BlockSpec((1, N_EXPERTS, SLAB), lambda i, j: (i * nb + j, 0, 0))],
        out_shape=[jax.ShapeDtypeStruct((b, s, d), F32), jax.ShapeDtypeStruct((b, s, d), BF16),
                   jax.ShapeDtypeStruct((b, 8, s), F32), jax.ShapeDtypeStruct((b, s, SLAB), F32),
                   jax.ShapeDtypeStruct((b * nb, N_EXPERTS, SLAB), F32)],
        compiler_params=pltpu.CompilerParams(dimension_semantics=("parallel", "parallel"),
                                             vmem_limit_bytes=VMEM_LIMIT),
        name="out_router",
    )(x, oa, ob, w_o, gain, w_route, b_route, before)


def _granule_copies(table_ref, block, buf, local_ref, global_ref, sem, to_global, n_gran):
    copies = []
    for g in range(n_gran):
        loc = local_ref.at[buf, pl.ds(g * SLOT_GRAN, SLOT_GRAN)]
        glob = global_ref.at[pl.ds(pl.multiple_of(table_ref[block, g], SLOT_GRAN), SLOT_GRAN)]
        copies.append(pltpu.make_async_copy(loc, glob, sem.at[buf]) if to_global
                      else pltpu.make_async_copy(glob, loc, sem.at[buf]))
    return copies


def _dispatch_kernel(gdst_ref, t_ref, rows_ref, xg_init_ref, xg_ref, xs_scr, sem, *, n_slots):
    del xg_init_ref
    blk = pl.program_id(0)
    last = pl.num_programs(0) - 1
    buf = blk % 2
    tb = t_ref.shape[0]
    n_gran = n_slots // SLOT_GRAN
    info = rows_ref[0]
    slot = lax.broadcasted_iota(jnp.int32, (n_slots, tb), 0).astype(F32)
    onehot = jnp.where((slot == info[0:1]) | (slot == info[1:2]), 1.0, 0.0).astype(BF16)
    xs_scr[buf] = _dot(onehot, t_ref[...]).astype(BF16)
    for copy in _granule_copies(gdst_ref, blk, buf, xs_scr, xg_ref, sem, True, n_gran):
        copy.start()

    @pl.when(blk > 0)
    def _():
        for copy in _granule_copies(gdst_ref, blk - 1, 1 - buf, xs_scr, xg_ref, sem, True, n_gran):
            copy.wait()

    @pl.when(blk == last)
    def _():
        for copy in _granule_copies(gdst_ref, blk, buf, xs_scr, xg_ref, sem, True, n_gran):
            copy.wait()


def _dispatch(granule_dst, t, rows, xg_init, tb, n_slots):
    n, d = t.shape
    nblk = n // tb
    nb = rows.shape[2] // tb
    grid_spec = pltpu.PrefetchScalarGridSpec(
        num_scalar_prefetch=1,
        grid=(nblk,),
        in_specs=[pl.BlockSpec((tb, d), lambda i, gd: (i, 0)),
                  pl.BlockSpec((1, 8, tb), lambda i, gd: (i // nb, 0, i % nb)),
                  pl.BlockSpec(memory_space=pl.ANY)],
        out_specs=pl.BlockSpec(memory_space=pl.ANY),
        scratch_shapes=[pltpu.VMEM((2, n_slots, d), BF16), pltpu.SemaphoreType.DMA((2,))],
    )
    return pl.pallas_call(
        functools.partial(_dispatch_kernel, n_slots=n_slots),
        grid_spec=grid_spec,
        out_shape=jax.ShapeDtypeStruct(xg_init.shape, BF16),
        input_output_aliases={3: 0},
        compiler_params=pltpu.CompilerParams(dimension_semantics=("arbitrary",), vmem_limit_bytes=VMEM_LIMIT),
        name="moe_dispatch",
    )(granule_dst, t, rows, xg_init)


def _experts_kernel(tile_expert_ref, n_tiles_ref, x_ref, wg_ref, wu_ref, wd_ref, y_ref):
    del tile_expert_ref

    @pl.when(pl.program_id(0) < n_tiles_ref[0])
    def _():
        x = x_ref[...]
        act = _silu(_dot(x, wg_ref[0].astype(BF16))) * _dot(x, wu_ref[0].astype(BF16))
        y_ref[...] = _dot(act.astype(BF16), wd_ref[0].astype(BF16)).astype(BF16)


def _experts(tile_expert, n_tiles, xg, w_gate, w_up, w_down, tile):
    rows, d = xg.shape
    used = lambda i, te, nt: jnp.minimum(i, nt[0] - 1)
    wspec = lambda shape: pl.BlockSpec((1,) + shape, lambda i, te, nt: (te[used(i, te, nt)], 0, 0))
    grid_spec = pltpu.PrefetchScalarGridSpec(
        num_scalar_prefetch=2,
        grid=(tile_expert.shape[0],),
        in_specs=[pl.BlockSpec((tile, d), lambda i, te, nt: (used(i, te, nt), 0)),
                  wspec((d, D_EXPERT)), wspec((d, D_EXPERT)), wspec((D_EXPERT, d))],
        out_specs=pl.BlockSpec((tile, d), lambda i, te, nt: (used(i, te, nt), 0)),
    )
    return pl.pallas_call(
        _experts_kernel,
        grid_spec=grid_spec,
        out_shape=jax.ShapeDtypeStruct((rows, d), BF16),
        input_output_aliases={2: 0},
        compiler_params=pltpu.CompilerParams(dimension_semantics=("arbitrary",), vmem_limit_bytes=VMEM_LIMIT),
        name="moe_experts",
    )(tile_expert, n_tiles, xg, w_gate, w_up, w_down)


def _combine_kernel(gsrc_ref, yg_ref, cols_ref, h_ref, gain_ref, o_ref, ys_scr, sem, *, n_slots):
    blk = pl.program_id(0)
    last = pl.num_programs(0) - 1
    buf = blk % 2
    tb = h_ref.shape[0]
    n_gran = n_slots // SLOT_GRAN

    @pl.when(blk == 0)
    def _():
        for copy in _granule_copies(gsrc_ref, 0, 0, ys_scr, yg_ref, sem, False, n_gran):
            copy.start()

    @pl.when(blk < last)
    def _():
        for copy in _granule_copies(gsrc_ref, blk + 1, 1 - buf, ys_scr, yg_ref, sem, False, n_gran):
            copy.start()

    info = cols_ref[...]
    slot = lax.broadcasted_iota(jnp.int32, (tb, n_slots), 1).astype(F32)
    weights = (jnp.where(slot == info[:, 0:1], info[:, 2:3], 0.0)
               + jnp.where(slot == info[:, 1:2], info[:, 3:4], 0.0)).astype(BF16)
    for copy in _granule_copies(gsrc_ref, blk, buf, ys_scr, yg_ref, sem, False, n_gran):
        copy.wait()
    y = _dot(weights, ys_scr[buf])
    o_ref[...] = _rms_scale(h_ref[...] + y) * gain_ref[...]


def _combine(granule_dst, yg, cols, h, gain, tb, n_slots):
    n, d = h.shape
    grid_spec = pltpu.PrefetchScalarGridSpec(
        num_scalar_prefetch=1,
        grid=(n // tb,),
        in_specs=[pl.BlockSpec(memory_space=pl.ANY),
                  pl.BlockSpec((tb, SLAB), lambda i, gd: (i, 0)),
                  pl.BlockSpec((tb, d), lambda i, gd: (i, 0)),
                  pl.BlockSpec((1, d), lambda i, gd: (0, 0))],
        out_specs=pl.BlockSpec((tb, d), lambda i, gd: (i, 0)),
        scratch_shapes=[pltpu.VMEM((2, n_slots, d), BF16), pltpu.SemaphoreType.DMA((2,))],
    )
    return pl.pallas_call(
        functools.partial(_combine_kernel, n_slots=n_slots),
        grid_spec=grid_spec,
        out_shape=jax.ShapeDtypeStruct((n, d), F32),
        compiler_params=pltpu.CompilerParams(dimension_semantics=("arbitrary",), vmem_limit_bytes=VMEM_LIMIT),
        name="moe_combine",
    )(granule_dst, yg, cols, h, gain)


def _regroup_plan(n_gran_be, n_slots, tile, n_tiles_max):
    cnt = n_gran_be.astype(jnp.int32)
    region = (jnp.sum(cnt, axis=0) + tile - 1) // tile * tile
    region_end = jnp.cumsum(region)
    first_row = (region_end - region)[None, :] + jnp.cumsum(cnt, axis=0) - cnt
    seg_end = jnp.cumsum(cnt, axis=1)
    seg_start = seg_end - cnt
    g_row = jnp.arange(n_slots // SLOT_GRAN, dtype=jnp.int32) * SLOT_GRAN
    owned = ((g_row[None, :, None] >= seg_start[:, None, :]) & (g_row[None, :, None] < seg_end[:, None, :]))
    in_use = g_row[None, :] < seg_end[:, -1:]
    row = g_row[None, :] + jnp.sum(jnp.where(owned, (first_row - seg_start)[:, None, :], 0), axis=-1)
    zero_row = n_tiles_max * tile
    spill_row = zero_row + SLOT_GRAN + (jnp.arange(cnt.shape[0], dtype=jnp.int32) % 2)[:, None] * n_slots + g_row
    dispatch_dst = jnp.where(in_use, row, spill_row).astype(jnp.int32)
    combine_src = jnp.where(in_use, row, zero_row).astype(jnp.int32)
    tile_row = jnp.arange(n_tiles_max, dtype=jnp.int32) * tile
    tile_expert = jnp.minimum(jnp.sum(tile_row[:, None] >= region_end[None, :], axis=-1), N_EXPERTS - 1)
    n_tiles = (region_end[-1] // tile).astype(jnp.int32)[None]
    return dispatch_dst, combine_src, tile_expert.astype(jnp.int32), n_tiles


def _rope_tables(first, length):
    pos = jnp.arange(first, first + length, dtype=F32)
    inv_freq = ROPE_THETA ** (-jnp.arange(0, HEAD_DIM, 2, dtype=F32) / HEAD_DIM)
    ang = pos[:, None] * inv_freq[None, :]
    cos, sin = jnp.cos(ang), jnp.sin(ang)
    zero = jnp.zeros_like(sin)
    cos_t = jnp.tile(cos, (1, 4))
    sina_t = jnp.tile(jnp.concatenate([-sin, zero], axis=1), (1, 2))
    sinb_t = jnp.tile(jnp.concatenate([zero, sin], axis=1), (1, 2))
    return cos_t, sina_t, sinb_t


def _block(total, want):
    blk = min(total, want)
    assert total % blk == 0, (total, blk)
    return blk


def kernel(x, meta, norm_mix, w_in, lambda_q1, lambda_k1, lambda_q2, lambda_k2, diff_norm, conv_w, a_log, dt_bias,
           gdn_norm, w_out, norm_ffn, w_group, b_group, w_router, b_router, w_gate, w_up, w_down, norm_final):
    b, s, d = x.shape
    assert d == D_MODEL and meta.shape == (N_META, D_MODEL) and s % CHUNK == 0
    assert norm_mix.shape[0] == 1, "single-layer block"
    l = 0

    w = w_in[l]
    w_a = w[:, :3 * MIX_HALF].astype(BF16)
    w_b = w[:, 3 * MIX_HALF:6 * MIX_HALF].astype(BF16)
    w_z = w[:, 6 * MIX_HALF:7 * MIX_HALF].astype(BF16)
    c_ab = 7 * MIX_HALF
    lane_pad = lambda a: jnp.pad(a, [(0, 0)] * (a.ndim - 1) + [(0, SLAB - a.shape[-1])])
    w_g = lane_pad(w[:, c_ab:c_ab + 2 * GDN_HEADS]).astype(BF16)
    alog_rep = lane_pad(a_log[l])[None]
    dtb_rep = lane_pad(dt_bias[l])[None]
    gn_rep = jnp.tile(gdn_norm[l], GDN_HEADS)[None]
    g512 = (jnp.arange(MIX_HALF)[:, None] // HEAD_DIM == jnp.arange(MIX_HALF)[None, :] // HEAD_DIM).astype(BF16)
    gain_mix = norm_mix[l][None]
    w_o = w_out[l].astype(BF16)
    route_pad = lambda g, e: jnp.concatenate(
        [g, jnp.zeros(g.shape[:-1] + (EXPERT_ROW0 - N_GROUPS,), F32), e,
         jnp.zeros(g.shape[:-1] + (ROUTE_LANES - EXPERT_ROW0 - N_EXPERTS,), F32)], axis=-1)
    w_route_f = route_pad(w_group[l], w_router[l])
    w_route_hi = w_route_f.astype(BF16)
    w_route = jnp.concatenate([w_route_hi, (w_route_f - w_route_hi.astype(F32)).astype(BF16)], axis=1)
    b_route = route_pad(b_group[l], b_router[l])[None]
    lam_vecs = [v[l][None] for v in (lambda_q1, lambda_k1, lambda_q2, lambda_k2)]
    rope_meta = _rope_tables(0, N_META)
    rope_frames = _rope_tables(N_META, s)

    meta3 = meta[None]
    _, ka_m, va_m = _proj_attn(meta3, gain_mix, w_a, *rope_meta, N_META)
    zero_halo = jnp.zeros((1, 1, N_META, d), F32)
    gdn_m = _proj_gdn(meta3, zero_halo, gain_mix, w_b, w_z, w_g, conv_w[l], alog_rep, dtb_rep, g512, N_META)
    pad_m = lambda a: jnp.pad(a, ((0, 0), (0, CHUNK - N_META), (0, 0)))
    s_zero = jnp.zeros((GDN_HEADS // GDN_GROUP, GDN_GW, GDN_GW), F32)
    _, s_meta = _gated_delta(*[pad_m(a) for a in gdn_m], gn_rep, g512, s_zero, CHUNK)

    tb_a = _block(s, PROJ_ATTN_ROWS)
    qa, ka, va = _proj_attn(x, gain_mix, w_a, *rope_frames, tb_a)
    tb_g = _block(s, PROJ_GDN_ROWS)
    nb_g = s // tb_g
    tails = x.reshape(b, nb_g, tb_g, d)[:, :-1, tb_g - N_META:, :]
    halo = jnp.concatenate([jnp.broadcast_to(meta[None, None], (b, 1, N_META, d)), tails], axis=1)
    gdn_f = _proj_gdn(x, halo, gain_mix, w_b, w_z, w_g, conv_w[l], alog_rep, dtb_rep, g512, tb_g)

    oa = _diff_attention(qa, ka, va, ka_m, va_m, *lam_vecs, diff_norm[l][None],
                         _block(s, ATTN_QUERY_ROWS), ATTN_KEY_BLOCK)
    ob, _ = _gated_delta(*gdn_f, gn_rep, g512, s_meta[0], _block(s, GDN_ROWS))

    tb_r = _block(s, MOE_BLOCK)
    tok = jnp.arange(tb_r)
    before = (tok[:, None] < tok[None, :]).astype(BF16)
    h1, t, rows, cols, cnt = _out_router(x, oa, ob, w_o, norm_ffn[l][None], w_route, b_route, before, tb_r)

    n = b * s
    nblk = n // tb_r
    n_slots = 2 * tb_r + N_EXPERTS * SLOT_GRAN
    rows_max = 2 * n + (SLOT_GRAN - 1) * N_EXPERTS * nblk + (MOE_TILE - 1) * N_EXPERTS
    n_tiles_max = -(-rows_max // MOE_TILE)
    dispatch_dst, combine_src, tile_expert, n_tiles = _regroup_plan(cnt[:, :, 0], n_slots, MOE_TILE, n_tiles_max)
    buf_rows = n_tiles_max * MOE_TILE + SLOT_GRAN + 2 * n_slots
    xg = _dispatch(dispatch_dst, t.reshape(n, d), rows, jnp.zeros((buf_rows, d), BF16), tb_r, n_slots)
    yg = _experts(tile_expert, n_tiles, xg, w_gate[l], w_up[l], w_down[l], MOE_TILE)
    out = _combine(combine_src, yg, cols.reshape(n, SLAB), h1.reshape(n, d), norm_final[None], tb_r, n_slots)
    return out.reshape(b, s, d)
```
